```python
import math
import jax, jax.numpy as jnp
from jax import lax
import numpy as np

D_MODEL = 1024
BATCH = 8
SEQ = 4096
DEPTH = 4

GRID_W = 64
CTX_LEN = 256
SSD_HEAD_DIM = 64
SSD_INNER = D_MODEL
SSD_HEADS = SSD_INNER // SSD_HEAD_DIM
SSD_GROUPS = 4
SSD_STATE = 128
SSD_GN = SSD_GROUPS * SSD_STATE
SSD_CONV_W = 5
SSD_CONV_DIM = SSD_INNER + 2 * SSD_GN
SSD_CHUNK = 128
NA_HEAD_DIM = 64
NA_WIDTH = D_MODEL // 2
NA_HEADS = NA_WIDTH // NA_HEAD_DIM
NA_WIN_ROWS = 8
NA_WIN_COLS = 16
DIFF_SUB_DIM = 64
DIFF_HEADS = D_MODEL // (2 * DIFF_SUB_DIM)
Q_BLOCK = 128
ROPE_BASE = 10000.0
D_FF = ((8 * D_MODEL // 3 + 255) // 256) * 256
FFN_CONV_W = 3
ALPHA = (2.0 * DEPTH) ** 0.25
BETA = (8.0 * DEPTH) ** -0.25
LN_EPS = 1e-5
RMS_EPS = 1e-5
N_EVEN = (DEPTH + 1) // 2
N_ODD = DEPTH // 2
HYB_SPLITS = [SSD_INNER, SSD_INNER + SSD_CONV_DIM, SSD_INNER + SSD_CONV_DIM + 2 * SSD_HEADS, SSD_INNER + SSD_CONV_DIM + 2 * SSD_HEADS + NA_WIDTH, SSD_INNER + SSD_CONV_DIM + 2 * SSD_HEADS + 2 * NA_WIDTH]
HYB_IN = SSD_INNER + SSD_CONV_DIM + 2 * SSD_HEADS + 3 * NA_WIDTH
HYB_OUT = SSD_INNER + NA_WIDTH

kernel_name = 'hybrid_ssd_natten_diffattn_dit'


def layer_norm(h, g, b):
    hf = h.astype(jnp.float32)
    mu = jnp.mean(hf, -1, keepdims=True)
    var = jnp.mean(jnp.square(hf - mu), -1, keepdims=True)
    return ((hf - mu) * lax.rsqrt(var + LN_EPS) * g.astype(jnp.float32) + b.astype(jnp.float32)).astype(h.dtype)


def rms_norm(h, w):
    hf = h.astype(jnp.float32)
    return (hf * lax.rsqrt(jnp.mean(jnp.square(hf), -1, keepdims=True) + RMS_EPS) * w.astype(jnp.float32)).astype(h.dtype)


def modulate(h, shift, scale):
    return h * (1.0 + scale) + shift


def dwconv_centred(h, w, b):
    k = w.shape[0]
    y = lax.conv_general_dilated(h, w[:, None, :].astype(h.dtype), window_strides=(1,), padding=[(k // 2, k // 2)], dimension_numbers=('NWC', 'WIO', 'NWC'), feature_group_count=h.shape[-1])
    return y + b


def axial_rope(n_tokens, dim):
    t = jnp.arange(n_tokens)
    row = (t // GRID_W).astype(jnp.float32)
    col = (t % GRID_W).astype(jnp.float32)
    n_freq = dim // 4
    inv = ROPE_BASE ** (-jnp.arange(n_freq, dtype=jnp.float32) / n_freq)
    ang = jnp.concatenate([row[:, None] * inv, col[:, None] * inv], axis=-1)
    return jnp.cos(ang), jnp.sin(ang)


def apply_rope(h, cos, sin):
    h1, h2 = jnp.split(h.astype(jnp.float32), 2, axis=-1)
    cs = cos[None, :, None, :]
    sn = sin[None, :, None, :]
    return jnp.concatenate([h1 * cs - h2 * sn, h1 * sn + h2 * cs], axis=-1).astype(h.dtype)


def segsum(a):
    t = a.shape[-1]
    cs = jnp.cumsum(a, axis=-1)
    diff = cs[..., :, None] - cs[..., None, :]
    return jnp.where(jnp.tril(jnp.ones((t, t), dtype=bool)), diff, -jnp.inf)


def ssd_scan(xdt, a, bm, cm, h0):
    b, n, h, p = xdt.shape
    g, ns = bm.shape[2], bm.shape[3]
    r = h // g
    nc = n // SSD_CHUNK
    x = xdt.reshape(b, nc, SSD_CHUNK, g, r, p)
    a = a.reshape(b, nc, SSD_CHUNK, g, r).transpose(0, 1, 3, 4, 2)
    bc = bm.reshape(b, nc, SSD_CHUNK, g, ns)
    cc = cm.reshape(b, nc, SSD_CHUNK, g, ns)
    a_cs = jnp.cumsum(a, axis=-1)
    decay = jnp.exp(segsum(a))
    cb = jnp.einsum('bclgn,bcsgn->bcgls', cc, bc)
    y_diag = jnp.einsum('bcgrls,bcsgrp->bclgrp', cb[:, :, :, None] * decay, x)
    decay_states = jnp.exp(a_cs[..., -1:] - a_cs).transpose(0, 1, 4, 2, 3)
    states = jnp.einsum('bcsgn,bcsgrp->bcgrpn', bc, x * decay_states[..., None])
    chunk_decay = jnp.exp(a_cs[..., -1])

    def step(hs, inp):
        s_c, d_c = inp
        return d_c[..., None, None] * hs + s_c, hs

    final, prev = lax.scan(step, h0, (jnp.swapaxes(states, 0, 1), jnp.swapaxes(chunk_decay, 0, 1)))
    prev = jnp.swapaxes(prev, 0, 1)
    state_decay = jnp.exp(a_cs).transpose(0, 1, 4, 2, 3)
    y_off = jnp.einsum('bclgn,bcgrpn->bclgrp', cc, prev) * state_decay[..., None]
    return (y_diag + y_off).reshape(b, n, h, p), final


def ssd_branch(z_c, xbc_c, dt_c, z_x, xbc_x, dt_x, conv_w, conv_b, a_log, dt_bias, d_skip, norm_w, with_ctx):
    out_dtype = z_x.dtype

    def prep(xbc):
        u = jax.nn.silu(dwconv_centred(xbc, conv_w, conv_b)).astype(jnp.float32)
        b, n, _ = u.shape
        xs = u[..., :SSD_INNER].reshape(b, n, SSD_HEADS, SSD_HEAD_DIM)
        bm = u[..., SSD_INNER:SSD_INNER + SSD_GN].reshape(b, n, SSD_GROUPS, SSD_STATE)
        cm = u[..., SSD_INNER + SSD_GN:].reshape(b, n, SSD_GROUPS, SSD_STATE)
        return xs, bm, cm

    xs_c, b_c, c_c = prep(xbc_c)
    xs_x, b_x, c_x = prep(xbc_x)
    h0 = jnp.zeros((xs_x.shape[0], SSD_GROUPS, SSD_HEADS // SSD_GROUPS, SSD_HEAD_DIM, SSD_STATE), jnp.float32)
    ys_c, ys_x = [], []
    for d in range(2):
        flip = (lambda t: jnp.flip(t, axis=1)) if d == 1 else (lambda t: t)
        a = -jnp.exp(a_log[d].astype(jnp.float32))
        dtc = jax.nn.softplus(dt_c[:, :, d].astype(jnp.float32) + dt_bias[d].astype(jnp.float32))
        dtx = jax.nn.softplus(dt_x[:, :, d].astype(jnp.float32) + dt_bias[d].astype(jnp.float32))
        dsk = d_skip[d].astype(jnp.float32)[:, None]
        yc, hc = ssd_scan(flip(xs_c * dtc[..., None]), flip(dtc * a), flip(b_c), flip(c_c), h0)
        yx, _ = ssd_scan(flip(xs_x * dtx[..., None]), flip(dtx * a), flip(b_x), flip(c_x), hc)
        ys_c.append(flip(yc) + dsk * xs_c)
        ys_x.append(flip(yx) + dsk * xs_x)

    def gate_norm(y, z):
        b, n = y.shape[:2]
        y = y.reshape(b, n, SSD_INNER) * jax.nn.silu(z.astype(jnp.float32))
        return rms_norm(y, norm_w).astype(out_dtype)

    y_x = gate_norm(ys_x[0] + ys_x[1], z_x)
    y_c = gate_norm(ys_c[0] + ys_c[1], z_c) if with_ctx else None
    return y_x, y_c


def neighbourhood_attention(q, k, v, k_ctx, v_ctx, rpb):
    b, n, h, dh = q.shape
    rows = n // GRID_W
    wr = min(NA_WIN_ROWS, rows)
    wc = NA_WIN_COLS
    scale = dh ** -0.5
    qg = q.reshape(b, rows, GRID_W, h, dh)
    kg = k.reshape(b, rows, GRID_W, h, dh)
    vg = v.reshape(b, rows, GRID_W, h, dh)
    col = jnp.arange(GRID_W)
    c0 = jnp.clip(col - wc // 2, 0, GRID_W - wc)
    col_idx = c0[:, None] + jnp.arange(wc)[None, :]
    dc = col_idx - col[:, None] + (NA_WIN_COLS - 1)

    def row_block(r):
        r0 = jnp.clip(r - wr // 2, 0, rows - wr)
        q_r = lax.dynamic_index_in_dim(qg, r, axis=1, keepdims=False)
        k_r = lax.dynamic_slice_in_dim(kg, r0, wr, axis=1)[:, :, col_idx]
        v_r = lax.dynamic_slice_in_dim(vg, r0, wr, axis=1)[:, :, col_idx]
        dr = r0 + jnp.arange(wr) - r + (NA_WIN_ROWS - 1)
        bias = rpb[:, dr[None, :, None], dc[:, None, :]]
        s_loc = jnp.einsum('bqhd,bjqkhd->bhqjk', q_r, k_r) * scale + bias
        s_ctx = jnp.einsum('bqhd,bkhd->bhqk', q_r, k_ctx) * scale
        s = jnp.concatenate([s_loc.reshape(b, h, GRID_W, wr * wc), s_ctx], axis=-1)
        p = jax.nn.softmax(s.astype(jnp.float32), axis=-1).astype(v.dtype)
        p_loc = p[..., :wr * wc].reshape(b, h, GRID_W, wr, wc)
        p_ctx = p[..., wr * wc:]
        return jnp.einsum('bhqjk,bjqkhd->bqhd', p_loc, v_r) + jnp.einsum('bhqk,bkhd->bqhd', p_ctx, v_ctx)

    out = lax.map(row_block, jnp.arange(rows))
    return out.transpose(1, 0, 2, 3, 4).reshape(b, n, h * dh)


def dense_attention(q, k, v):
    b, n, h, dh = q.shape
    s = jnp.einsum('bqhd,bkhd->bhqk', q, k) * dh ** -0.5
    p = jax.nn.softmax(s.astype(jnp.float32), axis=-1).astype(v.dtype)
    return jnp.einsum('bhqk,bkhd->bqhd', p, v).reshape(b, n, h * dh)


def diff_attend(q, k, v, lam):
    b, lq = q.shape[:2]
    s = jnp.einsum('bqhd,bkhd->bhqk', q, k) * DIFF_SUB_DIM ** -0.5
    p = jax.nn.softmax(s.astype(jnp.float32), axis=-1).reshape(b, DIFF_HEADS, 2, lq, k.shape[1])
    a = (p[:, :, 0] - lam * p[:, :, 1]).astype(v.dtype)
    return jnp.einsum('bhqk,bkhd->bqhd', a, v)


def hybrid_mixer(ux, uc, w_in, conv_w, conv_b, a_log, dt_bias, d_skip, norm_w, rpb, w_out, with_ctx):
    def parts(pr):
        b, n, _ = pr.shape
        z, xbc, dt, q, k, v = jnp.split(pr, HYB_SPLITS, axis=-1)
        heads = lambda t: t.reshape(b, n, NA_HEADS, NA_HEAD_DIM)
        return z, xbc, dt.reshape(b, n, 2, SSD_HEADS), heads(q), heads(k), heads(v)

    z_x, xbc_x, dt_x, q_x, k_x, v_x = parts(ux @ w_in)
    z_c, xbc_c, dt_c, q_c, k_c, v_c = parts(uc @ w_in)
    y_x, y_c = ssd_branch(z_c, xbc_c, dt_c, z_x, xbc_x, dt_x, conv_w, conv_b, a_log, dt_bias, d_skip, norm_w, with_ctx)
    a_x = neighbourhood_attention(q_x, k_x, v_x, k_c, v_c, rpb)
    ox = jnp.concatenate([y_x, a_x], axis=-1) @ w_out
    if not with_ctx:
        return ox, None
    a_c = dense_attention(q_c, k_c, v_c)
    oc = jnp.concatenate([y_c, a_c], axis=-1) @ w_out
    return ox, oc


def diff_mixer(ux, uc, w_in, lam_p, subln_w, w_out, lam_init, rope_cos, rope_sin, with_ctx):
    b, n, _ = ux.shape
    m = uc.shape[1]
    h2 = 2 * DIFF_HEADS
    q_x, k_x, v_x = jnp.split(ux @ w_in, 3, axis=-1)
    q_x = apply_rope(q_x.reshape(b, n, h2, DIFF_SUB_DIM), rope_cos, rope_sin)
    k_x = apply_rope(k_x.reshape(b, n, h2, DIFF_SUB_DIM), rope_cos, rope_sin)
    v_x = v_x.reshape(b, n, DIFF_HEADS, 2 * DIFF_SUB_DIM)
    if with_ctx:
        q_c, k_c, v_c = jnp.split(uc @ w_in, 3, axis=-1)
    else:
        k_c, v_c = jnp.split(uc @ w_in[:, D_MODEL:], 2, axis=-1)
    k_c = k_c.reshape(b, m, h2, DIFF_SUB_DIM)
    v_c = v_c.reshape(b, m, DIFF_HEADS, 2 * DIFF_SUB_DIM)
    lp = lam_p.astype(jnp.float32)
    lam = jnp.exp(jnp.sum(lp[0] * lp[1])) - jnp.exp(jnp.sum(lp[2] * lp[3])) + lam_init

    def finish(o):
        o = rms_norm(o, subln_w) * (1.0 - lam_init)
        return o.reshape(o.shape[0], o.shape[1], DIFF_HEADS * 2 * DIFF_SUB_DIM) @ w_out

    k_all = jnp.concatenate([k_x, k_c], axis=1)
    v_all = jnp.concatenate([v_x, v_c], axis=1)
    nb = n // Q_BLOCK
    q_blocks = jnp.swapaxes(q_x.reshape(b, nb, Q_BLOCK, h2, DIFF_SUB_DIM), 0, 1)
    o_x = lax.map(lambda qb: diff_attend(qb, k_all, v_all, lam), q_blocks)
    o_x = jnp.swapaxes(o_x, 0, 1).reshape(b, n, DIFF_HEADS, 2 * DIFF_SUB_DIM)
    ox = finish(o_x)
    if not with_ctx:
        return ox, None
    q_c = q_c.reshape(b, m, h2, DIFF_SUB_DIM)
    oc = finish(diff_attend(q_c, k_c, v_c, lam))
    return ox, oc


def conv_ffn(u, w_up, conv_w, conv_b, w_down):
    hdn = dwconv_centred(u @ w_up, conv_w, conv_b)
    val, gate = jnp.split(hdn, 2, axis=-1)
    return (jax.nn.silu(gate) * val) @ w_down


def setup_inputs(seed: int = 0) -> dict:
    key = jax.random.key(seed)
    keys = iter(jax.random.split(key, 40))

    def nrm(shape, std):
        return jax.random.normal(next(keys), shape, jnp.float32) * std

    D = D_MODEL
    x = nrm((BATCH, SEQ, D), 1.0)
    c = nrm((BATCH, D), 1.0)
    ctx = nrm((BATCH, CTX_LEN, D), 1.0)
    c_ctx = nrm((D,), 1.0)
    ada_w = nrm((DEPTH, D, 6 * D), D ** -0.5)
    ada_b = nrm((DEPTH, 6 * D), 0.02)
    ln1_g = 1.0 + nrm((DEPTH, D), 0.02)
    ln1_b = nrm((DEPTH, D), 0.02)
    ln2_g = 1.0 + nrm((DEPTH, D), 0.02)
    ln2_b = nrm((DEPTH, D), 0.02)
    ffn_w_up = nrm((DEPTH, D, 2 * D_FF), D ** -0.5)
    ffn_conv_w = nrm((DEPTH, FFN_CONV_W, 2 * D_FF), FFN_CONV_W ** -0.5)
    ffn_conv_b = nrm((DEPTH, 2 * D_FF), 0.02)
    ffn_w_down = nrm((DEPTH, D_FF, D), BETA * D_FF ** -0.5)
    hyb_w_in = nrm((N_EVEN, D, HYB_IN), D ** -0.5)
    ssd_conv_w = nrm((N_EVEN, SSD_CONV_W, SSD_CONV_DIM), SSD_CONV_W ** -0.5)
    ssd_conv_b = nrm((N_EVEN, SSD_CONV_DIM), 0.02)
    ssd_a_log = jnp.log(jax.random.uniform(next(keys), (N_EVEN, 2, SSD_HEADS), jnp.float32, 1.0, 16.0))
    dt0 = jnp.exp(jax.random.uniform(next(keys), (N_EVEN, 2, SSD_HEADS), jnp.float32, math.log(1e-3), math.log(1e-1)))
    ssd_dt_bias = dt0 + jnp.log(-jnp.expm1(-dt0))
    ssd_d = 1.0 + nrm((N_EVEN, 2, SSD_HEADS), 0.1)
    ssd_norm_w = 1.0 + nrm((N_EVEN, SSD_INNER), 0.02)
    na_rpb = nrm((N_EVEN, NA_HEADS, 2 * NA_WIN_ROWS - 1, 2 * NA_WIN_COLS - 1), 0.02)
    hyb_w_out = nrm((N_EVEN, HYB_OUT, D), BETA * HYB_OUT ** -0.5)
    diff_w_in = nrm((N_ODD, D, 3 * D), D ** -0.5)
    diff_lambda = nrm((N_ODD, 4, DIFF_SUB_DIM), 0.1)
    diff_subln_w = 1.0 + nrm((N_ODD, 2 * DIFF_SUB_DIM), 0.02)
    diff_w_out = nrm((N_ODD, D, D), BETA * D ** -0.5)
    return {'x': x, 'c': c, 'ctx': ctx, 'c_ctx': c_ctx, 'ada_w': ada_w, 'ada_b': ada_b,
            'ln1_g': ln1_g, 'ln1_b': ln1_b, 'ln2_g': ln2_g, 'ln2_b': ln2_b,
            'ffn_w_up': ffn_w_up, 'ffn_conv_w': ffn_conv_w, 'ffn_conv_b': ffn_conv_b, 'ffn_w_down': ffn_w_down,
            'hyb_w_in': hyb_w_in, 'ssd_conv_w': ssd_conv_w, 'ssd_conv_b': ssd_conv_b, 'ssd_a_log': ssd_a_log,
            'ssd_dt_bias': ssd_dt_bias, 'ssd_d': ssd_d, 'ssd_norm_w': ssd_norm_w, 'na_rpb': na_rpb,
            'hyb_w_out': hyb_w_out, 'diff_w_in': diff_w_in, 'diff_lambda': diff_lambda,
            'diff_subln_w': diff_subln_w, 'diff_w_out': diff_w_out}


def reference(x, c, ctx, c_ctx, ada_w, ada_b, ln1_g, ln1_b, ln2_g, ln2_b, ffn_w_up, ffn_conv_w, ffn_conv_b, ffn_w_down, hyb_w_in, ssd_conv_w, ssd_conv_b, ssd_a_log, ssd_dt_bias, ssd_d, ssd_norm_w, na_rpb, hyb_w_out, diff_w_in, diff_lambda, diff_subln_w, diff_w_out):
    n_lat = x.shape[1]
    rope_cos, rope_sin = axial_rope(n_lat, DIFF_SUB_DIM)
    cond_x = jax.nn.silu(c)
    cond_c = jax.nn.silu(c_ctx)
    hx, hc = x, ctx
    for i in range(DEPTH):
        last = i == DEPTH - 1
        j = i // 2
        mx = jnp.split((cond_x @ ada_w[i] + ada_b[i])[:, None, :], 6, axis=-1)
        mc = jnp.split((cond_c @ ada_w[i] + ada_b[i])[None, None, :], 6, axis=-1)
        ux = modulate(hx, mx[0], mx[1])
        uc = modulate(hc, mc[0], mc[1])
        if i % 2 == 0:
            ox, oc = hybrid_mixer(ux, uc, hyb_w_in[j], ssd_conv_w[j], ssd_conv_b[j], ssd_a_log[j], ssd_dt_bias[j], ssd_d[j], ssd_norm_w[j], na_rpb[j], hyb_w_out[j], not last)
        else:
            lam_init = 0.8 - 0.6 * math.exp(-0.3 * i)
            ox, oc = diff_mixer(ux, uc, diff_w_in[j], diff_lambda[j], diff_subln_w[j], diff_w_out[j], lam_init, rope_cos, rope_sin, not last)
        hx = layer_norm(ALPHA * hx + mx[2] * ox, ln1_g[i], ln1_b[i])
        fx = conv_ffn(modulate(hx, mx[3], mx[4]), ffn_w_up[i], ffn_conv_w[i], ffn_conv_b[i], ffn_w_down[i])
        hx = layer_norm(ALPHA * hx + mx[5] * fx, ln2_g[i], ln2_b[i])
        if not last:
            hc = layer_norm(ALPHA * hc + mc[2] * oc, ln1_g[i], ln1_b[i])
            fc = conv_ffn(modulate(hc, mc[3], mc[4]), ffn_w_up[i], ffn_conv_w[i], ffn_conv_b[i], ffn_w_down[i])
            hc = layer_norm(ALPHA * hc + mc[5] * fc, ln2_g[i], ln2_b[i])
    return hx
```

```python
import functools
import math

import numpy as np
import jax
import jax.numpy as jnp
from jax import lax
from jax.experimental import pallas as pl
from jax.experimental.pallas import tpu as pltpu

F32 = jnp.float32
BF16 = jnp.bfloat16

D_MODEL = 1024
DEPTH = 4
GRID_W = 64
SSD_P = 64
SSD_H = 16
SSD_G = 4
SSD_R = SSD_H // SSD_G
SSD_N = 128
SSD_INNER = SSD_H * SSD_P
SSD_GN = SSD_G * SSD_N
SSD_CONV_W = 5
SSD_CONV_DIM = SSD_INNER + 2 * SSD_GN
SSD_CHUNK = 128
NA_HEADS = 8
NA_DH = 64
NA_WIDTH = NA_HEADS * NA_DH
NA_WIN_ROWS = 8
NA_WIN_COLS = 16
DIFF_HEADS = 8
DIFF_SUB = 64
ROPE_BASE = 10000.0
D_FF = 2816
FFN_CH = 256
ALPHA = (2.0 * DEPTH) ** 0.25
LN_EPS = 1e-5
RMS_EPS = 1e-5

LANES = 128
TM = 256
HALO_F32 = 8
HALO_BF16 = 16
VMEM_LIMIT = 56 * 1024 * 1024
HYB_MAIN = SSD_CONV_DIM + SSD_INNER + 3 * NA_WIDTH
HYB_NCH = 512


def _params(sem, vmem=VMEM_LIMIT):
    return pltpu.CompilerParams(dimension_semantics=sem, vmem_limit_bytes=vmem)


def _resident(shape):
    nd = len(shape)
    return pl.BlockSpec(shape, lambda *_: (0,) * nd, pipeline_mode=pl.Buffered(1))


def _dot(a, b):
    return jnp.dot(a, b, preferred_element_type=F32)


def _dot_nt(a, b):
    return lax.dot_general(a, b, (((1,), (1,)), ((), ())), preferred_element_type=F32)


def _silu(v):
    return v * jax.nn.sigmoid(v)


def _layer_norm(r, g, b):
    mu = jnp.mean(r, axis=-1, keepdims=True)
    xc = r - mu
    var = jnp.mean(xc * xc, axis=-1, keepdims=True)
    return xc * lax.rsqrt(var + LN_EPS) * g + b


def _mod_spec(nxt, nb):
    return pl.BlockSpec((1, 6, D_MODEL), lambda b, i: (jnp.where(i >= nxt, nb, b), 0, 0))


def _ada_kernel(c_ref, w_ref, b_ref, o_ref):
    s = _silu(c_ref[...]).astype(BF16)
    o_ref[0] = _dot(s, w_ref[0].astype(BF16)) + b_ref[0]


def _ada_mod(cond, ada_w, ada_b):
    rows = cond.shape[0]
    n = ada_w.shape[-1]
    tn = n // 4
    return pl.pallas_call(
        _ada_kernel,
        grid=(DEPTH, n // tn),
        in_specs=[pl.BlockSpec((rows, D_MODEL), lambda l, j: (0, 0)),
                  pl.BlockSpec((1, D_MODEL, tn), lambda l, j: (l, 0, j)),
                  pl.BlockSpec((1, 1, tn), lambda l, j: (l, 0, j))],
        out_specs=pl.BlockSpec((1, rows, tn), lambda l, j: (l, 0, j)),
        out_shape=jax.ShapeDtypeStruct((DEPTH, rows, n), F32),
        compiler_params=_params(("parallel", "parallel")),
    )(cond, ada_w, ada_b.reshape(DEPTH, 1, n))


def _hyb_proj_kernel(h_ref, mod_ref, w_ref, wdt_ref, o_ref, dt_ref):
    m = mod_ref[0]
    u = (h_ref[0] * (1.0 + m[1:2, :]) + m[0:1, :]).astype(BF16)
    for n0 in range(0, HYB_MAIN, HYB_NCH):
        o_ref[0, :, n0:n0 + HYB_NCH] = _dot(u, w_ref[:, n0:n0 + HYB_NCH]).astype(BF16)
    dt_ref[0] = _dot(u, wdt_ref[...])


def _hyb_proj(h, mod, w_main, w_dt, nxt):
    nb, t, _ = h.shape
    nt = t // TM
    return pl.pallas_call(
        _hyb_proj_kernel,
        grid=(nb, nt),
        in_specs=[pl.BlockSpec((1, TM, D_MODEL), lambda b, i: (b, i, 0)),
                  _mod_spec(nxt, nb),
                  _resident(w_main.shape),
                  _resident(w_dt.shape)],
        out_specs=[pl.BlockSpec((1, TM, HYB_MAIN), lambda b, i: (b, i, 0)),
                   pl.BlockSpec((1, TM, 2 * LANES), lambda b, i: (b, i, 0))],
        out_shape=[jax.ShapeDtypeStruct((nb, t, HYB_MAIN), BF16),
                   jax.ShapeDtypeStruct((nb, t, 2 * LANES), F32)],
        compiler_params=_params(("parallel", "parallel")),
    )(h, mod, w_main, w_dt)


def _ssd_chunk_index(d, j, nxc, nc):
    return jnp.where(d == 0, (j + nxc) % nc, nc - 1 - j)


def _ssd_kernel(xm_ref, xp_ref, xn_ref, dt_ref, cw_ref, cb_ref, arow_ref, brow_ref, dsk_ref, e_ref,
                y_ref, ext_ref, st_ref, *, nxc, nc):
    d = pl.program_id(1)
    j = pl.program_id(2)
    c = _ssd_chunk_index(d, j, nxc, nc)
    q = SSD_CHUNK

    @pl.when(j == 0)
    def _():
        st_ref[...] = jnp.zeros_like(st_ref)

    seg_start = jnp.logical_or(c == 0, c == nxc)
    seg_end = jnp.logical_or(c == nxc - 1, c == nc - 1)
    ext_ref[0:HALO_BF16, :] = jnp.where(seg_start, 0.0, xp_ref[0].astype(F32))
    ext_ref[HALO_BF16:HALO_BF16 + q, :] = xm_ref[0].astype(F32)
    ext_ref[HALO_BF16 + q:, :] = jnp.where(seg_end, 0.0, xn_ref[0].astype(F32))
    acc = cb_ref[...]
    for k in range(SSD_CONV_W):
        acc = acc + cw_ref[k:k + 1, :] * ext_ref[pl.ds(HALO_BF16 - SSD_CONV_W // 2 + k, q), :]
    u = _silu(acc)
    xs = u[:, :SSD_INNER]

    dtr = dt_ref[0] + brow_ref[0]
    dtv = jnp.maximum(dtr, 0.0) + jnp.log1p(jnp.exp(-jnp.abs(dtr)))
    adt = dtv * arow_ref[0]
    ri = lax.broadcasted_iota(jnp.int32, (q, q), 0)
    ci = lax.broadcasted_iota(jnp.int32, (q, q), 1)
    tri = jnp.where(d == 0, ri - ci, ci - ri) >= 0
    cs = jnp.dot(tri.astype(F32), adt, precision=lax.Precision.HIGHEST, preferred_element_type=F32)
    cs_t = cs.T
    dt_t = dtv.T
    tot = jnp.where(d == 0, cs[q - 1:q, :], cs[0:1, :])
    tot_t = jnp.where(d == 0, cs_t[:, q - 1:q], cs_t[:, 0:1])
    w_t = jnp.exp(tot_t - cs_t) * dt_t
    dec_row = jnp.dot(jnp.broadcast_to(jnp.exp(tot), (8, LANES)), e_ref[...],
                      precision=lax.Precision.HIGHEST, preferred_element_type=F32)[0:1]

    lane = lax.broadcasted_iota(jnp.int32, (q, LANES), 1)
    lo = lane < SSD_P
    dsk = dsk_ref[0]
    for g in range(SSD_G):
        bm = u[:, SSD_INNER + g * SSD_N:SSD_INNER + (g + 1) * SSD_N]
        cm = u[:, SSD_INNER + SSD_GN + g * SSD_N:SSD_INNER + SSD_GN + (g + 1) * SSD_N]
        cmb = cm.astype(BF16)
        cb = _dot_nt(cmb, bm.astype(BF16))
        bm_t = bm.T
        s_prev = st_ref[g]
        y_off = _dot(cmb, s_prev.astype(BF16))
        s_parts = []
        for pr in range(SSD_R // 2):
            col0 = g * SSD_R * SSD_P + pr * LANES
            xs_pair = xs[:, col0:col0 + LANES]
            xsb = xs_pair.astype(BF16)
            yd, sc, colbs = [], [], []
            for sub in range(2):
                h = g * SSD_R + pr * 2 + sub
                colb = jnp.broadcast_to(cs[:, h:h + 1], (q, q))
                decay = jnp.exp(jnp.where(tri, colb - cs_t[h:h + 1, :], -jnp.inf))
                gmat = (cb * decay * dt_t[h:h + 1, :]).astype(BF16)
                yd.append(_dot(gmat, xsb))
                sc.append(_dot((bm_t * w_t[h:h + 1, :]).astype(BF16), xsb))
                colbs.append(colb)
            y_diag = jnp.where(lo, yd[0], yd[1])
            e_col = jnp.exp(jnp.where(lo, colbs[0], colbs[1]))
            y_pair = y_diag + y_off[:, pr * LANES:(pr + 1) * LANES] * e_col + dsk[:, col0:col0 + LANES] * xs_pair
            y_ref[0, 0, :, col0:col0 + LANES] = y_pair.astype(BF16)
            s_parts.append(jnp.where(lo, sc[0], sc[1]))
        g0 = g * SSD_R * SSD_P
        st_ref[g] = s_prev * dec_row[:, g0:g0 + SSD_R * SSD_P] + jnp.concatenate(s_parts, axis=1)


def _ssd(pm, dt, conv_w, conv_b, arow, brow, dsk, expand, n_lat):
    nb, t, _ = pm.shape
    q = SSD_CHUNK
    nc = t // q
    nxc = n_lat // q
    hb = q // HALO_BF16
    last_hb = t // HALO_BF16 - 1

    def cidx(d, j):
        return _ssd_chunk_index(d, j, nxc, nc)

    return pl.pallas_call(
        functools.partial(_ssd_kernel, nxc=nxc, nc=nc),
        grid=(nb, 2, nc),
        in_specs=[pl.BlockSpec((1, q, SSD_CONV_DIM), lambda b, d, j: (b, cidx(d, j), 0)),
                  pl.BlockSpec((1, HALO_BF16, SSD_CONV_DIM),
                               lambda b, d, j: (b, jnp.maximum(cidx(d, j) * hb - 1, 0), 0)),
                  pl.BlockSpec((1, HALO_BF16, SSD_CONV_DIM),
                               lambda b, d, j: (b, jnp.minimum((cidx(d, j) + 1) * hb, last_hb), 0)),
                  pl.BlockSpec((1, q, LANES), lambda b, d, j: (b, cidx(d, j), d)),
                  _resident(conv_w.shape),
                  _resident(conv_b.shape),
                  pl.BlockSpec((1, 1, LANES), lambda b, d, j: (d, 0, 0)),
                  pl.BlockSpec((1, 1, LANES), lambda b, d, j: (d, 0, 0)),
                  pl.BlockSpec((1, 1, SSD_INNER), lambda b, d, j: (d, 0, 0)),
                  _resident(expand.shape)],
        out_specs=pl.BlockSpec((1, 1, q, SSD_INNER), lambda b, d, j: (d, b, cidx(d, j), 0)),
        out_shape=jax.ShapeDtypeStruct((2, nb, t, SSD_INNER), BF16),
        scratch_shapes=[pltpu.VMEM((q + 2 * HALO_BF16, SSD_CONV_DIM), F32),
                        pltpu.VMEM((SSD_G, SSD_N, SSD_R * SSD_P), F32)],
        compiler_params=_params(("parallel", "parallel", "arbitrary")),
    )(pm, pm, pm, dt, conv_w, conv_b, arow, brow, dsk, expand)


def _na_kernel(q_ref, k_ref, v_ref, bias_ref, o_ref, *, rows, n_lat, n_ctx):
    i = pl.program_id(1)
    r0 = jnp.clip(i - NA_WIN_ROWS // 2, 0, rows - NA_WIN_ROWS)
    start = pl.multiple_of(r0 * GRID_W, GRID_W)
    nloc = NA_WIN_ROWS * GRID_W
    k_loc = k_ref[0, pl.ds(start, nloc), :]
    v_loc = v_ref[0, pl.ds(start, nloc), :]
    k_ctx = k_ref[0, n_lat:n_lat + n_ctx, :]
    v_ctx = v_ref[0, n_lat:n_lat + n_ctx, :]
    qv = q_ref[0]
    lane = lax.broadcasted_iota(jnp.int32, (GRID_W, LANES), 1)
    lo = lane < NA_DH
    for p in range(NA_HEADS // 2):
        sl = slice(p * LANES, (p + 1) * LANES)
        qp = qv[:, sl]
        outs = []
        for sub in range(2):
            qm = jnp.where(lo if sub == 0 else jnp.logical_not(lo), qp, jnp.zeros_like(qp))
            s_loc = _dot_nt(qm, k_loc[:, sl]) + bias_ref[0, 2 * p + sub]
            s_ctx = _dot_nt(qm, k_ctx[:, sl])
            mx = jnp.maximum(jnp.max(s_loc, axis=-1, keepdims=True), jnp.max(s_ctx, axis=-1, keepdims=True))
            p_loc = jnp.exp(s_loc - mx)
            p_ctx = jnp.exp(s_ctx - mx)
            den = jnp.sum(p_loc, axis=-1, keepdims=True) + jnp.sum(p_ctx, axis=-1, keepdims=True)
            o = _dot(p_loc.astype(BF16), v_loc[:, sl]) + _dot(p_ctx.astype(BF16), v_ctx[:, sl])
            outs.append(o * (1.0 / den))
        o_ref[0, :, sl] = jnp.where(lo, outs[0], outs[1]).astype(BF16)


def _na(pm, bias, n_lat):
    nb, t, _ = pm.shape
    n_ctx = t - n_lat
    rows = n_lat // GRID_W
    steps = t // GRID_W
    nvar = bias.shape[0] - 1
    qcol = (SSD_CONV_DIM + SSD_INNER) // NA_WIDTH

    def variant(i):
        r0 = jnp.clip(i - NA_WIN_ROWS // 2, 0, rows - NA_WIN_ROWS)
        return jnp.where(i >= rows, nvar, i - r0)

    return pl.pallas_call(
        functools.partial(_na_kernel, rows=rows, n_lat=n_lat, n_ctx=n_ctx),
        grid=(nb, steps),
        in_specs=[pl.BlockSpec((1, GRID_W, NA_WIDTH), lambda b, i: (b, i, qcol)),
                  pl.BlockSpec((1, t, NA_WIDTH), lambda b, i: (b, 0, qcol + 1)),
                  pl.BlockSpec((1, t, NA_WIDTH), lambda b, i: (b, 0, qcol + 2)),
                  pl.BlockSpec((1, NA_HEADS, GRID_W, NA_WIN_ROWS * GRID_W), lambda b, i: (variant(i), 0, 0, 0))],
        out_specs=pl.BlockSpec((1, GRID_W, NA_WIDTH), lambda b, i: (b, i, 0)),
        out_shape=jax.ShapeDtypeStruct((nb, t, NA_WIDTH), BF16),
        compiler_params=_params(("parallel", "arbitrary")),
    )(pm, pm, pm, bias)


def _na_bias_table(rpb):
    nv = NA_WIN_ROWS
    v = np.arange(nv + 1)[:, None, None, None]
    c = np.arange(GRID_W)[None, :, None, None]
    jr = np.arange(NA_WIN_ROWS)[None, None, :, None]
    kc = np.arange(GRID_W)[None, None, None, :]
    c0 = np.clip(c - NA_WIN_COLS // 2, 0, GRID_W - NA_WIN_COLS)
    valid = (kc >= c0) & (kc < c0 + NA_WIN_COLS) & (v < nv) & (jr >= 0)
    dr = np.broadcast_to(np.clip(jr - v + NA_WIN_ROWS - 1, 0, 2 * NA_WIN_ROWS - 2), valid.shape)
    dc = np.broadcast_to(np.clip(kc - c + NA_WIN_COLS - 1, 0, 2 * NA_WIN_COLS - 2), valid.shape)
    tab = rpb.astype(F32)[:, dr, dc]
    tab = jnp.where(jnp.asarray(valid)[None], tab, -jnp.inf)
    return tab.transpose(1, 0, 2, 3, 4).reshape(nv + 1, NA_HEADS, GRID_W, NA_WIN_ROWS * GRID_W)


def _hyb_out_kernel(yf_ref, yb_ref, z_ref, a_ref, h_ref, mod_ref, nw_ref, wy_ref, wa_ref, g_ref, b_ref, o_ref):
    y = (yf_ref[0, 0].astype(F32) + yb_ref[0, 0].astype(F32)) * _silu(z_ref[0].astype(F32))
    y = y * lax.rsqrt(jnp.mean(y * y, axis=-1, keepdims=True) + RMS_EPS) * nw_ref[...]
    o = _dot(y.astype(BF16), wy_ref[...]) + _dot(a_ref[0], wa_ref[...])
    gate = mod_ref[0][2:3, :]
    o_ref[0] = _layer_norm(ALPHA * h_ref[0] + gate * o, g_ref[...], b_ref[...])


def _hyb_out(yd, pm, att, h, mod, norm_w, wy, wa, ln_g, ln_b, nxt):
    nb, t, _ = h.shape
    nt = t // TM
    zcol = SSD_CONV_DIM // SSD_INNER
    return pl.pallas_call(
        _hyb_out_kernel,
        grid=(nb, nt),
        in_specs=[pl.BlockSpec((1, 1, TM, SSD_INNER), lambda b, i: (0, b, i, 0)),
                  pl.BlockSpec((1, 1, TM, SSD_INNER), lambda b, i: (1, b, i, 0)),
                  pl.BlockSpec((1, TM, SSD_INNER), lambda b, i: (b, i, zcol)),
                  pl.BlockSpec((1, TM, NA_WIDTH), lambda b, i: (b, i, 0)),
                  pl.BlockSpec((1, TM, D_MODEL), lambda b, i: (b, i, 0)),
                  _mod_spec(nxt, nb),
                  _resident(norm_w.shape), _resident(wy.shape), _resident(wa.shape),
                  _resident(ln_g.shape), _resident(ln_b.shape)],
        out_specs=pl.BlockSpec((1, TM, D_MODEL), lambda b, i: (b, i, 0)),
        out_shape=jax.ShapeDtypeStruct((nb, t, D_MODEL), F32),
        compiler_params=_params(("parallel", "parallel")),
    )(yd, yd, pm, att, h, mod, norm_w, wy, wa, ln_g, ln_b)


def _ffn_kernel(hm_ref, hp_ref, hn_ref, mod_ref, wup_ref, cw_ref, cb_ref, wdn_ref, g_ref, b_ref, o_ref,
                hv_ref, hg_ref, acc_ref, *, nxt, nt):
    i = pl.program_id(1)
    m = mod_ref[0]
    shift, scale, gate = m[3:4, :], m[4:5, :], m[5:6, :]
    seg_start = jnp.logical_or(i == 0, i == nxt)
    seg_end = jnp.logical_or(i == nxt - 1, i == nt - 1)
    hm = hm_ref[0]
    up = jnp.where(seg_start, 0.0, hp_ref[0] * (1.0 + scale) + shift)
    un = jnp.where(seg_end, 0.0, hn_ref[0] * (1.0 + scale) + shift)
    uext = jnp.concatenate([up, hm * (1.0 + scale) + shift, un], axis=0).astype(BF16)

    def conv(ref, col0):
        w = cw_ref[:, col0:col0 + FFN_CH]
        return (w[0:1, :] * ref[pl.ds(HALO_F32 - 1, TM), :] + w[1:2, :] * ref[pl.ds(HALO_F32, TM), :]
                + w[2:3, :] * ref[pl.ds(HALO_F32 + 1, TM), :] + cb_ref[:, col0:col0 + FFN_CH])

    for c in range(D_FF // FFN_CH):
        v0 = c * FFN_CH
        g0 = D_FF + c * FFN_CH
        hv_ref[...] = _dot(uext, wup_ref[:, v0:v0 + FFN_CH])
        hg_ref[...] = _dot(uext, wup_ref[:, g0:g0 + FFN_CH])
        act = (_silu(conv(hg_ref, g0)) * conv(hv_ref, v0)).astype(BF16)
        contrib = _dot(act, wdn_ref[v0:v0 + FFN_CH, :])
        if c == 0:
            acc_ref[...] = contrib
        else:
            acc_ref[...] += contrib
    o_ref[0] = _layer_norm(ALPHA * hm + gate * acc_ref[...], g_ref[...], b_ref[...])


def _ffn(h, mod, w_up, conv_w, conv_b, w_dn, ln_g, ln_b, nxt, n_out_tiles):
    nb, t, _ = h.shape
    nt = t // TM
    hb = TM // HALO_F32
    last_hb = t // HALO_F32 - 1
    return pl.pallas_call(
        functools.partial(_ffn_kernel, nxt=nxt, nt=nt),
        grid=(nb, n_out_tiles),
        in_specs=[pl.BlockSpec((1, TM, D_MODEL), lambda b, i: (b, i, 0)),
                  pl.BlockSpec((1, HALO_F32, D_MODEL), lambda b, i: (b, jnp.maximum(i * hb - 1, 0), 0)),
                  pl.BlockSpec((1, HALO_F32, D_MODEL), lambda b, i: (b, jnp.minimum((i + 1) * hb, last_hb), 0)),
                  _mod_spec(nxt, nb),
                  _resident(w_up.shape), _resident(conv_w.shape), _resident(conv_b.shape),
                  _resident(w_dn.shape), _resident(ln_g.shape), _resident(ln_b.shape)],
        out_specs=pl.BlockSpec((1, TM, D_MODEL), lambda b, i: (b, i, 0)),
        out_shape=jax.ShapeDtypeStruct((nb, n_out_tiles * TM, D_MODEL), F32),
        scratch_shapes=[pltpu.VMEM((TM + 2 * HALO_F32, FFN_CH), F32),
                        pltpu.VMEM((TM + 2 * HALO_F32, FFN_CH), F32),
                        pltpu.VMEM((TM, D_MODEL), F32)],
        compiler_params=_params(("parallel", "parallel")),
    )(h, h, h, mod, w_up, conv_w, conv_b, w_dn, ln_g, ln_b)


def _diff_proj_kernel(h_ref, mod_ref, w_ref, cos_ref, sin_ref, o_ref):
    m = mod_ref[0]
    u = (h_ref[0] * (1.0 + m[1:2, :]) + m[0:1, :]).astype(BF16)
    cosv = cos_ref[...]
    sinv = sin_ref[...]
    for n0 in range(0, 3 * D_MODEL, HYB_NCH):
        y = _dot(u, w_ref[:, n0:n0 + HYB_NCH])
        if n0 < 2 * D_MODEL:
            parts = []
            for p0 in range(0, HYB_NCH, LANES):
                yp = y[:, p0:p0 + LANES]
                parts.append(yp * cosv + pltpu.roll(yp, LANES // 2, axis=1) * sinv)
            y = jnp.concatenate(parts, axis=1)
        o_ref[0, :, n0:n0 + HYB_NCH] = y.astype(BF16)


def _diff_proj(h, mod, w, cos_t, sin_t, nxt):
    nb, t, _ = h.shape
    nt = t // TM
    return pl.pallas_call(
        _diff_proj_kernel,
        grid=(nb, nt),
        in_specs=[pl.BlockSpec((1, TM, D_MODEL), lambda b, i: (b, i, 0)),
                  _mod_spec(nxt, nb),
                  _resident(w.shape),
                  pl.BlockSpec((TM, LANES), lambda b, i: (i, 0)),
                  pl.BlockSpec((TM, LANES), lambda b, i: (i, 0))],
        out_specs=pl.BlockSpec((1, TM, 3 * D_MODEL), lambda b, i: (b, i, 0)),
        out_shape=jax.ShapeDtypeStruct((nb, t, 3 * D_MODEL), BF16),
        compiler_params=_params(("parallel", "parallel")),
    )(h, mod, w, cos_t, sin_t)


def _diff_attn_kernel(q_ref, k_ref, v_ref, lam_ref, sw_ref, o_ref, *, n_lat, n_ctx, nxt, lam_init, tq):
    i = pl.program_id(2)
    lp = lam_ref[...]
    lam = (jnp.exp(jnp.sum(lp[0:1, :] * lp[1:2, :], axis=1, keepdims=True))
           - jnp.exp(jnp.sum(lp[2:3, :] * lp[3:4, :], axis=1, keepdims=True)) + lam_init)
    rb = LANES
    lane = lax.broadcasted_iota(jnp.int32, (rb, LANES), 1)
    sub0 = (lane % (LANES // 2)) < (DIFF_SUB // 2)

    def run(k0, nk):
        kk = k_ref[0, k0:k0 + nk, :]
        vv = v_ref[0, k0:k0 + nk, :]
        for r in range(tq // rb):
            qv = q_ref[0, r * rb:(r + 1) * rb, :]
            zero = jnp.zeros_like(qv)
            s0 = _dot_nt(jnp.where(sub0, qv, zero), kk)
            s1 = _dot_nt(jnp.where(sub0, zero, qv), kk)
            p0 = jnp.exp(s0 - jnp.max(s0, axis=-1, keepdims=True))
            p1 = jnp.exp(s1 - jnp.max(s1, axis=-1, keepdims=True))
            i0 = 1.0 / jnp.sum(p0, axis=-1, keepdims=True)
            i1 = lam / jnp.sum(p1, axis=-1, keepdims=True)
            a = (p0 * i0 - p1 * i1).astype(BF16)
            o = _dot(a, vv)
            o = o * lax.rsqrt(jnp.mean(o * o, axis=-1, keepdims=True) + RMS_EPS) * sw_ref[...] * (1.0 - lam_init)
            o_ref[0, r * rb:(r + 1) * rb, :] = o.astype(BF16)

    @pl.when(i < nxt)
    def _():
        run(0, n_lat + n_ctx)

    @pl.when(i >= nxt)
    def _():
        run(n_lat, n_ctx)


def _diff_attn(qkv, lam_p, subln_w, n_lat, lam_init, n_q_tiles, tq):
    nb, t, _ = qkv.shape
    n_ctx = t - n_lat
    nxt = n_lat // tq
    hd = 2 * DIFF_SUB
    return pl.pallas_call(
        functools.partial(_diff_attn_kernel, n_lat=n_lat, n_ctx=n_ctx, nxt=nxt, lam_init=lam_init, tq=tq),
        grid=(nb, DIFF_HEADS, n_q_tiles),
        in_specs=[pl.BlockSpec((1, tq, hd), lambda b, hh, i: (b, i, hh)),
                  pl.BlockSpec((1, t, hd), lambda b, hh, i: (b, 0, DIFF_HEADS + hh)),
                  pl.BlockSpec((1, t, hd), lambda b, hh, i: (b, 0, 2 * DIFF_HEADS + hh)),
                  _resident(lam_p.shape), _resident(subln_w.shape)],
        out_specs=pl.BlockSpec((1, tq, hd), lambda b, hh, i: (b, i, hh)),
        out_shape=jax.ShapeDtypeStruct((nb, n_q_tiles * tq, D_MODEL), BF16),
        compiler_params=_params(("parallel", "parallel", "arbitrary")),
    )(qkv, qkv, qkv, lam_p, subln_w)


def _diff_out_kernel(a_ref, h_ref, mod_ref, w_ref, g_ref, b_ref, o_ref):
    o = _dot(a_ref[0], w_ref[...])
    gate = mod_ref[0][2:3, :]
    o_ref[0] = _layer_norm(ALPHA * h_ref[0] + gate * o, g_ref[...], b_ref[...])


def _diff_out(att, h, mod, w, ln_g, ln_b, nxt, n_tiles):
    nb, t, _ = h.shape
    return pl.pallas_call(
        _diff_out_kernel,
        grid=(nb, n_tiles),
        in_specs=[pl.BlockSpec((1, TM, D_MODEL), lambda b, i: (b, i, 0)),
                  pl.BlockSpec((1, TM, D_MODEL), lambda b, i: (b, i, 0)),
                  _mod_spec(nxt, nb),
                  _resident(w.shape), _resident(ln_g.shape), _resident(ln_b.shape)],
        out_specs=pl.BlockSpec((1, TM, D_MODEL), lambda b, i: (b, i, 0)),
        out_shape=jax.ShapeDtypeStruct((nb, n_tiles * TM, D_MODEL), F32),
        compiler_params=_params(("parallel", "parallel")),
    )(att, h, mod, w, ln_g, ln_b)


def _rope_tables(n_lat, n_ctx):
    tok = np.arange(n_lat)
    row = (tok // GRID_W).astype(np.float32)
    col = (tok % GRID_W).astype(np.float32)
    n_freq = DIFF_SUB // 4
    inv = jnp.asarray(ROPE_BASE, F32) ** (-jnp.arange(n_freq, dtype=F32) / n_freq)
    ang = jnp.concatenate([jnp.asarray(row)[:, None] * inv, jnp.asarray(col)[:, None] * inv], axis=-1)
    cos = jnp.cos(ang)
    sin = jnp.sin(ang)
    cos_t = jnp.concatenate([cos, cos, cos, cos], axis=-1)
    sin_t = jnp.concatenate([-sin, -sin, sin, sin], axis=-1)
    cos_t = jnp.concatenate([cos_t, jnp.ones((n_ctx, LANES), F32)], axis=0)
    sin_t = jnp.concatenate([sin_t, jnp.zeros((n_ctx, LANES), F32)], axis=0)
    return cos_t, sin_t


def _diff_head_perm():
    half = DIFF_SUB // 2
    perm = []
    for hh in range(DIFF_HEADS):
        base = hh * 2 * DIFF_SUB
        for part in range(2):
            for sub in range(2):
                start = base + sub * DIFF_SUB + part * half
                perm.extend(range(start, start + half))
    return np.asarray(perm)


def kernel(x, c, ctx, c_ctx, ada_w, ada_b, ln1_g, ln1_b, ln2_g, ln2_b, ffn_w_up, ffn_conv_w, ffn_conv_b, ffn_w_down, hyb_w_in, ssd_conv_w, ssd_conv_b, ssd_a_log, ssd_dt_bias, ssd_d, ssd_norm_w, na_rpb, hyb_w_out, diff_w_in, diff_lambda, diff_subln_w, diff_w_out):
    nb, n_lat, d = x.shape
    n_ctx = ctx.shape[1]
    assert d == D_MODEL and n_lat % TM == 0 and n_ctx % TM == 0 and n_lat % (NA_WIN_ROWS * GRID_W) == 0
    t = n_lat + n_ctx
    nt = t // TM
    nxt = n_lat // TM

    h = jnp.concatenate([x, ctx], axis=1)
    cond_rows = -(-(nb + 1) // 8) * 8
    cond = jnp.concatenate([c, c_ctx[None, :], jnp.zeros((cond_rows - nb - 1, d), F32)], axis=0)
    mod_all = _ada_mod(cond, ada_w, ada_b).reshape(DEPTH, cond_rows, 6, d)
    cos_t, sin_t = _rope_tables(n_lat, n_ctx)
    perm = _diff_head_perm()
    expand = jnp.asarray(np.kron(np.eye(LANES, SSD_H, dtype=np.float32), np.ones((1, SSD_P), np.float32)))

    for i in range(DEPTH):
        last = i == DEPTH - 1
        j = i // 2
        mod = mod_all[i]
        n_tiles = nxt if last else nt
        row = lambda v: v.reshape(1, -1)
        if i % 2 == 0:
            w_in = hyb_w_in[j]
            o_xbc = SSD_INNER
            o_dt = o_xbc + SSD_CONV_DIM
            o_q = o_dt + 2 * SSD_H
            w_main = jnp.concatenate([w_in[:, o_xbc:o_dt], w_in[:, :SSD_INNER],
                                      w_in[:, o_q:o_q + NA_WIDTH] * NA_DH ** -0.5,
                                      w_in[:, o_q + NA_WIDTH:]], axis=1).astype(BF16)
            w_dt = jnp.zeros((d, 2 * LANES), F32)
            w_dt = w_dt.at[:, :SSD_H].set(w_in[:, o_dt:o_dt + SSD_H])
            w_dt = w_dt.at[:, LANES:LANES + SSD_H].set(w_in[:, o_dt + SSD_H:o_q]).astype(BF16)
            pm, dt = _hyb_proj(h, mod, w_main, w_dt, nxt)

            pad = lambda v: jnp.zeros((2, 1, LANES), F32).at[:, 0, :SSD_H].set(v)
            arow = pad(-jnp.exp(ssd_a_log[j].astype(F32)))
            brow = pad(ssd_dt_bias[j].astype(F32))
            dsk = jnp.repeat(ssd_d[j].astype(F32), SSD_P, axis=-1).reshape(2, 1, SSD_INNER)
            yd = _ssd(pm, dt, ssd_conv_w[j], row(ssd_conv_b[j]), arow, brow, dsk, expand, n_lat)
            att = _na(pm, _na_bias_table(na_rpb[j]), n_lat)
            w_out = hyb_w_out[j].astype(BF16)
            h = _hyb_out(yd, pm, att, h, mod, row(ssd_norm_w[j]), w_out[:SSD_INNER], w_out[SSD_INNER:],
                         row(ln1_g[i]), row(ln1_b[i]), nxt)
        else:
            lam_init = 0.8 - 0.6 * math.exp(-0.3 * i)
            w_in = diff_w_in[j]
            wq = w_in[:, :d][:, perm] * DIFF_SUB ** -0.5
            wk = w_in[:, d:2 * d][:, perm]
            w_qkv = jnp.concatenate([wq, wk, w_in[:, 2 * d:]], axis=1).astype(BF16)
            qkv = _diff_proj(h, mod, w_qkv, cos_t, sin_t, nxt)
            att = _diff_attn(qkv, diff_lambda[j].astype(F32), row(diff_subln_w[j]), n_lat, lam_init, n_tiles, TM)
            h = _diff_out(att, h, mod, diff_w_out[j].astype(BF16), row(ln1_g[i]), row(ln1_b[i]), nxt, n_tiles)
        h = _ffn(h, mod, ffn_w_up[i].astype(BF16), ffn_conv_w[i], row(ffn_conv_b[i]),
                 ffn_w_down[i].astype(BF16), row(ln2_g[i]), row(ln2_b[i]), nxt, n_tiles)
    return h
```

```python
import functools
import math

import numpy as np
import jax
import jax.numpy as jnp
from jax import lax
from jax.experimental import pallas as pl
from jax.experimental.pallas import tpu as pltpu

F32 = jnp.float32
BF16 = jnp.bfloat16

D_MODEL = 1024
DEPTH = 4
GRID_W = 64
SSD_P = 64
SSD_H = 16
SSD_G = 4
SSD_R = SSD_H // SSD_G
SSD_N = 128
SSD_INNER = SSD_H * SSD_P
SSD_GN = SSD_G * SSD_N
SSD_CONV_W = 5
SSD_CONV_DIM = SSD_INNER + 2 * SSD_GN
SSD_CHUNK = 128
NA_HEADS = 8
NA_DH = 64
NA_WIDTH = NA_HEADS * NA_DH
NA_WIN_ROWS = 8
NA_WIN_COLS = 16
DIFF_HEADS = 8
DIFF_SUB = 64
ROPE_BASE = 10000.0
D_FF = 2816
FFN_CH = 256
ALPHA = (2.0 * DEPTH) ** 0.25
LN_EPS = 1e-5
RMS_EPS = 1e-5

LANES = 128
TM = 256
HALO_F32 = 8
HALO_BF16 = 16
VMEM_LIMIT = 56 * 1024 * 1024
HYB_SSD = SSD_CONV_DIM + SSD_INNER
HYB_MAIN = HYB_SSD + 3 * NA_WIDTH
NA_VROWS = NA_DH + HALO_BF16
NA_BLK_ROWS = 4
NA_KEY_BLKS = 3
DIFF_VROWS = 2 * DIFF_SUB + HALO_BF16
DIFF_KCH = 512
LOG2E = 1.4426950408889634
HYB_NCH = 512


def _params(sem, vmem=VMEM_LIMIT):
    return pltpu.CompilerParams(dimension_semantics=sem, vmem_limit_bytes=vmem)


def _resident(shape):
    nd = len(shape)
    return pl.BlockSpec(shape, lambda *_: (0,) * nd, pipeline_mode=pl.Buffered(1))


def _dot(a, b):
    return jnp.dot(a, b, preferred_element_type=F32)


def _dot_nt(a, b):
    return lax.dot_general(a, b, (((1,), (1,)), ((), ())), preferred_element_type=F32)


def _silu(v):
    return v * jax.nn.sigmoid(v)


def _layer_norm(r, g, b):
    mu = jnp.mean(r, axis=-1, keepdims=True)
    xc = r - mu
    var = jnp.mean(xc * xc, axis=-1, keepdims=True)
    return xc * lax.rsqrt(var + LN_EPS) * g + b


def _mod_spec(nxt, nb):
    return pl.BlockSpec((1, 6, D_MODEL), lambda b, i: (jnp.where(i >= nxt, nb, b), 0, 0))


def _ada_kernel(c_ref, w_ref, b_ref, o_ref):
    s = _silu(c_ref[...]).astype(BF16)
    o_ref[0] = _dot(s, w_ref[0].astype(BF16)) + b_ref[0]


def _ada_mod(cond, ada_w, ada_b):
    rows = cond.shape[0]
    n = ada_w.shape[-1]
    tn = n // 4
    return pl.pallas_call(
        _ada_kernel,
        grid=(DEPTH, n // tn),
        in_specs=[pl.BlockSpec((rows, D_MODEL), lambda l, j: (0, 0)),
                  pl.BlockSpec((1, D_MODEL, tn), lambda l, j: (l, 0, j)),
                  pl.BlockSpec((1, 1, tn), lambda l, j: (l, 0, j))],
        out_specs=pl.BlockSpec((1, rows, tn), lambda l, j: (l, 0, j)),
        out_shape=jax.ShapeDtypeStruct((DEPTH, rows, n), F32),
        name="ada_mod",
        compiler_params=_params(("parallel", "parallel")),
    )(cond, ada_w, ada_b.reshape(DEPTH, 1, n))


def _hyb_proj_kernel(h_ref, mod_ref, w_ref, wdt_ref, o_ref, dt_ref, qt_ref, k_ref, vt_ref):
    m = mod_ref[0]
    u = (h_ref[0] * (1.0 + m[1:2, :]) + m[0:1, :]).astype(BF16)
    for n0 in range(0, HYB_SSD, HYB_NCH):
        o_ref[0, :, n0:n0 + HYB_NCH] = _dot(u, w_ref[:, n0:n0 + HYB_NCH]).astype(BF16)
    dt_ref[0] = _dot(u, wdt_ref[...])
    qt_ref[0] = _dot(u, w_ref[:, HYB_SSD:HYB_SSD + NA_WIDTH]).T.astype(BF16)
    k_ref[0] = _dot(u, w_ref[:, HYB_SSD + NA_WIDTH:HYB_SSD + 2 * NA_WIDTH]).astype(BF16)
    v_t = _dot(u, w_ref[:, HYB_SSD + 2 * NA_WIDTH:HYB_MAIN]).T
    for hh in range(NA_HEADS):
        vt_ref[0, hh, 0:NA_DH, :] = v_t[hh * NA_DH:(hh + 1) * NA_DH, :].astype(BF16)
        vt_ref[0, hh, NA_DH:NA_VROWS, :] = jnp.ones((NA_VROWS - NA_DH, TM), BF16)


def _hyb_proj(h, mod, w_main, w_dt, nxt):
    nb, t, _ = h.shape
    nt = t // TM
    return pl.pallas_call(
        _hyb_proj_kernel,
        grid=(nb, nt),
        in_specs=[pl.BlockSpec((1, TM, D_MODEL), lambda b, i: (b, i, 0)),
                  _mod_spec(nxt, nb),
                  _resident(w_main.shape),
                  _resident(w_dt.shape)],
        out_specs=[pl.BlockSpec((1, TM, HYB_SSD), lambda b, i: (b, i, 0)),
                   pl.BlockSpec((1, TM, 2 * LANES), lambda b, i: (b, i, 0)),
                   pl.BlockSpec((1, NA_WIDTH, TM), lambda b, i: (b, 0, i)),
                   pl.BlockSpec((1, TM, NA_WIDTH), lambda b, i: (b, i, 0)),
                   pl.BlockSpec((1, NA_HEADS, NA_VROWS, TM), lambda b, i: (b, 0, 0, i))],
        out_shape=[jax.ShapeDtypeStruct((nb, t, HYB_SSD), BF16),
                   jax.ShapeDtypeStruct((nb, t, 2 * LANES), F32),
                   jax.ShapeDtypeStruct((nb, NA_WIDTH, t), BF16),
                   jax.ShapeDtypeStruct((nb, t, NA_WIDTH), BF16),
                   jax.ShapeDtypeStruct((nb, NA_HEADS, NA_VROWS, t), BF16)],
        name="hyb_proj",
        compiler_params=_params(("parallel", "parallel")),
    )(h, mod, w_main, w_dt)


def _ssd_chunk_index(d, j, nxc, nc):
    return jnp.where(d == 0, (j + nxc) % nc, nc - 1 - j)


def _ssd_kernel(xm_ref, xp_ref, xn_ref, dt_ref, cw_ref, cb_ref, arow_ref, brow_ref, dsk_ref, e_ref,
                y_ref, ext_ref, st_ref, *, nxc, nc):
    d = pl.program_id(1)
    j = pl.program_id(2)
    c = _ssd_chunk_index(d, j, nxc, nc)
    q = SSD_CHUNK

    @pl.when(j == 0)
    def _():
        st_ref[...] = jnp.zeros_like(st_ref)

    seg_start = jnp.logical_or(c == 0, c == nxc)
    seg_end = jnp.logical_or(c == nxc - 1, c == nc - 1)
    ext_ref[0:HALO_BF16, :] = jnp.where(seg_start, 0.0, xp_ref[0].astype(F32))
    ext_ref[HALO_BF16:HALO_BF16 + q, :] = xm_ref[0].astype(F32)
    ext_ref[HALO_BF16 + q:, :] = jnp.where(seg_end, 0.0, xn_ref[0].astype(F32))
    acc = cb_ref[...]
    for k in range(SSD_CONV_W):
        acc = acc + cw_ref[k:k + 1, :] * ext_ref[pl.ds(HALO_BF16 - SSD_CONV_W // 2 + k, q), :]
    u = _silu(acc)
    xs = u[:, :SSD_INNER]

    dtr = dt_ref[0] + brow_ref[0]
    dtv = jnp.maximum(dtr, 0.0) + jnp.log1p(jnp.exp(-jnp.abs(dtr)))
    adt = dtv * arow_ref[0]
    ri = lax.broadcasted_iota(jnp.int32, (q, q), 0)
    ci = lax.broadcasted_iota(jnp.int32, (q, q), 1)
    tri = jnp.where(d == 0, ri - ci, ci - ri) >= 0
    cs = jnp.dot(tri.astype(F32), adt, precision=lax.Precision.HIGHEST, preferred_element_type=F32)
    cs_t = cs.T
    dt_t = dtv.T
    tot = jnp.where(d == 0, cs[q - 1:q, :], cs[0:1, :])
    tot_t = jnp.where(d == 0, cs_t[:, q - 1:q], cs_t[:, 0:1])
    w_t = jnp.exp(tot_t - cs_t) * dt_t
    dec_row = jnp.dot(jnp.broadcast_to(jnp.exp(tot), (8, LANES)), e_ref[...],
                      precision=lax.Precision.HIGHEST, preferred_element_type=F32)[0:1]

    lane = lax.broadcasted_iota(jnp.int32, (q, LANES), 1)
    lo = lane < SSD_P
    dsk = dsk_ref[0]
    for g in range(SSD_G):
        bm = u[:, SSD_INNER + g * SSD_N:SSD_INNER + (g + 1) * SSD_N]
        cm = u[:, SSD_INNER + SSD_GN + g * SSD_N:SSD_INNER + SSD_GN + (g + 1) * SSD_N]
        cmb = cm.astype(BF16)
        cb = _dot_nt(cmb, bm.astype(BF16))
        bm_t = bm.T
        s_prev = st_ref[g]
        y_off = _dot(cmb, s_prev.astype(BF16))
        s_parts = []
        for pr in range(SSD_R // 2):
            col0 = g * SSD_R * SSD_P + pr * LANES
            xs_pair = xs[:, col0:col0 + LANES]
            xsb = xs_pair.astype(BF16)
            yd, sc, colbs = [], [], []
            for sub in range(2):
                h = g * SSD_R + pr * 2 + sub
                colb = jnp.broadcast_to(cs[:, h:h + 1], (q, q))
                decay = jnp.exp(jnp.where(tri, colb - cs_t[h:h + 1, :], -jnp.inf))
                gmat = (cb * decay * dt_t[h:h + 1, :]).astype(BF16)
                yd.append(_dot(gmat, xsb))
                sc.append(_dot((bm_t * w_t[h:h + 1, :]).astype(BF16), xsb))
                colbs.append(colb)
            y_diag = jnp.where(lo, yd[0], yd[1])
            e_col = jnp.exp(jnp.where(lo, colbs[0], colbs[1]))
            y_pair = y_diag + y_off[:, pr * LANES:(pr + 1) * LANES] * e_col + dsk[:, col0:col0 + LANES] * xs_pair
            y_ref[0, 0, :, col0:col0 + LANES] = y_pair.astype(BF16)
            s_parts.append(jnp.where(lo, sc[0], sc[1]))
        g0 = g * SSD_R * SSD_P
        st_ref[g] = s_prev * dec_row[:, g0:g0 + SSD_R * SSD_P] + jnp.concatenate(s_parts, axis=1)


def _ssd(pm, dt, conv_w, conv_b, arow, brow, dsk, expand, n_lat):
    nb, t, _ = pm.shape
    q = SSD_CHUNK
    nc = t // q
    nxc = n_lat // q
    hb = q // HALO_BF16
    last_hb = t // HALO_BF16 - 1

    def cidx(d, j):
        return _ssd_chunk_index(d, j, nxc, nc)

    return pl.pallas_call(
        functools.partial(_ssd_kernel, nxc=nxc, nc=nc),
        grid=(nb, 2, nc),
        in_specs=[pl.BlockSpec((1, q, SSD_CONV_DIM), lambda b, d, j: (b, cidx(d, j), 0)),
                  pl.BlockSpec((1, HALO_BF16, SSD_CONV_DIM),
                               lambda b, d, j: (b, jnp.maximum(cidx(d, j) * hb - 1, 0), 0)),
                  pl.BlockSpec((1, HALO_BF16, SSD_CONV_DIM),
                               lambda b, d, j: (b, jnp.minimum((cidx(d, j) + 1) * hb, last_hb), 0)),
                  pl.BlockSpec((1, q, LANES), lambda b, d, j: (b, cidx(d, j), d)),
                  _resident(conv_w.shape),
                  _resident(conv_b.shape),
                  pl.BlockSpec((1, 1, LANES), lambda b, d, j: (d, 0, 0)),
                  pl.BlockSpec((1, 1, LANES), lambda b, d, j: (d, 0, 0)),
                  pl.BlockSpec((1, 1, SSD_INNER), lambda b, d, j: (d, 0, 0)),
                  _resident(expand.shape)],
        out_specs=pl.BlockSpec((1, 1, q, SSD_INNER), lambda b, d, j: (d, b, cidx(d, j), 0)),
        out_shape=jax.ShapeDtypeStruct((2, nb, t, SSD_INNER), BF16),
        scratch_shapes=[pltpu.VMEM((q + 2 * HALO_BF16, SSD_CONV_DIM), F32),
                        pltpu.VMEM((SSD_G, SSD_N, SSD_R * SSD_P), F32)],
        name="ssd_scan",
        compiler_params=_params(("parallel", "parallel", "arbitrary")),
    )(pm, pm, pm, dt, conv_w, conv_b, arow, brow, dsk, expand)


def _na_kernel(qt_ref, k0_ref, k1_ref, k2_ref, kc_ref, v0_ref, v1_ref, v2_ref, vc_ref, bias_ref, o_ref, *, nblk):
    i = pl.program_id(1)
    nq = NA_BLK_ROWS * GRID_W
    zeros = jnp.zeros((NA_DH, nq), BF16)
    local = ((k0_ref, v0_ref), (k1_ref, v1_ref), (k2_ref, v2_ref))

    def run(with_local):
        def scores(hh):
            sl = slice((hh // 2) * LANES, (hh // 2 + 1) * LANES)
            qh = qt_ref[0, hh * NA_DH:(hh + 1) * NA_DH, :]
            qm = jnp.concatenate([qh, zeros] if hh % 2 == 0 else [zeros, qh], axis=0)
            sc = []
            if with_local:
                for jb, (kr, _) in enumerate(local):
                    sc.append(_dot(kr[0, :, sl], qm) + bias_ref[0, hh, jb * nq:(jb + 1) * nq, :])
            sc.append(_dot(kc_ref[0, :, sl], qm))
            return sc

        outs = []
        s_next = scores(0)
        for hh in range(NA_HEADS):
            s_cur = s_next
            if hh + 1 < NA_HEADS:
                s_next = scores(hh + 1)
            vals = ([vr[0, hh] for _, vr in local] if with_local else []) + [vc_ref[0, hh]]
            mx = functools.reduce(jnp.maximum, [jnp.max(s, axis=0, keepdims=True) for s in s_cur])
            o_t = functools.reduce(
                lambda a, b: a + b, [_dot(v, jnp.exp(s - mx).astype(BF16)) for s, v in zip(s_cur, vals)])
            outs.append(o_t[:NA_DH, :] * (1.0 / o_t[NA_DH:NA_DH + 1, :]))
        o_ref[0] = jnp.concatenate(outs, axis=0).T.astype(BF16)

    @pl.when(i < nblk)
    def _():
        run(True)

    @pl.when(i >= nblk)
    def _():
        run(False)


def _na_variants(rows):
    nblk = rows // NA_BLK_ROWS
    assert rows % NA_BLK_ROWS == 0 and nblk >= NA_KEY_BLKS
    keys, vid = [], []
    for b in range(nblk):
        ws = int(np.clip(b - 1, 0, nblk - NA_KEY_BLKS)) * NA_BLK_ROWS
        key = []
        for a in range(NA_BLK_ROWS):
            r = b * NA_BLK_ROWS + a
            r0 = int(np.clip(r - NA_WIN_ROWS // 2, 0, rows - NA_WIN_ROWS))
            assert 0 <= r0 - ws and r0 - ws + NA_WIN_ROWS <= NA_KEY_BLKS * NA_BLK_ROWS
            key.append((r0 - ws, r0 - r))
        key = tuple(key)
        if key not in keys:
            keys.append(key)
        vid.append(keys.index(key))
    return keys, vid


def _na_bias_table(rpb, variants):
    c = np.arange(GRID_W)[:, None]
    kc = np.arange(GRID_W)[None, :]
    c0 = np.clip(c - NA_WIN_COLS // 2, 0, GRID_W - NA_WIN_COLS)
    valid_c = (kc >= c0) & (kc < c0 + NA_WIN_COLS)
    dc = kc - c + NA_WIN_COLS - 1
    onehot = ((dc[None] == np.arange(2 * NA_WIN_COLS - 1)[:, None, None]) & valid_c[None]).astype(np.float32)
    tz = jnp.einsum('hrd,dck->hrck', rpb.astype(F32), jnp.asarray(onehot), precision=lax.Precision.HIGHEST)
    nkr = NA_KEY_BLKS * NA_BLK_ROWS
    ndr = 2 * NA_WIN_ROWS - 1
    tabs = []
    for key in variants:
        sel = np.zeros((NA_BLK_ROWS, nkr, ndr), np.float32)
        valid_r = np.zeros((NA_BLK_ROWS, nkr), bool)
        for a, (off, e) in enumerate(key):
            for w in range(NA_WIN_ROWS):
                sel[a, off + w, e + NA_WIN_ROWS - 1 + w] = 1.0
                valid_r[a, off + w] = True
        tab = jnp.einsum('ajr,hrck->hajck', jnp.asarray(sel), tz, precision=lax.Precision.HIGHEST)
        ok = jnp.asarray(valid_r[None, :, :, None, None] & valid_c[None, None, None, :, :])
        tab = jnp.where(ok, tab, -jnp.inf)
        tabs.append(tab.transpose(0, 2, 4, 1, 3).reshape(NA_HEADS, nkr * GRID_W, NA_BLK_ROWS * GRID_W))
    return jnp.stack(tabs)


def _na(qt, k, vt, bias, vid, n_lat):
    nb, t, _ = k.shape
    n_ctx = t - n_lat
    nq = NA_BLK_ROWS * GRID_W
    nblk = n_lat // nq
    assert n_ctx % nq == 0 and n_lat % n_ctx == 0
    steps = t // nq
    cblk = n_lat // n_ctx

    def wb(i):
        return jnp.clip(i - 1, 0, nblk - NA_KEY_BLKS)

    def variant(i):
        v = jnp.int32(0)
        for blk in range(1, nblk):
            v = jnp.where(i == blk, vid[blk], v)
        return v

    kspec = [pl.BlockSpec((1, nq, NA_WIDTH), functools.partial(lambda b, i, j: (b, wb(i) + j, 0), j=j))
             for j in range(NA_KEY_BLKS)]
    vspec = [pl.BlockSpec((1, NA_HEADS, NA_VROWS, nq), functools.partial(lambda b, i, j: (b, 0, 0, wb(i) + j), j=j))
             for j in range(NA_KEY_BLKS)]
    return pl.pallas_call(
        functools.partial(_na_kernel, nblk=nblk),
        grid=(nb, steps),
        in_specs=([pl.BlockSpec((1, NA_WIDTH, nq), lambda b, i: (b, 0, i))] + kspec
                  + [pl.BlockSpec((1, n_ctx, NA_WIDTH), lambda b, i: (b, cblk, 0))] + vspec
                  + [pl.BlockSpec((1, NA_HEADS, NA_VROWS, n_ctx), lambda b, i: (b, 0, 0, cblk)),
                     pl.BlockSpec((1,) + bias.shape[1:], lambda b, i: (variant(i), 0, 0, 0))]),
        out_specs=pl.BlockSpec((1, nq, NA_WIDTH), lambda b, i: (b, i, 0)),
        out_shape=jax.ShapeDtypeStruct((nb, t, NA_WIDTH), BF16),
        name="na_attn",
        compiler_params=_params(("parallel", "arbitrary")),
    )(qt, k, k, k, k, vt, vt, vt, vt, bias)


def _hyb_out_kernel(yf_ref, yb_ref, z_ref, a_ref, h_ref, mod_ref, nw_ref, wy_ref, wa_ref, g_ref, b_ref, o_ref):
    y = (yf_ref[0, 0].astype(F32) + yb_ref[0, 0].astype(F32)) * _silu(z_ref[0].astype(F32))
    y = y * lax.rsqrt(jnp.mean(y * y, axis=-1, keepdims=True) + RMS_EPS) * nw_ref[...]
    o = _dot(y.astype(BF16), wy_ref[...]) + _dot(a_ref[0], wa_ref[...])
    gate = mod_ref[0][2:3, :]
    o_ref[0] = _layer_norm(ALPHA * h_ref[0] + gate * o, g_ref[...], b_ref[...])


def _hyb_out(yd, pm, att, h, mod, norm_w, wy, wa, ln_g, ln_b, nxt):
    nb, t, _ = h.shape
    nt = t // TM
    zcol = SSD_CONV_DIM // SSD_INNER
    return pl.pallas_call(
        _hyb_out_kernel,
        grid=(nb, nt),
        in_specs=[pl.BlockSpec((1, 1, TM, SSD_INNER), lambda b, i: (0, b, i, 0)),
                  pl.BlockSpec((1, 1, TM, SSD_INNER), lambda b, i: (1, b, i, 0)),
                  pl.BlockSpec((1, TM, SSD_INNER), lambda b, i: (b, i, zcol)),
                  pl.BlockSpec((1, TM, NA_WIDTH), lambda b, i: (b, i, 0)),
                  pl.BlockSpec((1, TM, D_MODEL), lambda b, i: (b, i, 0)),
                  _mod_spec(nxt, nb),
                  _resident(norm_w.shape), _resident(wy.shape), _resident(wa.shape),
                  _resident(ln_g.shape), _resident(ln_b.shape)],
        out_specs=pl.BlockSpec((1, TM, D_MODEL), lambda b, i: (b, i, 0)),
        out_shape=jax.ShapeDtypeStruct((nb, t, D_MODEL), F32),
        name="hyb_out",
        compiler_params=_params(("parallel", "parallel")),
    )(yd, yd, pm, att, h, mod, norm_w, wy, wa, ln_g, ln_b)


def _ffn_kernel(hm_ref, hp_ref, hn_ref, mod_ref, wup_ref, cw_ref, cb_ref, wdn_ref, g_ref, b_ref, o_ref,
                hv_ref, hg_ref, acc_ref, *, nxt, nt):
    i = pl.program_id(1)
    m = mod_ref[0]
    shift, scale, gate = m[3:4, :], m[4:5, :], m[5:6, :]
    seg_start = jnp.logical_or(i == 0, i == nxt)
    seg_end = jnp.logical_or(i == nxt - 1, i == nt - 1)
    hm = hm_ref[0]
    up = jnp.where(seg_start, 0.0, hp_ref[0] * (1.0 + scale) + shift)
    un = jnp.where(seg_end, 0.0, hn_ref[0] * (1.0 + scale) + shift)
    uext = jnp.concatenate([up, hm * (1.0 + scale) + shift, un], axis=0).astype(BF16)

    def conv(ref, col0):
        w = cw_ref[:, col0:col0 + FFN_CH]
        return (w[0:1, :] * ref[pl.ds(HALO_F32 - 1, TM), :] + w[1:2, :] * ref[pl.ds(HALO_F32, TM), :]
                + w[2:3, :] * ref[pl.ds(HALO_F32 + 1, TM), :] + cb_ref[:, col0:col0 + FFN_CH])

    n_ch = D_FF // FFN_CH

    def up_proj(c):
        hv_ref[c % 2] = _dot(uext, wup_ref[:, c * FFN_CH:(c + 1) * FFN_CH])
        hg_ref[c % 2] = _dot(uext, wup_ref[:, D_FF + c * FFN_CH:D_FF + (c + 1) * FFN_CH])

    up_proj(0)
    for c in range(n_ch):
        v0 = c * FFN_CH
        if c + 1 < n_ch:
            up_proj(c + 1)
        act = (_silu(conv(hg_ref.at[c % 2], D_FF + v0)) * conv(hv_ref.at[c % 2], v0)).astype(BF16)
        contrib = _dot(act, wdn_ref[v0:v0 + FFN_CH, :])
        if c == 0:
            acc_ref[...] = contrib
        else:
            acc_ref[...] += contrib
    o_ref[0] = _layer_norm(ALPHA * hm + gate * acc_ref[...], g_ref[...], b_ref[...])


def _ffn(h, mod, w_up, conv_w, conv_b, w_dn, ln_g, ln_b, nxt, n_out_tiles):
    nb, t, _ = h.shape
    nt = t // TM
    hb = TM // HALO_F32
    last_hb = t // HALO_F32 - 1
    return pl.pallas_call(
        functools.partial(_ffn_kernel, nxt=nxt, nt=nt),
        grid=(nb, n_out_tiles),
        in_specs=[pl.BlockSpec((1, TM, D_MODEL), lambda b, i: (b, i, 0)),
                  pl.BlockSpec((1, HALO_F32, D_MODEL), lambda b, i: (b, jnp.maximum(i * hb - 1, 0), 0)),
                  pl.BlockSpec((1, HALO_F32, D_MODEL), lambda b, i: (b, jnp.minimum((i + 1) * hb, last_hb), 0)),
                  _mod_spec(nxt, nb),
                  _resident(w_up.shape), _resident(conv_w.shape), _resident(conv_b.shape),
                  _resident(w_dn.shape), _resident(ln_g.shape), _resident(ln_b.shape)],
        out_specs=pl.BlockSpec((1, TM, D_MODEL), lambda b, i: (b, i, 0)),
        out_shape=jax.ShapeDtypeStruct((nb, n_out_tiles * TM, D_MODEL), F32),
        scratch_shapes=[pltpu.VMEM((2, TM + 2 * HALO_F32, FFN_CH), F32),
                        pltpu.VMEM((2, TM + 2 * HALO_F32, FFN_CH), F32),
                        pltpu.VMEM((TM, D_MODEL), F32)],
        name="conv_ffn",
        compiler_params=_params(("parallel", "parallel")),
    )(h, h, h, mod, w_up, conv_w, conv_b, w_dn, ln_g, ln_b)


def _diff_proj_kernel(h_ref, mod_ref, w_ref, cos_ref, sin_ref, cost_ref, sint_ref, qt_ref, k_ref, vt_ref):
    m = mod_ref[0]
    u = (h_ref[0] * (1.0 + m[1:2, :]) + m[0:1, :]).astype(BF16)
    half = LANES // 2
    for n0 in range(0, D_MODEL, HYB_NCH):
        y_t = _dot(u, w_ref[:, n0:n0 + HYB_NCH]).T
        for p0 in range(0, HYB_NCH, LANES):
            blk = y_t[p0:p0 + LANES, :]
            partner = jnp.concatenate([blk[half:, :], blk[:half, :]], axis=0)
            qt_ref[0, n0 + p0:n0 + p0 + LANES, :] = (blk * cost_ref[...] + partner * sint_ref[...]).astype(BF16)
    for n0 in range(0, D_MODEL, HYB_NCH):
        y = _dot(u, w_ref[:, D_MODEL + n0:D_MODEL + n0 + HYB_NCH])
        parts = []
        for p0 in range(0, HYB_NCH, LANES):
            yp = y[:, p0:p0 + LANES]
            parts.append(yp * cos_ref[...] + pltpu.roll(yp, half, axis=1) * sin_ref[...])
        k_ref[0, :, n0:n0 + HYB_NCH] = jnp.concatenate(parts, axis=1).astype(BF16)
    hd = 2 * DIFF_SUB
    for n0 in range(0, D_MODEL, HYB_NCH):
        y_t = _dot(u, w_ref[:, 2 * D_MODEL + n0:2 * D_MODEL + n0 + HYB_NCH]).T
        for p0 in range(0, HYB_NCH, hd):
            vt_ref[0, (n0 + p0) // hd, 0:hd, :] = y_t[p0:p0 + hd, :].astype(BF16)
    vt_ref[0, :, hd:DIFF_VROWS, :] = jnp.ones((DIFF_HEADS, DIFF_VROWS - hd, TM), BF16)


def _diff_proj(h, mod, w, cos_t, sin_t, nxt):
    nb, t, _ = h.shape
    nt = t // TM
    return pl.pallas_call(
        _diff_proj_kernel,
        grid=(nb, nt),
        in_specs=[pl.BlockSpec((1, TM, D_MODEL), lambda b, i: (b, i, 0)),
                  _mod_spec(nxt, nb),
                  _resident(w.shape),
                  pl.BlockSpec((TM, LANES), lambda b, i: (i, 0)),
                  pl.BlockSpec((TM, LANES), lambda b, i: (i, 0)),
                  pl.BlockSpec((LANES, TM), lambda b, i: (0, i)),
                  pl.BlockSpec((LANES, TM), lambda b, i: (0, i))],
        out_specs=[pl.BlockSpec((1, D_MODEL, TM), lambda b, i: (b, 0, i)),
                   pl.BlockSpec((1, TM, D_MODEL), lambda b, i: (b, i, 0)),
                   pl.BlockSpec((1, DIFF_HEADS, DIFF_VROWS, TM), lambda b, i: (b, 0, 0, i))],
        out_shape=[jax.ShapeDtypeStruct((nb, D_MODEL, t), BF16),
                   jax.ShapeDtypeStruct((nb, t, D_MODEL), BF16),
                   jax.ShapeDtypeStruct((nb, DIFF_HEADS, DIFF_VROWS, t), BF16)],
        name="diff_proj",
        compiler_params=_params(("parallel", "parallel")),
    )(h, mod, w, cos_t, sin_t, cos_t.T, sin_t.T)


def _diff_attn_kernel(qt_ref, k_ref, vt_ref, lam_ref, sw_ref, o_ref, *, n_lat, n_ctx, nxt, lam_init):
    i = pl.program_id(2)
    lp = lam_ref[...]
    lam = (jnp.exp(jnp.sum(lp[0:1, :] * lp[1:2, :], axis=1, keepdims=True))
           - jnp.exp(jnp.sum(lp[2:3, :] * lp[3:4, :], axis=1, keepdims=True)) + lam_init)
    hd = 2 * DIFF_SUB
    qt = qt_ref[0]
    row = lax.broadcasted_iota(jnp.int32, qt.shape, 0)
    sub0 = (row % DIFF_SUB) < (DIFF_SUB // 2)
    zero = jnp.zeros_like(qt)

    qms = (jnp.where(sub0, qt, zero), jnp.where(sub0, zero, qt))

    def run(k0, nk):
        chunks = [(k0 + c0, min(DIFF_KCH, nk - c0)) for c0 in range(0, nk, DIFF_KCH)]

        def scores(ci):
            c0, n = chunks[ci]
            return [_dot(k_ref[0, c0:c0 + n, :], qm) for qm in qms]

        s_next = scores(0)
        mx = [None, None]
        acc = [None, None]
        for ci, (c0, n) in enumerate(chunks):
            s_cur = s_next
            if ci + 1 < len(chunks):
                s_next = scores(ci + 1)
            vv = vt_ref[0, 0, :, c0:c0 + n]
            for sub in range(2):
                cm = jnp.max(s_cur[sub], axis=0, keepdims=True)
                m_new = cm if ci == 0 else jnp.maximum(mx[sub], cm)
                pv = _dot(vv, jnp.exp2(s_cur[sub] - m_new).astype(BF16))
                acc[sub] = pv if ci == 0 else acc[sub] * jnp.exp2(mx[sub] - m_new) + pv
                mx[sub] = m_new
        outs = [a[:hd, :] * (1.0 / a[hd:hd + 1, :]) for a in acc]
        o = outs[0] - lam * outs[1]
        o = o * lax.rsqrt(jnp.mean(o * o, axis=0, keepdims=True) + RMS_EPS) * sw_ref[...] * (1.0 - lam_init)
        o_ref[0] = o.T.astype(BF16)

    @pl.when(i < nxt)
    def _():
        run(0, n_lat + n_ctx)

    @pl.when(i >= nxt)
    def _():
        run(n_lat, n_ctx)


def _diff_attn(qt, k, vt, lam_p, subln_b, n_lat, lam_init, n_q_tiles, tq):
    nb, t, _ = k.shape
    n_ctx = t - n_lat
    nxt = n_lat // tq
    hd = 2 * DIFF_SUB
    return pl.pallas_call(
        functools.partial(_diff_attn_kernel, n_lat=n_lat, n_ctx=n_ctx, nxt=nxt, lam_init=lam_init),
        grid=(nb, DIFF_HEADS, n_q_tiles),
        in_specs=[pl.BlockSpec((1, hd, tq), lambda b, hh, i: (b, hh, i)),
                  pl.BlockSpec((1, t, hd), lambda b, hh, i: (b, 0, hh)),
                  pl.BlockSpec((1, 1, DIFF_VROWS, t), lambda b, hh, i: (b, hh, 0, 0)),
                  _resident(lam_p.shape), _resident(subln_b.shape)],
        out_specs=pl.BlockSpec((1, tq, hd), lambda b, hh, i: (b, i, hh)),
        out_shape=jax.ShapeDtypeStruct((nb, n_q_tiles * tq, D_MODEL), BF16),
        name="diff_attn",
        compiler_params=_params(("parallel", "parallel", "arbitrary")),
    )(qt, k, vt, lam_p, subln_b)


def _diff_out_kernel(a_ref, h_ref, mod_ref, w_ref, g_ref, b_ref, o_ref):
    o = _dot(a_ref[0], w_ref[...])
    gate = mod_ref[0][2:3, :]
    o_ref[0] = _layer_norm(ALPHA * h_ref[0] + gate * o, g_ref[...], b_ref[...])


def _diff_out(att, h, mod, w, ln_g, ln_b, nxt, n_tiles):
    nb, t, _ = h.shape
    return pl.pallas_call(
        _diff_out_kernel,
        grid=(nb, n_tiles),
        in_specs=[pl.BlockSpec((1, TM, D_MODEL), lambda b, i: (b, i, 0)),
                  pl.BlockSpec((1, TM, D_MODEL), lambda b, i: (b, i, 0)),
                  _mod_spec(nxt, nb),
                  _resident(w.shape), _resident(ln_g.shape), _resident(ln_b.shape)],
        out_specs=pl.BlockSpec((1, TM, D_MODEL), lambda b, i: (b, i, 0)),
        out_shape=jax.ShapeDtypeStruct((nb, n_tiles * TM, D_MODEL), F32),
        name="diff_out",
        compiler_params=_params(("parallel", "parallel")),
    )(att, h, mod, w, ln_g, ln_b)


def _rope_tables(n_lat, n_ctx):
    tok = np.arange(n_lat)
    row = (tok // GRID_W).astype(np.float32)
    col = (tok % GRID_W).astype(np.float32)
    n_freq = DIFF_SUB // 4
    inv = jnp.asarray(ROPE_BASE, F32) ** (-jnp.arange(n_freq, dtype=F32) / n_freq)
    ang = jnp.concatenate([jnp.asarray(row)[:, None] * inv, jnp.asarray(col)[:, None] * inv], axis=-1)
    cos = jnp.cos(ang)
    sin = jnp.sin(ang)
    cos_t = jnp.concatenate([cos, cos, cos, cos], axis=-1)
    sin_t = jnp.concatenate([-sin, -sin, sin, sin], axis=-1)
    cos_t = jnp.concatenate([cos_t, jnp.ones((n_ctx, LANES), F32)], axis=0)
    sin_t = jnp.concatenate([sin_t, jnp.zeros((n_ctx, LANES), F32)], axis=0)
    return cos_t, sin_t


def _diff_head_perm():
    half = DIFF_SUB // 2
    perm = []
    for hh in range(DIFF_HEADS):
        base = hh * 2 * DIFF_SUB
        for part in range(2):
            for sub in range(2):
                start = base + sub * DIFF_SUB + part * half
                perm.extend(range(start, start + half))
    return np.asarray(perm)


def kernel(x, c, ctx, c_ctx, ada_w, ada_b, ln1_g, ln1_b, ln2_g, ln2_b, ffn_w_up, ffn_conv_w, ffn_conv_b, ffn_w_down, hyb_w_in, ssd_conv_w, ssd_conv_b, ssd_a_log, ssd_dt_bias, ssd_d, ssd_norm_w, na_rpb, hyb_w_out, diff_w_in, diff_lambda, diff_subln_w, diff_w_out):
    nb, n_lat, d = x.shape
    n_ctx = ctx.shape[1]
    assert d == D_MODEL and n_lat % TM == 0 and n_ctx % TM == 0 and n_lat % (NA_WIN_ROWS * GRID_W) == 0
    t = n_lat + n_ctx
    nt = t // TM
    nxt = n_lat // TM

    h = jnp.concatenate([x, ctx], axis=1)
    cond_rows = -(-(nb + 1) // 8) * 8
    cond = jnp.concatenate([c, c_ctx[None, :], jnp.zeros((cond_rows - nb - 1, d), F32)], axis=0)
    mod_all = _ada_mod(cond, ada_w, ada_b).reshape(DEPTH, cond_rows, 6, d)
    cos_t, sin_t = _rope_tables(n_lat, n_ctx)
    perm = _diff_head_perm()
    na_variants, na_vid = _na_variants(n_lat // GRID_W)
    expand = jnp.asarray(np.kron(np.eye(LANES, SSD_H, dtype=np.float32), np.ones((1, SSD_P), np.float32)))

    for i in range(DEPTH):
        last = i == DEPTH - 1
        j = i // 2
        mod = mod_all[i]
        n_tiles = nxt if last else nt
        row = lambda v: v.reshape(1, -1)
        if i % 2 == 0:
            w_in = hyb_w_in[j]
            o_xbc = SSD_INNER
            o_dt = o_xbc + SSD_CONV_DIM
            o_q = o_dt + 2 * SSD_H
            w_main = jnp.concatenate([w_in[:, o_xbc:o_dt], w_in[:, :SSD_INNER],
                                      w_in[:, o_q:o_q + NA_WIDTH] * NA_DH ** -0.5,
                                      w_in[:, o_q + NA_WIDTH:]], axis=1).astype(BF16)
            w_dt = jnp.zeros((d, 2 * LANES), F32)
            w_dt = w_dt.at[:, :SSD_H].set(w_in[:, o_dt:o_dt + SSD_H])
            w_dt = w_dt.at[:, LANES:LANES + SSD_H].set(w_in[:, o_dt + SSD_H:o_q]).astype(BF16)
            pm, dt, na_qt, na_k, na_vt = _hyb_proj(h, mod, w_main, w_dt, nxt)

            pad = lambda v: jnp.zeros((2, 1, LANES), F32).at[:, 0, :SSD_H].set(v)
            arow = pad(-jnp.exp(ssd_a_log[j].astype(F32)))
            brow = pad(ssd_dt_bias[j].astype(F32))
            dsk = jnp.repeat(ssd_d[j].astype(F32), SSD_P, axis=-1).reshape(2, 1, SSD_INNER)
            yd = _ssd(pm, dt, ssd_conv_w[j], row(ssd_conv_b[j]), arow, brow, dsk, expand, n_lat)
            att = _na(na_qt, na_k, na_vt, _na_bias_table(na_rpb[j], na_variants), na_vid, n_lat)
            w_out = hyb_w_out[j].astype(BF16)
            h = _hyb_out(yd, pm, att, h, mod, row(ssd_norm_w[j]), w_out[:SSD_INNER], w_out[SSD_INNER:],
                         row(ln1_g[i]), row(ln1_b[i]), nxt)
        else:
            lam_init = 0.8 - 0.6 * math.exp(-0.3 * i)
            w_in = diff_w_in[j]
            wq = w_in[:, :d][:, perm] * (DIFF_SUB ** -0.5 * LOG2E)
            wk = w_in[:, d:2 * d][:, perm]
            w_qkv = jnp.concatenate([wq, wk, w_in[:, 2 * d:]], axis=1).astype(BF16)
            d_qt, d_k, d_vt = _diff_proj(h, mod, w_qkv, cos_t, sin_t, nxt)
            subln_b = jnp.broadcast_to(diff_subln_w[j].astype(F32)[:, None], (2 * DIFF_SUB, TM))
            att = _diff_attn(d_qt, d_k, d_vt, diff_lambda[j].astype(F32), subln_b, n_lat, lam_init, n_tiles, TM)
            h = _diff_out(att, h, mod, diff_w_out[j].astype(BF16), row(ln1_g[i]), row(ln1_b[i]), nxt, n_tiles)
        h = _ffn(h, mod, ffn_w_up[i].astype(BF16), ffn_conv_w[i], row(ffn_conv_b[i]),
                 ffn_w_down[i].astype(BF16), row(ln2_g[i]), row(ln2_b[i]), nxt, n_tiles)
    return h
```

```python
import functools
import math

import numpy as np
import jax
import jax.numpy as jnp
from jax import lax
from jax.experimental import pallas as pl
from jax.experimental.pallas import tpu as pltpu

F32 = jnp.float32
BF16 = jnp.bfloat16

D_MODEL = 1024
DEPTH = 4
GRID_W = 64
SSD_P = 64
SSD_H = 16
SSD_G = 4
SSD_R = SSD_H // SSD_G
SSD_N = 128
SSD_INNER = SSD_H * SSD_P
SSD_GN = SSD_G * SSD_N
SSD_CONV_W = 5
SSD_CONV_DIM = SSD_INNER + 2 * SSD_GN
SSD_CHUNK = 128
NA_HEADS = 8
NA_DH = 64
NA_WIDTH = NA_HEADS * NA_DH
NA_WIN_ROWS = 8
NA_WIN_COLS = 16
DIFF_HEADS = 8
DIFF_SUB = 64
ROPE_BASE = 10000.0
D_FF = 2816
FFN_CH = 256
ALPHA = (2.0 * DEPTH) ** 0.25
LN_EPS = 1e-5
RMS_EPS = 1e-5

LANES = 128
TM = 256
HALO_F32 = 8
HALO_BF16 = 16
VMEM_LIMIT = 56 * 1024 * 1024
HYB_SSD = SSD_CONV_DIM + SSD_INNER
HYB_MAIN = HYB_SSD + 3 * NA_WIDTH
NA_VROWS = NA_DH + HALO_BF16
NA_BLK_ROWS = 4
NA_KEY_BLKS = 3
DIFF_VROWS = 2 * DIFF_SUB + HALO_BF16
DIFF_KCH = 512
LOG2E = 1.4426950408889634
SUM_LO = 2.0 ** -90
SUM_HI = 2.0 ** 100
HYB_NCH = 512


def _params(sem, vmem=VMEM_LIMIT):
    return pltpu.CompilerParams(dimension_semantics=sem, vmem_limit_bytes=vmem)


def _resident(shape):
    nd = len(shape)
    return pl.BlockSpec(shape, lambda *_: (0,) * nd, pipeline_mode=pl.Buffered(1))


def _dot(a, b):
    return jnp.dot(a, b, preferred_element_type=F32)


def _dot_nt(a, b):
    return lax.dot_general(a, b, (((1,), (1,)), ((), ())), preferred_element_type=F32)


def _silu(v):
    return v * jax.nn.sigmoid(v)


def _layer_norm(r, g, b):
    mu = jnp.mean(r, axis=-1, keepdims=True)
    xc = r - mu
    var = jnp.mean(xc * xc, axis=-1, keepdims=True)
    return xc * lax.rsqrt(var + LN_EPS) * g + b


def _mod_spec(nxt, nb):
    return pl.BlockSpec((1, 6, D_MODEL), lambda b, i: (jnp.where(i >= nxt, nb, b), 0, 0))


def _ada_kernel(c_ref, w_ref, b_ref, o_ref):
    s = _silu(c_ref[...]).astype(BF16)
    o_ref[0] = _dot(s, w_ref[0].astype(BF16)) + b_ref[0]


def _ada_mod(cond, ada_w, ada_b):
    rows = cond.shape[0]
    n = ada_w.shape[-1]
    tn = n // 4
    return pl.pallas_call(
        _ada_kernel,
        grid=(DEPTH, n // tn),
        in_specs=[pl.BlockSpec((rows, D_MODEL), lambda l, j: (0, 0)),
                  pl.BlockSpec((1, D_MODEL, tn), lambda l, j: (l, 0, j)),
                  pl.BlockSpec((1, 1, tn), lambda l, j: (l, 0, j))],
        out_specs=pl.BlockSpec((1, rows, tn), lambda l, j: (l, 0, j)),
        out_shape=jax.ShapeDtypeStruct((DEPTH, rows, n), F32),
        name="ada_mod",
        compiler_params=_params(("parallel", "parallel")),
    )(cond, ada_w, ada_b.reshape(DEPTH, 1, n))


def _hyb_proj_kernel(h_ref, hp_ref, hn_ref, mod_ref, w_ref, wdt_ref, cw_ref, cb_ref,
                     o_ref, dt_ref, qt_ref, k_ref, vt_ref, ext_ref, *, nxt, nt):
    i = pl.program_id(1)
    m = mod_ref[0]
    shift, scale = m[0:1, :], m[1:2, :]
    seg_start = jnp.logical_or(i == 0, i == nxt)
    seg_end = jnp.logical_or(i == nxt - 1, i == nt - 1)
    um = h_ref[0] * (1.0 + scale) + shift
    u = um.astype(BF16)
    up = jnp.where(seg_start, 0.0, hp_ref[0] * (1.0 + scale) + shift)
    un = jnp.where(seg_end, 0.0, hn_ref[0] * (1.0 + scale) + shift)
    uext = jnp.concatenate([up, um, un], axis=0).astype(BF16)

    def xbc_proj(c):
        ext_ref[c % 2] = _dot(uext, w_ref[:, c * HYB_NCH:(c + 1) * HYB_NCH])

    n_ch = SSD_CONV_DIM // HYB_NCH
    xbc_proj(0)
    for c in range(n_ch):
        n0 = c * HYB_NCH
        if c + 1 < n_ch:
            xbc_proj(c + 1)
        acc = cb_ref[:, n0:n0 + HYB_NCH]
        for k in range(SSD_CONV_W):
            acc = acc + cw_ref[k:k + 1, n0:n0 + HYB_NCH] * ext_ref[c % 2, pl.ds(HALO_F32 - SSD_CONV_W // 2 + k, TM), :]
        o_ref[0, :, n0:n0 + HYB_NCH] = _silu(acc).astype(BF16)
    for n0 in range(SSD_CONV_DIM, HYB_SSD, HYB_NCH):
        o_ref[0, :, n0:n0 + HYB_NCH] = _dot(u, w_ref[:, n0:n0 + HYB_NCH]).astype(BF16)
    dt_ref[0] = _dot(u, wdt_ref[...])
    qt_ref[0] = _dot(u, w_ref[:, HYB_SSD:HYB_SSD + NA_WIDTH]).T.astype(BF16)
    k_ref[0] = _dot(u, w_ref[:, HYB_SSD + NA_WIDTH:HYB_SSD + 2 * NA_WIDTH]).astype(BF16)
    v_t = _dot(u, w_ref[:, HYB_SSD + 2 * NA_WIDTH:HYB_MAIN]).T
    for hh in range(NA_HEADS):
        vt_ref[0, hh, 0:NA_DH, :] = v_t[hh * NA_DH:(hh + 1) * NA_DH, :].astype(BF16)
        vt_ref[0, hh, NA_DH:NA_VROWS, :] = jnp.ones((NA_VROWS - NA_DH, TM), BF16)


def _hyb_proj(h, mod, w_main, w_dt, conv_w, conv_b, nxt):
    nb, t, _ = h.shape
    nt = t // TM
    hb = TM // HALO_F32
    last_hb = t // HALO_F32 - 1
    return pl.pallas_call(
        functools.partial(_hyb_proj_kernel, nxt=nxt, nt=nt),
        grid=(nb, nt),
        in_specs=[pl.BlockSpec((1, TM, D_MODEL), lambda b, i: (b, i, 0)),
                  pl.BlockSpec((1, HALO_F32, D_MODEL), lambda b, i: (b, jnp.maximum(i * hb - 1, 0), 0)),
                  pl.BlockSpec((1, HALO_F32, D_MODEL), lambda b, i: (b, jnp.minimum((i + 1) * hb, last_hb), 0)),
                  _mod_spec(nxt, nb),
                  _resident(w_main.shape),
                  _resident(w_dt.shape),
                  _resident(conv_w.shape),
                  _resident(conv_b.shape)],
        out_specs=[pl.BlockSpec((1, TM, HYB_SSD), lambda b, i: (b, i, 0)),
                   pl.BlockSpec((1, TM, 2 * LANES), lambda b, i: (b, i, 0)),
                   pl.BlockSpec((1, NA_WIDTH, TM), lambda b, i: (b, 0, i)),
                   pl.BlockSpec((1, TM, NA_WIDTH), lambda b, i: (b, i, 0)),
                   pl.BlockSpec((1, NA_HEADS, NA_VROWS, TM), lambda b, i: (b, 0, 0, i))],
        out_shape=[jax.ShapeDtypeStruct((nb, t, HYB_SSD), BF16),
                   jax.ShapeDtypeStruct((nb, t, 2 * LANES), F32),
                   jax.ShapeDtypeStruct((nb, NA_WIDTH, t), BF16),
                   jax.ShapeDtypeStruct((nb, t, NA_WIDTH), BF16),
                   jax.ShapeDtypeStruct((nb, NA_HEADS, NA_VROWS, t), BF16)],
        scratch_shapes=[pltpu.VMEM((2, TM + 2 * HALO_F32, HYB_NCH), F32)],
        name="hyb_proj",
        compiler_params=_params(("parallel", "parallel")),
    )(h, h, h, mod, w_main, w_dt, conv_w, conv_b)


def _ssd_chunk_index(d, j, nxc, nc):
    return jnp.where(d == 0, (j + nxc) % nc, nc - 1 - j)


def _ssd_kernel(u_ref, dt_ref, arow_ref, brow_ref, dsk_ref, e_ref, y_ref, st_ref):
    d = pl.program_id(1)
    j = pl.program_id(2)
    q = SSD_CHUNK

    @pl.when(j == 0)
    def _():
        st_ref[...] = jnp.zeros_like(st_ref)

    u = u_ref[0].astype(F32)
    xs = u[:, :SSD_INNER]

    dtr = dt_ref[0] + brow_ref[0]
    dtv = jnp.maximum(dtr, 0.0) + jnp.log1p(jnp.exp(-jnp.abs(dtr)))
    adt = dtv * arow_ref[0]
    ri = lax.broadcasted_iota(jnp.int32, (q, q), 0)
    ci = lax.broadcasted_iota(jnp.int32, (q, q), 1)
    tri = jnp.where(d == 0, ri - ci, ci - ri) >= 0
    cs = jnp.dot(tri.astype(F32), adt, precision=lax.Precision.HIGHEST, preferred_element_type=F32)
    cs_t = cs.T
    dt_t = dtv.T
    tot = jnp.where(d == 0, cs[q - 1:q, :], cs[0:1, :])
    tot_t = jnp.where(d == 0, cs_t[:, q - 1:q], cs_t[:, 0:1])
    w_t = jnp.exp(tot_t - cs_t) * dt_t
    dec_row = jnp.dot(jnp.broadcast_to(jnp.exp(tot), (8, LANES)), e_ref[...],
                      precision=lax.Precision.HIGHEST, preferred_element_type=F32)[0:1]

    lane = lax.broadcasted_iota(jnp.int32, (q, LANES), 1)
    lo = lane < SSD_P
    dsk = dsk_ref[0]
    for g in range(SSD_G):
        bm = u[:, SSD_INNER + g * SSD_N:SSD_INNER + (g + 1) * SSD_N]
        cm = u[:, SSD_INNER + SSD_GN + g * SSD_N:SSD_INNER + SSD_GN + (g + 1) * SSD_N]
        cmb = cm.astype(BF16)
        cb = _dot_nt(cmb, bm.astype(BF16))
        bm_t = bm.T
        s_prev = st_ref[g]
        y_off = _dot(cmb, s_prev.astype(BF16))
        s_parts = []
        for pr in range(SSD_R // 2):
            col0 = g * SSD_R * SSD_P + pr * LANES
            xs_pair = xs[:, col0:col0 + LANES]
            xsb = xs_pair.astype(BF16)
            yd, sc, colbs = [], [], []
            for sub in range(2):
                h = g * SSD_R + pr * 2 + sub
                colb = jnp.broadcast_to(cs[:, h:h + 1], (q, q))
                decay = jnp.exp(jnp.where(tri, colb - cs_t[h:h + 1, :], -jnp.inf))
                gmat = (cb * decay * dt_t[h:h + 1, :]).astype(BF16)
                yd.append(_dot(gmat, xsb))
                sc.append(_dot((bm_t * w_t[h:h + 1, :]).astype(BF16), xsb))
                colbs.append(colb)
            y_diag = jnp.where(lo, yd[0], yd[1])
            e_col = jnp.exp(jnp.where(lo, colbs[0], colbs[1]))
            y_pair = y_diag + y_off[:, pr * LANES:(pr + 1) * LANES] * e_col + dsk[:, col0:col0 + LANES] * xs_pair
            y_ref[0, 0, :, col0:col0 + LANES] = y_pair.astype(BF16)
            s_parts.append(jnp.where(lo, sc[0], sc[1]))
        g0 = g * SSD_R * SSD_P
        st_ref[g] = s_prev * dec_row[:, g0:g0 + SSD_R * SSD_P] + jnp.concatenate(s_parts, axis=1)


def _ssd(pm, dt, arow, brow, dsk, expand, n_lat):
    nb, t, _ = pm.shape
    q = SSD_CHUNK
    nc = t // q
    nxc = n_lat // q

    def cidx(d, j):
        return _ssd_chunk_index(d, j, nxc, nc)

    return pl.pallas_call(
        _ssd_kernel,
        grid=(nb, 2, nc),
        in_specs=[pl.BlockSpec((1, q, SSD_CONV_DIM), lambda b, d, j: (b, cidx(d, j), 0)),
                  pl.BlockSpec((1, q, LANES), lambda b, d, j: (b, cidx(d, j), d)),
                  pl.BlockSpec((1, 1, LANES), lambda b, d, j: (d, 0, 0)),
                  pl.BlockSpec((1, 1, LANES), lambda b, d, j: (d, 0, 0)),
                  pl.BlockSpec((1, 1, SSD_INNER), lambda b, d, j: (d, 0, 0)),
                  _resident(expand.shape)],
        out_specs=pl.BlockSpec((1, 1, q, SSD_INNER), lambda b, d, j: (d, b, cidx(d, j), 0)),
        out_shape=jax.ShapeDtypeStruct((2, nb, t, SSD_INNER), BF16),
        scratch_shapes=[pltpu.VMEM((SSD_G, SSD_N, SSD_R * SSD_P), F32)],
        name="ssd_scan",
        compiler_params=_params(("parallel", "parallel", "arbitrary")),
    )(pm, dt, arow, brow, dsk, expand)


def _na_kernel(qt_ref, k0_ref, k1_ref, k2_ref, kc_ref, v0_ref, v1_ref, v2_ref, vc_ref, bias_ref, o_ref, *, nblk):
    i = pl.program_id(1)
    nq = NA_BLK_ROWS * GRID_W
    zeros = jnp.zeros((NA_DH, nq), BF16)
    local = ((k0_ref, v0_ref), (k1_ref, v1_ref), (k2_ref, v2_ref))

    def run(with_local):
        def scores(hh):
            sl = slice((hh // 2) * LANES, (hh // 2 + 1) * LANES)
            qh = qt_ref[0, hh * NA_DH:(hh + 1) * NA_DH, :]
            qm = jnp.concatenate([qh, zeros] if hh % 2 == 0 else [zeros, qh], axis=0)
            sc = []
            if with_local:
                for jb, (kr, _) in enumerate(local):
                    sc.append(_dot(kr[0, :, sl], qm) + bias_ref[0, hh, jb * nq:(jb + 1) * nq, :])
            sc.append(_dot(kc_ref[0, :, sl], qm))
            return sc

        outs = []
        s_next = scores(0)
        for hh in range(NA_HEADS):
            s_cur = s_next
            if hh + 1 < NA_HEADS:
                s_next = scores(hh + 1)
            vals = ([vr[0, hh] for _, vr in local] if with_local else []) + [vc_ref[0, hh]]
            mx = functools.reduce(jnp.maximum, [jnp.max(s, axis=0, keepdims=True) for s in s_cur])
            o_t = functools.reduce(
                lambda a, b: a + b, [_dot(v, jnp.exp(s - mx).astype(BF16)) for s, v in zip(s_cur, vals)])
            outs.append(o_t[:NA_DH, :] * (1.0 / o_t[NA_DH:NA_DH + 1, :]))
        o_ref[0] = jnp.concatenate(outs, axis=0).T.astype(BF16)

    @pl.when(i < nblk)
    def _():
        run(True)

    @pl.when(i >= nblk)
    def _():
        run(False)


def _na_variants(rows):
    nblk = rows // NA_BLK_ROWS
    assert rows % NA_BLK_ROWS == 0 and nblk >= NA_KEY_BLKS
    keys, vid = [], []
    for b in range(nblk):
        ws = int(np.clip(b - 1, 0, nblk - NA_KEY_BLKS)) * NA_BLK_ROWS
        key = []
        for a in range(NA_BLK_ROWS):
            r = b * NA_BLK_ROWS + a
            r0 = int(np.clip(r - NA_WIN_ROWS // 2, 0, rows - NA_WIN_ROWS))
            assert 0 <= r0 - ws and r0 - ws + NA_WIN_ROWS <= NA_KEY_BLKS * NA_BLK_ROWS
            key.append((r0 - ws, r0 - r))
        key = tuple(key)
        if key not in keys:
            keys.append(key)
        vid.append(keys.index(key))
    return keys, vid


def _na_bias_table(rpb, variants):
    c = np.arange(GRID_W)[:, None]
    kc = np.arange(GRID_W)[None, :]
    c0 = np.clip(c - NA_WIN_COLS // 2, 0, GRID_W - NA_WIN_COLS)
    valid_c = (kc >= c0) & (kc < c0 + NA_WIN_COLS)
    dc = kc - c + NA_WIN_COLS - 1
    onehot = ((dc[None] == np.arange(2 * NA_WIN_COLS - 1)[:, None, None]) & valid_c[None]).astype(np.float32)
    tz = jnp.einsum('hrd,dck->hrck', rpb.astype(F32), jnp.asarray(onehot), precision=lax.Precision.HIGHEST)
    nkr = NA_KEY_BLKS * NA_BLK_ROWS
    ndr = 2 * NA_WIN_ROWS - 1
    tabs = []
    for key in variants:
        sel = np.zeros((NA_BLK_ROWS, nkr, ndr), np.float32)
        valid_r = np.zeros((NA_BLK_ROWS, nkr), bool)
        for a, (off, e) in enumerate(key):
            for w in range(NA_WIN_ROWS):
                sel[a, off + w, e + NA_WIN_ROWS - 1 + w] = 1.0
                valid_r[a, off + w] = True
        tab = jnp.einsum('ajr,hrck->hajck', jnp.asarray(sel), tz, precision=lax.Precision.HIGHEST)
        ok = jnp.asarray(valid_r[None, :, :, None, None] & valid_c[None, None, None, :, :])
        tab = jnp.where(ok, tab, -jnp.inf)
        tabs.append(tab.transpose(0, 2, 4, 1, 3).reshape(NA_HEADS, nkr * GRID_W, NA_BLK_ROWS * GRID_W))
    return jnp.stack(tabs)


def _na(qt, k, vt, bias, vid, n_lat):
    nb, t, _ = k.shape
    n_ctx = t - n_lat
    nq = NA_BLK_ROWS * GRID_W
    nblk = n_lat // nq
    assert n_ctx % nq == 0 and n_lat % n_ctx == 0
    steps = t // nq
    cblk = n_lat // n_ctx

    def wb(i):
        return jnp.clip(i - 1, 0, nblk - NA_KEY_BLKS)

    def variant(i):
        v = jnp.int32(0)
        for blk in range(1, nblk):
            v = jnp.where(i == blk, vid[blk], v)
        return v

    kspec = [pl.BlockSpec((1, nq, NA_WIDTH), functools.partial(lambda b, i, j: (b, wb(i) + j, 0), j=j))
             for j in range(NA_KEY_BLKS)]
    vspec = [pl.BlockSpec((1, NA_HEADS, NA_VROWS, nq), functools.partial(lambda b, i, j: (b, 0, 0, wb(i) + j), j=j))
             for j in range(NA_KEY_BLKS)]
    return pl.pallas_call(
        functools.partial(_na_kernel, nblk=nblk),
        grid=(nb, steps),
        in_specs=([pl.BlockSpec((1, NA_WIDTH, nq), lambda b, i: (b, 0, i))] + kspec
                  + [pl.BlockSpec((1, n_ctx, NA_WIDTH), lambda b, i: (b, cblk, 0))] + vspec
                  + [pl.BlockSpec((1, NA_HEADS, NA_VROWS, n_ctx), lambda b, i: (b, 0, 0, cblk)),
                     pl.BlockSpec((1,) + bias.shape[1:], lambda b, i: (variant(i), 0, 0, 0))]),
        out_specs=pl.BlockSpec((1, nq, NA_WIDTH), lambda b, i: (b, i, 0)),
        out_shape=jax.ShapeDtypeStruct((nb, t, NA_WIDTH), BF16),
        name="na_attn",
        compiler_params=_params(("parallel", "arbitrary")),
    )(qt, k, k, k, k, vt, vt, vt, vt, bias)


def _hyb_out_kernel(yf_ref, yb_ref, z_ref, a_ref, h_ref, mod_ref, nw_ref, wy_ref, wa_ref, g_ref, b_ref, o_ref):
    y = (yf_ref[0, 0].astype(F32) + yb_ref[0, 0].astype(F32)) * _silu(z_ref[0].astype(F32))
    y = y * lax.rsqrt(jnp.mean(y * y, axis=-1, keepdims=True) + RMS_EPS) * nw_ref[...]
    o = _dot(y.astype(BF16), wy_ref[...]) + _dot(a_ref[0], wa_ref[...])
    gate = mod_ref[0][2:3, :]
    o_ref[0] = _layer_norm(ALPHA * h_ref[0] + gate * o, g_ref[...], b_ref[...])


def _hyb_out(yd, pm, att, h, mod, norm_w, wy, wa, ln_g, ln_b, nxt):
    nb, t, _ = h.shape
    nt = t // TM
    zcol = SSD_CONV_DIM // SSD_INNER
    return pl.pallas_call(
        _hyb_out_kernel,
        grid=(nb, nt),
        in_specs=[pl.BlockSpec((1, 1, TM, SSD_INNER), lambda b, i: (0, b, i, 0)),
                  pl.BlockSpec((1, 1, TM, SSD_INNER), lambda b, i: (1, b, i, 0)),
                  pl.BlockSpec((1, TM, SSD_INNER), lambda b, i: (b, i, zcol)),
                  pl.BlockSpec((1, TM, NA_WIDTH), lambda b, i: (b, i, 0)),
                  pl.BlockSpec((1, TM, D_MODEL), lambda b, i: (b, i, 0)),
                  _mod_spec(nxt, nb),
                  _resident(norm_w.shape), _resident(wy.shape), _resident(wa.shape),
                  _resident(ln_g.shape), _resident(ln_b.shape)],
        out_specs=pl.BlockSpec((1, TM, D_MODEL), lambda b, i: (b, i, 0)),
        out_shape=jax.ShapeDtypeStruct((nb, t, D_MODEL), F32),
        name="hyb_out",
        compiler_params=_params(("parallel", "parallel")),
    )(yd, yd, pm, att, h, mod, norm_w, wy, wa, ln_g, ln_b)


def _ffn_kernel(hm_ref, hp_ref, hn_ref, mod_ref, wup_ref, cw_ref, cb_ref, wdn_ref, g_ref, b_ref, o_ref,
                hv_ref, hg_ref, acc_ref, perm_ref, *, nxt, nt):
    i = pl.program_id(1)
    m = mod_ref[0]
    shift, scale, gate = m[3:4, :], m[4:5, :], m[5:6, :]
    seg_start = jnp.logical_or(i == 0, i == nxt)
    seg_end = jnp.logical_or(i == nxt - 1, i == nt - 1)
    sl = HALO_F32
    ngrp = TM // sl
    ncb = D_MODEL // LANES
    for cb in range(ncb):
        perm_ref[cb] = hm_ref[0, :, cb * LANES:(cb + 1) * LANES]
    hm = jnp.concatenate(
        [jnp.concatenate([perm_ref[cb, pl.ds(a, sl, stride=ngrp), :] for a in range(ngrp)], axis=0)
         for cb in range(ncb)], axis=1)
    prev_row = jnp.where(seg_start, 0.0, hp_ref[0, sl - 1:sl, :] * (1.0 + scale) + shift)
    next_row = jnp.where(seg_end, 0.0, hn_ref[0, 0:1, :] * (1.0 + scale) + shift)
    sub_d = lax.broadcasted_iota(jnp.int32, (2 * sl, D_MODEL), 0)
    halo = jnp.where(sub_d == 0, prev_row, jnp.where(sub_d == 1, next_row, 0.0))
    uext = jnp.concatenate([hm * (1.0 + scale) + shift, halo], axis=0).astype(BF16)
    sub_c = lax.broadcasted_iota(jnp.int32, (sl, FFN_CH), 0)

    def conv(ref, col0):
        w = cw_ref[:, col0:col0 + FFN_CH]
        before_first = jnp.where(sub_c == 0, ref[TM:TM + 1, :], pltpu.roll(ref[TM - sl:TM, :], 1, axis=0))
        after_last = jnp.where(sub_c == sl - 1, ref[TM + 1:TM + 2, :], pltpu.roll(ref[0:sl, :], sl - 1, axis=0))
        prev = jnp.concatenate([before_first, ref[0:TM - sl, :]], axis=0)
        nxt_ = jnp.concatenate([ref[sl:TM, :], after_last], axis=0)
        return w[0:1, :] * prev + w[1:2, :] * ref[0:TM, :] + w[2:3, :] * nxt_ + cb_ref[:, col0:col0 + FFN_CH]

    n_ch = D_FF // FFN_CH

    def up_proj(c):
        hv_ref[c % 2] = _dot(uext, wup_ref[:, c * FFN_CH:(c + 1) * FFN_CH])
        hg_ref[c % 2] = _dot(uext, wup_ref[:, D_FF + c * FFN_CH:D_FF + (c + 1) * FFN_CH])

    up_proj(0)
    for c in range(n_ch):
        v0 = c * FFN_CH
        if c + 1 < n_ch:
            up_proj(c + 1)
        act = (_silu(conv(hg_ref.at[c % 2], D_FF + v0)) * conv(hv_ref.at[c % 2], v0)).astype(BF16)
        contrib = _dot(act, wdn_ref[v0:v0 + FFN_CH, :])
        if c == 0:
            acc_ref[...] = contrib
        else:
            acc_ref[...] += contrib
    out = _layer_norm(ALPHA * hm + gate * acc_ref[...], g_ref[...], b_ref[...])
    for cb in range(ncb):
        for a in range(ngrp):
            perm_ref[cb, pl.ds(a, sl, stride=ngrp), :] = out[a * sl:(a + 1) * sl, cb * LANES:(cb + 1) * LANES]
        o_ref[0, :, cb * LANES:(cb + 1) * LANES] = perm_ref[cb]


def _ffn(h, mod, w_up, conv_w, conv_b, w_dn, ln_g, ln_b, nxt, n_out_tiles):
    nb, t, _ = h.shape
    nt = t // TM
    hb = TM // HALO_F32
    last_hb = t // HALO_F32 - 1
    return pl.pallas_call(
        functools.partial(_ffn_kernel, nxt=nxt, nt=nt),
        grid=(nb, n_out_tiles),
        in_specs=[pl.BlockSpec((1, TM, D_MODEL), lambda b, i: (b, i, 0)),
                  pl.BlockSpec((1, HALO_F32, D_MODEL), lambda b, i: (b, jnp.maximum(i * hb - 1, 0), 0)),
                  pl.BlockSpec((1, HALO_F32, D_MODEL), lambda b, i: (b, jnp.minimum((i + 1) * hb, last_hb), 0)),
                  _mod_spec(nxt, nb),
                  _resident(w_up.shape), _resident(conv_w.shape), _resident(conv_b.shape),
                  _resident(w_dn.shape), _resident(ln_g.shape), _resident(ln_b.shape)],
        out_specs=pl.BlockSpec((1, TM, D_MODEL), lambda b, i: (b, i, 0)),
        out_shape=jax.ShapeDtypeStruct((nb, n_out_tiles * TM, D_MODEL), F32),
        scratch_shapes=[pltpu.VMEM((2, TM + 2 * HALO_F32, FFN_CH), F32),
                        pltpu.VMEM((2, TM + 2 * HALO_F32, FFN_CH), F32),
                        pltpu.VMEM((TM, D_MODEL), F32),
                        pltpu.VMEM((D_MODEL // LANES, TM, LANES), F32)],
        name="conv_ffn",
        compiler_params=_params(("parallel", "parallel")),
    )(h, h, h, mod, w_up, conv_w, conv_b, w_dn, ln_g, ln_b)


def _diff_proj_kernel(h_ref, mod_ref, w_ref, cos_ref, sin_ref, cost_ref, sint_ref, qt_ref, k_ref, vt_ref):
    m = mod_ref[0]
    u = (h_ref[0] * (1.0 + m[1:2, :]) + m[0:1, :]).astype(BF16)
    half = LANES // 2
    for n0 in range(0, D_MODEL, HYB_NCH):
        y_t = _dot(u, w_ref[:, n0:n0 + HYB_NCH]).T
        for p0 in range(0, HYB_NCH, LANES):
            blk = y_t[p0:p0 + LANES, :]
            partner = jnp.concatenate([blk[half:, :], blk[:half, :]], axis=0)
            qt_ref[0, n0 + p0:n0 + p0 + LANES, :] = (blk * cost_ref[...] + partner * sint_ref[...]).astype(BF16)
    for n0 in range(0, D_MODEL, HYB_NCH):
        y = _dot(u, w_ref[:, D_MODEL + n0:D_MODEL + n0 + HYB_NCH])
        parts = []
        for p0 in range(0, HYB_NCH, LANES):
            yp = y[:, p0:p0 + LANES]
            parts.append(yp * cos_ref[...] + pltpu.roll(yp, half, axis=1) * sin_ref[...])
        k_ref[0, :, n0:n0 + HYB_NCH] = jnp.concatenate(parts, axis=1).astype(BF16)
    hd = 2 * DIFF_SUB
    for n0 in range(0, D_MODEL, HYB_NCH):
        y_t = _dot(u, w_ref[:, 2 * D_MODEL + n0:2 * D_MODEL + n0 + HYB_NCH]).T
        for p0 in range(0, HYB_NCH, hd):
            vt_ref[0, (n0 + p0) // hd, 0:hd, :] = y_t[p0:p0 + hd, :].astype(BF16)
    vt_ref[0, :, hd:DIFF_VROWS, :] = jnp.ones((DIFF_HEADS, DIFF_VROWS - hd, TM), BF16)


def _diff_proj(h, mod, w, cos_t, sin_t, nxt):
    nb, t, _ = h.shape
    nt = t // TM
    return pl.pallas_call(
        _diff_proj_kernel,
        grid=(nb, nt),
        in_specs=[pl.BlockSpec((1, TM, D_MODEL), lambda b, i: (b, i, 0)),
                  _mod_spec(nxt, nb),
                  _resident(w.shape),
                  pl.BlockSpec((TM, LANES), lambda b, i: (i, 0)),
                  pl.BlockSpec((TM, LANES), lambda b, i: (i, 0)),
                  pl.BlockSpec((LANES, TM), lambda b, i: (0, i)),
                  pl.BlockSpec((LANES, TM), lambda b, i: (0, i))],
        out_specs=[pl.BlockSpec((1, D_MODEL, TM), lambda b, i: (b, 0, i)),
                   pl.BlockSpec((1, TM, D_MODEL), lambda b, i: (b, i, 0)),
                   pl.BlockSpec((1, DIFF_HEADS, DIFF_VROWS, TM), lambda b, i: (b, 0, 0, i))],
        out_shape=[jax.ShapeDtypeStruct((nb, D_MODEL, t), BF16),
                   jax.ShapeDtypeStruct((nb, t, D_MODEL), BF16),
                   jax.ShapeDtypeStruct((nb, DIFF_HEADS, DIFF_VROWS, t), BF16)],
        name="diff_proj",
        compiler_params=_params(("parallel", "parallel")),
    )(h, mod, w, cos_t, sin_t, cos_t.T, sin_t.T)


def _diff_attn_kernel(qt_ref, k_ref, vt_ref, lam_ref, sw_ref, o_ref, kmax_ref, *, n_lat, n_ctx, nxt, lam_init):
    i = pl.program_id(2)
    lp = lam_ref[...]
    lam = (jnp.exp(jnp.sum(lp[0:1, :] * lp[1:2, :], axis=1, keepdims=True))
           - jnp.exp(jnp.sum(lp[2:3, :] * lp[3:4, :], axis=1, keepdims=True)) + lam_init)
    hd = 2 * DIFF_SUB
    qt = qt_ref[0]
    row = lax.broadcasted_iota(jnp.int32, qt.shape, 0)
    sub0 = (row % DIFF_SUB) < (DIFF_SUB // 2)
    zero = jnp.zeros_like(qt)

    qms = (jnp.where(sub0, qt, zero), jnp.where(sub0, zero, qt))

    @pl.when(i == 0)
    def _():
        kf = k_ref[0].astype(F32)
        kmax_ref[...] = jnp.sqrt(jnp.max(jnp.sum(kf * kf, axis=1, keepdims=True), axis=0, keepdims=True))

    bounds = [jnp.sqrt(jnp.sum(jnp.square(qm.astype(F32)), axis=0, keepdims=True)) * kmax_ref[...] for qm in qms]

    def finish(acc):
        outs = [a[:hd, :] * (1.0 / a[hd:hd + 1, :]) for a in acc]
        o = outs[0] - lam * outs[1]
        o = o * lax.rsqrt(jnp.mean(o * o, axis=0, keepdims=True) + RMS_EPS) * sw_ref[...] * (1.0 - lam_init)
        o_ref[0] = o.T.astype(BF16)

    def run(k0, nk):
        chunks = [(k0 + c0, min(DIFF_KCH, nk - c0)) for c0 in range(0, nk, DIFF_KCH)]

        def scores(ci):
            c0, n = chunks[ci]
            return [_dot(k_ref[0, c0:c0 + n, :], qm) for qm in qms]

        def attend(shifted):
            s_next = scores(0)
            mx = [None, None]
            acc = [None, None]
            for ci, (c0, n) in enumerate(chunks):
                s_cur = s_next
                if ci + 1 < len(chunks):
                    s_next = scores(ci + 1)
                vv = vt_ref[0, 0, :, c0:c0 + n]
                for sub in range(2):
                    if shifted:
                        pv = _dot(vv, jnp.exp2(s_cur[sub] - bounds[sub]).astype(BF16))
                        acc[sub] = pv if ci == 0 else acc[sub] + pv
                    else:
                        cm = jnp.max(s_cur[sub], axis=0, keepdims=True)
                        m_new = cm if ci == 0 else jnp.maximum(mx[sub], cm)
                        pv = _dot(vv, jnp.exp2(s_cur[sub] - m_new).astype(BF16))
                        acc[sub] = pv if ci == 0 else acc[sub] * jnp.exp2(mx[sub] - m_new) + pv
                        mx[sub] = m_new
            return acc

        acc = attend(True)
        sums = jnp.concatenate([a[hd:hd + 1, :] for a in acc], axis=0)
        ok = jnp.logical_and(jnp.min(sums) > SUM_LO, jnp.max(sums) < SUM_HI)
        finish(acc)

        @pl.when(jnp.logical_not(ok))
        def _():
            finish(attend(False))

    @pl.when(i < nxt)
    def _():
        run(0, n_lat + n_ctx)

    @pl.when(i >= nxt)
    def _():
        run(n_lat, n_ctx)


def _diff_attn(qt, k, vt, lam_p, subln_b, n_lat, lam_init, n_q_tiles, tq):
    nb, t, _ = k.shape
    n_ctx = t - n_lat
    nxt = n_lat // tq
    hd = 2 * DIFF_SUB
    return pl.pallas_call(
        functools.partial(_diff_attn_kernel, n_lat=n_lat, n_ctx=n_ctx, nxt=nxt, lam_init=lam_init),
        grid=(nb, DIFF_HEADS, n_q_tiles),
        in_specs=[pl.BlockSpec((1, hd, tq), lambda b, hh, i: (b, hh, i)),
                  pl.BlockSpec((1, t, hd), lambda b, hh, i: (b, 0, hh)),
                  pl.BlockSpec((1, 1, DIFF_VROWS, t), lambda b, hh, i: (b, hh, 0, 0)),
                  _resident(lam_p.shape), _resident(subln_b.shape)],
        out_specs=pl.BlockSpec((1, tq, hd), lambda b, hh, i: (b, i, hh)),
        out_shape=jax.ShapeDtypeStruct((nb, n_q_tiles * tq, D_MODEL), BF16),
        scratch_shapes=[pltpu.VMEM((1, 1), F32)],
        name="diff_attn",
        compiler_params=_params(("parallel", "parallel", "arbitrary")),
    )(qt, k, vt, lam_p, subln_b)


def _diff_out_kernel(a_ref, h_ref, mod_ref, w_ref, g_ref, b_ref, o_ref):
    o = _dot(a_ref[0], w_ref[...])
    gate = mod_ref[0][2:3, :]
    o_ref[0] = _layer_norm(ALPHA * h_ref[0] + gate * o, g_ref[...], b_ref[...])


def _diff_out(att, h, mod, w, ln_g, ln_b, nxt, n_tiles):
    nb, t, _ = h.shape
    return pl.pallas_call(
        _diff_out_kernel,
        grid=(nb, n_tiles),
        in_specs=[pl.BlockSpec((1, TM, D_MODEL), lambda b, i: (b, i, 0)),
                  pl.BlockSpec((1, TM, D_MODEL), lambda b, i: (b, i, 0)),
                  _mod_spec(nxt, nb),
                  _resident(w.shape), _resident(ln_g.shape), _resident(ln_b.shape)],
        out_specs=pl.BlockSpec((1, TM, D_MODEL), lambda b, i: (b, i, 0)),
        out_shape=jax.ShapeDtypeStruct((nb, n_tiles * TM, D_MODEL), F32),
        name="diff_out",
        compiler_params=_params(("parallel", "parallel")),
    )(att, h, mod, w, ln_g, ln_b)


def _rope_tables(n_lat, n_ctx):
    tok = np.arange(n_lat)
    row = (tok // GRID_W).astype(np.float32)
    col = (tok % GRID_W).astype(np.float32)
    n_freq = DIFF_SUB // 4
    inv = jnp.asarray(ROPE_BASE, F32) ** (-jnp.arange(n_freq, dtype=F32) / n_freq)
    ang = jnp.concatenate([jnp.asarray(row)[:, None] * inv, jnp.asarray(col)[:, None] * inv], axis=-1)
    cos = jnp.cos(ang)
    sin = jnp.sin(ang)
    cos_t = jnp.concatenate([cos, cos, cos, cos], axis=-1)
    sin_t = jnp.concatenate([-sin, -sin, sin, sin], axis=-1)
    cos_t = jnp.concatenate([cos_t, jnp.ones((n_ctx, LANES), F32)], axis=0)
    sin_t = jnp.concatenate([sin_t, jnp.zeros((n_ctx, LANES), F32)], axis=0)
    return cos_t, sin_t


def _diff_head_perm():
    half = DIFF_SUB // 2
    perm = []
    for hh in range(DIFF_HEADS):
        base = hh * 2 * DIFF_SUB
        for part in range(2):
            for sub in range(2):
                start = base + sub * DIFF_SUB + part * half
                perm.extend(range(start, start + half))
    return np.asarray(perm)


def kernel(x, c, ctx, c_ctx, ada_w, ada_b, ln1_g, ln1_b, ln2_g, ln2_b, ffn_w_up, ffn_conv_w, ffn_conv_b, ffn_w_down, hyb_w_in, ssd_conv_w, ssd_conv_b, ssd_a_log, ssd_dt_bias, ssd_d, ssd_norm_w, na_rpb, hyb_w_out, diff_w_in, diff_lambda, diff_subln_w, diff_w_out):
    nb, n_lat, d = x.shape
    n_ctx = ctx.shape[1]
    assert d == D_MODEL and n_lat % TM == 0 and n_ctx % TM == 0 and n_lat % (NA_WIN_ROWS * GRID_W) == 0
    t = n_lat + n_ctx
    nt = t // TM
    nxt = n_lat // TM

    h = jnp.concatenate([x, ctx], axis=1)
    cond_rows = -(-(nb + 1) // 8) * 8
    cond = jnp.concatenate([c, c_ctx[None, :], jnp.zeros((cond_rows - nb - 1, d), F32)], axis=0)
    mod_all = _ada_mod(cond, ada_w, ada_b).reshape(DEPTH, cond_rows, 6, d)
    cos_t, sin_t = _rope_tables(n_lat, n_ctx)
    perm = _diff_head_perm()
    na_variants, na_vid = _na_variants(n_lat // GRID_W)
    expand = jnp.asarray(np.kron(np.eye(LANES, SSD_H, dtype=np.float32), np.ones((1, SSD_P), np.float32)))

    for i in range(DEPTH):
        last = i == DEPTH - 1
        j = i // 2
        mod = mod_all[i]
        n_tiles = nxt if last else nt
        row = lambda v: v.reshape(1, -1)
        if i % 2 == 0:
            w_in = hyb_w_in[j]
            o_xbc = SSD_INNER
            o_dt = o_xbc + SSD_CONV_DIM
            o_q = o_dt + 2 * SSD_H
            w_main = jnp.concatenate([w_in[:, o_xbc:o_dt], w_in[:, :SSD_INNER],
                                      w_in[:, o_q:o_q + NA_WIDTH] * NA_DH ** -0.5,
                                      w_in[:, o_q + NA_WIDTH:]], axis=1).astype(BF16)
            w_dt = jnp.zeros((d, 2 * LANES), F32)
            w_dt = w_dt.at[:, :SSD_H].set(w_in[:, o_dt:o_dt + SSD_H])
            w_dt = w_dt.at[:, LANES:LANES + SSD_H].set(w_in[:, o_dt + SSD_H:o_q]).astype(BF16)
            pm, dt, na_qt, na_k, na_vt = _hyb_proj(h, mod, w_main, w_dt, ssd_conv_w[j], row(ssd_conv_b[j]), nxt)

            pad = lambda v: jnp.zeros((2, 1, LANES), F32).at[:, 0, :SSD_H].set(v)
            arow = pad(-jnp.exp(ssd_a_log[j].astype(F32)))
            brow = pad(ssd_dt_bias[j].astype(F32))
            dsk = jnp.repeat(ssd_d[j].astype(F32), SSD_P, axis=-1).reshape(2, 1, SSD_INNER)
            yd = _ssd(pm, dt, arow, brow, dsk, expand, n_lat)
            att = _na(na_qt, na_k, na_vt, _na_bias_table(na_rpb[j], na_variants), na_vid, n_lat)
            w_out = hyb_w_out[j].astype(BF16)
            h = _hyb_out(yd, pm, att, h, mod, row(ssd_norm_w[j]), w_out[:SSD_INNER], w_out[SSD_INNER:],
                         row(ln1_g[i]), row(ln1_b[i]), nxt)
        else:
            lam_init = 0.8 - 0.6 * math.exp(-0.3 * i)
            w_in = diff_w_in[j]
            wq = w_in[:, :d][:, perm] * (DIFF_SUB ** -0.5 * LOG2E)
            wk = w_in[:, d:2 * d][:, perm]
            w_qkv = jnp.concatenate([wq, wk, w_in[:, 2 * d:]], axis=1).astype(BF16)
            d_qt, d_k, d_vt = _diff_proj(h, mod, w_qkv, cos_t, sin_t, nxt)
            subln_b = jnp.broadcast_to(diff_subln_w[j].astype(F32)[:, None], (2 * DIFF_SUB, TM))
            att = _diff_attn(d_qt, d_k, d_vt, diff_lambda[j].astype(F32), subln_b, n_lat, lam_init, n_tiles, TM)
            h = _diff_out(att, h, mod, diff_w_out[j].astype(BF16), row(ln1_g[i]), row(ln1_b[i]), nxt, n_tiles)
        h = _ffn(h, mod, ffn_w_up[i].astype(BF16), ffn_conv_w[i], row(ffn_conv_b[i]),
                 ffn_w_down[i].astype(BF16), row(ln2_g[i]), row(ln2_b[i]), nxt, n_tiles)
    return h
```

```python
import functools
import math

import numpy as np
import jax
import jax.numpy as jnp
from jax import lax
from jax.experimental import pallas as pl
from jax.experimental.pallas import tpu as pltpu

F32 = jnp.float32
BF16 = jnp.bfloat16

D_MODEL = 1024
DEPTH = 4
GRID_W = 64
SSD_P = 64
SSD_H = 16
SSD_G = 4
SSD_R = SSD_H // SSD_G
SSD_N = 128
SSD_INNER = SSD_H * SSD_P
SSD_GN = SSD_G * SSD_N
SSD_CONV_W = 5
SSD_CONV_DIM = SSD_INNER + 2 * SSD_GN
SSD_CHUNK = 128
NA_HEADS = 8
NA_DH = 64
NA_WIDTH = NA_HEADS * NA_DH
NA_WIN_ROWS = 8
NA_WIN_COLS = 16
DIFF_HEADS = 8
DIFF_SUB = 64
ROPE_BASE = 10000.0
D_FF = 2816
FFN_CH = 256
FFN_BPS = 2
ALPHA = (2.0 * DEPTH) ** 0.25
LN_EPS = 1e-5
RMS_EPS = 1e-5

LANES = 128
TM = 256
HALO_F32 = 8
HALO_BF16 = 16
VMEM_LIMIT = 56 * 1024 * 1024
HYB_SSD = SSD_CONV_DIM + SSD_INNER
HYB_MAIN = HYB_SSD + 3 * NA_WIDTH
NA_VROWS = NA_DH + HALO_BF16
NA_BLK_ROWS = 4
NA_KEY_BLKS = 3
DIFF_VROWS = 2 * DIFF_SUB + HALO_BF16
DIFF_KCH = 512
DIFF_HPS = 2
LOG2E = 1.4426950408889634
SUM_LO = 2.0 ** -90
SUM_HI = 2.0 ** 100
HYB_NCH = 512


def _params(sem, vmem=VMEM_LIMIT):
    return pltpu.CompilerParams(dimension_semantics=sem, vmem_limit_bytes=vmem)


def _resident(shape):
    nd = len(shape)
    return pl.BlockSpec(shape, lambda *_: (0,) * nd, pipeline_mode=pl.Buffered(1))


def _dot(a, b):
    return jnp.dot(a, b, preferred_element_type=F32)


def _dot_nt(a, b):
    return lax.dot_general(a, b, (((1,), (1,)), ((), ())), preferred_element_type=F32)


def _silu(v):
    return v * jax.nn.sigmoid(v)


def _layer_norm(r, g, b):
    mu = jnp.mean(r, axis=-1, keepdims=True)
    xc = r - mu
    var = jnp.mean(xc * xc, axis=-1, keepdims=True)
    return xc * lax.rsqrt(var + LN_EPS) * g + b


def _mod_spec(nxt, nb):
    return pl.BlockSpec((1, 6, D_MODEL), lambda b, i: (jnp.where(i >= nxt, nb, b), 0, 0))


def _ada_kernel(c_ref, w_ref, b_ref, o_ref):
    s = _silu(c_ref[...]).astype(BF16)
    o_ref[0] = _dot(s, w_ref[0].astype(BF16)) + b_ref[0]


def _ada_mod(cond, ada_w, ada_b):
    rows = cond.shape[0]
    n = ada_w.shape[-1]
    tn = n // 4
    return pl.pallas_call(
        _ada_kernel,
        grid=(DEPTH, n // tn),
        in_specs=[pl.BlockSpec((rows, D_MODEL), lambda l, j: (0, 0)),
                  pl.BlockSpec((1, D_MODEL, tn), lambda l, j: (l, 0, j)),
                  pl.BlockSpec((1, 1, tn), lambda l, j: (l, 0, j))],
        out_specs=pl.BlockSpec((1, rows, tn), lambda l, j: (l, 0, j)),
        out_shape=jax.ShapeDtypeStruct((DEPTH, rows, n), F32),
        name="ada_mod",
        compiler_params=_params(("parallel", "parallel")),
    )(cond, ada_w, ada_b.reshape(DEPTH, 1, n))


def _hyb_proj_kernel(h_ref, hp_ref, hn_ref, mod_ref, w_ref, wdt_ref, cw_ref, cb_ref,
                     o_ref, dt_ref, qt_ref, k_ref, vt_ref, ext_ref, *, nxt, nt):
    i = pl.program_id(1)
    m = mod_ref[0]
    shift, scale = m[0:1, :], m[1:2, :]
    seg_start = jnp.logical_or(i == 0, i == nxt)
    seg_end = jnp.logical_or(i == nxt - 1, i == nt - 1)
    um = h_ref[0] * (1.0 + scale) + shift
    u = um.astype(BF16)
    up = jnp.where(seg_start, 0.0, hp_ref[0] * (1.0 + scale) + shift)
    un = jnp.where(seg_end, 0.0, hn_ref[0] * (1.0 + scale) + shift)
    uext = jnp.concatenate([up, um, un], axis=0).astype(BF16)

    def xbc_proj(c):
        ext_ref[c % 2] = _dot(uext, w_ref[:, c * HYB_NCH:(c + 1) * HYB_NCH])

    n_ch = SSD_CONV_DIM // HYB_NCH
    xbc_proj(0)
    for c in range(n_ch):
        n0 = c * HYB_NCH
        if c + 1 < n_ch:
            xbc_proj(c + 1)
        acc = cb_ref[:, n0:n0 + HYB_NCH]
        for k in range(SSD_CONV_W):
            acc = acc + cw_ref[k:k + 1, n0:n0 + HYB_NCH] * ext_ref[c % 2, pl.ds(HALO_F32 - SSD_CONV_W // 2 + k, TM), :]
        o_ref[0, :, n0:n0 + HYB_NCH] = _silu(acc).astype(BF16)
    for n0 in range(SSD_CONV_DIM, HYB_SSD, HYB_NCH):
        o_ref[0, :, n0:n0 + HYB_NCH] = _dot(u, w_ref[:, n0:n0 + HYB_NCH]).astype(BF16)
    dt_ref[0] = _dot(u, wdt_ref[...])
    qt_ref[0] = _dot(u, w_ref[:, HYB_SSD:HYB_SSD + NA_WIDTH]).T.astype(BF16)
    k_ref[0] = _dot(u, w_ref[:, HYB_SSD + NA_WIDTH:HYB_SSD + 2 * NA_WIDTH]).astype(BF16)
    v_t = _dot(u, w_ref[:, HYB_SSD + 2 * NA_WIDTH:HYB_MAIN]).T
    for hh in range(NA_HEADS):
        vt_ref[0, hh, 0:NA_DH, :] = v_t[hh * NA_DH:(hh + 1) * NA_DH, :].astype(BF16)
        vt_ref[0, hh, NA_DH:NA_VROWS, :] = jnp.ones((NA_VROWS - NA_DH, TM), BF16)


def _hyb_proj(h, mod, w_main, w_dt, conv_w, conv_b, nxt):
    nb, t, _ = h.shape
    nt = t // TM
    hb = TM // HALO_F32
    last_hb = t // HALO_F32 - 1
    return pl.pallas_call(
        functools.partial(_hyb_proj_kernel, nxt=nxt, nt=nt),
        grid=(nb, nt),
        in_specs=[pl.BlockSpec((1, TM, D_MODEL), lambda b, i: (b, i, 0)),
                  pl.BlockSpec((1, HALO_F32, D_MODEL), lambda b, i: (b, jnp.maximum(i * hb - 1, 0), 0)),
                  pl.BlockSpec((1, HALO_F32, D_MODEL), lambda b, i: (b, jnp.minimum((i + 1) * hb, last_hb), 0)),
                  _mod_spec(nxt, nb),
                  _resident(w_main.shape),
                  _resident(w_dt.shape),
                  _resident(conv_w.shape),
                  _resident(conv_b.shape)],
        out_specs=[pl.BlockSpec((1, TM, HYB_SSD), lambda b, i: (b, i, 0)),
                   pl.BlockSpec((1, TM, 2 * LANES), lambda b, i: (b, i, 0)),
                   pl.BlockSpec((1, NA_WIDTH, TM), lambda b, i: (b, 0, i)),
                   pl.BlockSpec((1, TM, NA_WIDTH), lambda b, i: (b, i, 0)),
                   pl.BlockSpec((1, NA_HEADS, NA_VROWS, TM), lambda b, i: (b, 0, 0, i))],
        out_shape=[jax.ShapeDtypeStruct((nb, t, HYB_SSD), BF16),
                   jax.ShapeDtypeStruct((nb, t, 2 * LANES), F32),
                   jax.ShapeDtypeStruct((nb, NA_WIDTH, t), BF16),
                   jax.ShapeDtypeStruct((nb, t, NA_WIDTH), BF16),
                   jax.ShapeDtypeStruct((nb, NA_HEADS, NA_VROWS, t), BF16)],
        scratch_shapes=[pltpu.VMEM((2, TM + 2 * HALO_F32, HYB_NCH), F32)],
        name="hyb_proj",
        compiler_params=_params(("parallel", "parallel")),
    )(h, h, h, mod, w_main, w_dt, conv_w, conv_b)


def _ssd_kernel(uf_ref, ub_ref, dtf_ref, dtb_ref, arow_ref, brow_ref, dsk_ref, e_ref, yf_ref, yb_ref, st_ref):
    j = pl.program_id(1)

    @pl.when(j == 0)
    def _():
        st_ref[...] = jnp.zeros_like(st_ref)

    _ssd_chunk(0, uf_ref, dtf_ref, arow_ref, brow_ref, dsk_ref, e_ref, yf_ref, st_ref)
    _ssd_chunk(1, ub_ref, dtb_ref, arow_ref, brow_ref, dsk_ref, e_ref, yb_ref, st_ref)


def _ssd_chunk(d, u_ref, dt_ref, arow_ref, brow_ref, dsk_ref, e_ref, y_ref, st_ref):
    q = SSD_CHUNK
    u = u_ref[0].astype(F32)
    xs = u[:, :SSD_INNER]

    dtr = dt_ref[0] + brow_ref[d]
    dtv = jnp.maximum(dtr, 0.0) + jnp.log1p(jnp.exp(-jnp.abs(dtr)))
    adt = dtv * arow_ref[d]
    ri = lax.broadcasted_iota(jnp.int32, (q, q), 0)
    ci = lax.broadcasted_iota(jnp.int32, (q, q), 1)
    tri = (ri >= ci) if d == 0 else (ri <= ci)
    cs = jnp.dot(tri.astype(F32), adt, precision=lax.Precision.HIGHEST, preferred_element_type=F32)
    cs_t = cs.T
    dt_t = dtv.T
    last = q - 1 if d == 0 else 0
    tot = cs[last:last + 1, :]
    tot_t = cs_t[:, last:last + 1]
    w_t = jnp.exp(tot_t - cs_t) * dt_t
    dec_row = jnp.dot(jnp.broadcast_to(jnp.exp(tot), (8, LANES)), e_ref[...],
                      precision=lax.Precision.HIGHEST, preferred_element_type=F32)[0:1]

    lane = lax.broadcasted_iota(jnp.int32, (q, LANES), 1)
    lo = lane < SSD_P
    dsk = dsk_ref[d]
    for g in range(SSD_G):
        bm = u[:, SSD_INNER + g * SSD_N:SSD_INNER + (g + 1) * SSD_N]
        cm = u[:, SSD_INNER + SSD_GN + g * SSD_N:SSD_INNER + SSD_GN + (g + 1) * SSD_N]
        cmb = cm.astype(BF16)
        cb = _dot_nt(cmb, bm.astype(BF16))
        bm_t = bm.T
        s_prev = st_ref[d, g]
        y_off = _dot(cmb, s_prev.astype(BF16))
        s_parts = []
        for pr in range(SSD_R // 2):
            col0 = g * SSD_R * SSD_P + pr * LANES
            xs_pair = xs[:, col0:col0 + LANES]
            xsb = xs_pair.astype(BF16)
            yd, sc, colbs = [], [], []
            for sub in range(2):
                h = g * SSD_R + pr * 2 + sub
                colb = jnp.broadcast_to(cs[:, h:h + 1], (q, q))
                decay = jnp.exp(jnp.where(tri, colb - cs_t[h:h + 1, :], -jnp.inf))
                gmat = (cb * decay * dt_t[h:h + 1, :]).astype(BF16)
                yd.append(_dot(gmat, xsb))
                sc.append(_dot((bm_t * w_t[h:h + 1, :]).astype(BF16), xsb))
                colbs.append(colb)
            y_diag = jnp.where(lo, yd[0], yd[1])
            e_col = jnp.exp(jnp.where(lo, colbs[0], colbs[1]))
            y_pair = y_diag + y_off[:, pr * LANES:(pr + 1) * LANES] * e_col + dsk[:, col0:col0 + LANES] * xs_pair
            y_ref[0, :, col0:col0 + LANES] = y_pair.astype(BF16)
            s_parts.append(jnp.where(lo, sc[0], sc[1]))
        g0 = g * SSD_R * SSD_P
        st_ref[d, g] = s_prev * dec_row[:, g0:g0 + SSD_R * SSD_P] + jnp.concatenate(s_parts, axis=1)


def _ssd(pm, dt, arow, brow, dsk, expand, n_lat):
    nb, t, _ = pm.shape
    q = SSD_CHUNK
    nc = t // q
    nxc = n_lat // q

    def cf(j):
        return (j + nxc) % nc

    def cbk(j):
        return nc - 1 - j

    y_shape = jax.ShapeDtypeStruct((nb, t, SSD_INNER), BF16)
    return pl.pallas_call(
        _ssd_kernel,
        grid=(nb, nc),
        in_specs=[pl.BlockSpec((1, q, SSD_CONV_DIM), lambda b, j: (b, cf(j), 0)),
                  pl.BlockSpec((1, q, SSD_CONV_DIM), lambda b, j: (b, cbk(j), 0)),
                  pl.BlockSpec((1, q, LANES), lambda b, j: (b, cf(j), 0)),
                  pl.BlockSpec((1, q, LANES), lambda b, j: (b, cbk(j), 1)),
                  _resident(arow.shape), _resident(brow.shape), _resident(dsk.shape), _resident(expand.shape)],
        out_specs=[pl.BlockSpec((1, q, SSD_INNER), lambda b, j: (b, cf(j), 0)),
                   pl.BlockSpec((1, q, SSD_INNER), lambda b, j: (b, cbk(j), 0))],
        out_shape=[y_shape, y_shape],
        scratch_shapes=[pltpu.VMEM((2, SSD_G, SSD_N, SSD_R * SSD_P), F32)],
        name="ssd_scan",
        compiler_params=_params(("parallel", "arbitrary")),
    )(pm, pm, dt, dt, arow, brow, dsk, expand)


def _na_kernel(qt_ref, k0_ref, k1_ref, k2_ref, kc_ref, v0_ref, v1_ref, v2_ref, vc_ref, bias_ref, o_ref, *, nblk):
    i = pl.program_id(1)
    nq = NA_BLK_ROWS * GRID_W
    zeros = jnp.zeros((NA_DH, nq), BF16)
    local = ((k0_ref, v0_ref), (k1_ref, v1_ref), (k2_ref, v2_ref))

    def run(with_local):
        def scores(hh):
            sl = slice((hh // 2) * LANES, (hh // 2 + 1) * LANES)
            qh = qt_ref[0, hh * NA_DH:(hh + 1) * NA_DH, :]
            qm = jnp.concatenate([qh, zeros] if hh % 2 == 0 else [zeros, qh], axis=0)
            sc = []
            if with_local:
                for jb, (kr, _) in enumerate(local):
                    sc.append(_dot(kr[0, :, sl], qm) + bias_ref[0, hh, jb * nq:(jb + 1) * nq, :])
            sc.append(_dot(kc_ref[0, :, sl], qm))
            return sc

        outs = []
        s_next = scores(0)
        for hh in range(NA_HEADS):
            s_cur = s_next
            if hh + 1 < NA_HEADS:
                s_next = scores(hh + 1)
            vals = ([vr[0, hh] for _, vr in local] if with_local else []) + [vc_ref[0, hh]]
            mx = functools.reduce(jnp.maximum, [jnp.max(s, axis=0, keepdims=True) for s in s_cur])
            o_t = functools.reduce(
                lambda a, b: a + b, [_dot(v, jnp.exp(s - mx).astype(BF16)) for s, v in zip(s_cur, vals)])
            outs.append(o_t[:NA_DH, :] * (1.0 / o_t[NA_DH:NA_DH + 1, :]))
        o_ref[0] = jnp.concatenate(outs, axis=0).T.astype(BF16)

    @pl.when(i < nblk)
    def _():
        run(True)

    @pl.when(i >= nblk)
    def _():
        run(False)


def _na_variants(rows):
    nblk = rows // NA_BLK_ROWS
    assert rows % NA_BLK_ROWS == 0 and nblk >= NA_KEY_BLKS
    keys, vid = [], []
    for b in range(nblk):
        ws = int(np.clip(b - 1, 0, nblk - NA_KEY_BLKS)) * NA_BLK_ROWS
        key = []
        for a in range(NA_BLK_ROWS):
            r = b * NA_BLK_ROWS + a
            r0 = int(np.clip(r - NA_WIN_ROWS // 2, 0, rows - NA_WIN_ROWS))
            assert 0 <= r0 - ws and r0 - ws + NA_WIN_ROWS <= NA_KEY_BLKS * NA_BLK_ROWS
            key.append((r0 - ws, r0 - r))
        key = tuple(key)
        if key not in keys:
            keys.append(key)
        vid.append(keys.index(key))
    return keys, vid


def _na_bias_table(rpb, variants):
    c = np.arange(GRID_W)[:, None]
    kc = np.arange(GRID_W)[None, :]
    c0 = np.clip(c - NA_WIN_COLS // 2, 0, GRID_W - NA_WIN_COLS)
    valid_c = (kc >= c0) & (kc < c0 + NA_WIN_COLS)
    dc = kc - c + NA_WIN_COLS - 1
    onehot = ((dc[None] == np.arange(2 * NA_WIN_COLS - 1)[:, None, None]) & valid_c[None]).astype(np.float32)
    tz = jnp.einsum('hrd,dck->hrck', rpb.astype(F32), jnp.asarray(onehot), precision=lax.Precision.HIGHEST)
    nkr = NA_KEY_BLKS * NA_BLK_ROWS
    ndr = 2 * NA_WIN_ROWS - 1
    tabs = []
    for key in variants:
        sel = np.zeros((NA_BLK_ROWS, nkr, ndr), np.float32)
        valid_r = np.zeros((NA_BLK_ROWS, nkr), bool)
        for a, (off, e) in enumerate(key):
            for w in range(NA_WIN_ROWS):
                sel[a, off + w, e + NA_WIN_ROWS - 1 + w] = 1.0
                valid_r[a, off + w] = True
        tab = jnp.einsum('ajr,hrck->hajck', jnp.asarray(sel), tz, precision=lax.Precision.HIGHEST)
        ok = jnp.asarray(valid_r[None, :, :, None, None] & valid_c[None, None, None, :, :])
        tab = jnp.where(ok, tab, -jnp.inf)
        tabs.append(tab.transpose(0, 2, 4, 1, 3).reshape(NA_HEADS, nkr * GRID_W, NA_BLK_ROWS * GRID_W))
    return jnp.stack(tabs)


def _na(qt, k, vt, bias, vid, n_lat):
    nb, t, _ = k.shape
    n_ctx = t - n_lat
    nq = NA_BLK_ROWS * GRID_W
    nblk = n_lat // nq
    assert n_ctx % nq == 0 and n_lat % n_ctx == 0
    steps = t // nq
    cblk = n_lat // n_ctx

    def wb(i):
        return jnp.clip(i - 1, 0, nblk - NA_KEY_BLKS)

    def variant(i):
        v = jnp.int32(0)
        for blk in range(1, nblk):
            v = jnp.where(i == blk, vid[blk], v)
        return v

    kspec = [pl.BlockSpec((1, nq, NA_WIDTH), functools.partial(lambda b, i, j: (b, wb(i) + j, 0), j=j))
             for j in range(NA_KEY_BLKS)]
    vspec = [pl.BlockSpec((1, NA_HEADS, NA_VROWS, nq), functools.partial(lambda b, i, j: (b, 0, 0, wb(i) + j), j=j))
             for j in range(NA_KEY_BLKS)]
    return pl.pallas_call(
        functools.partial(_na_kernel, nblk=nblk),
        grid=(nb, steps),
        in_specs=([pl.BlockSpec((1, NA_WIDTH, nq), lambda b, i: (b, 0, i))] + kspec
                  + [pl.BlockSpec((1, n_ctx, NA_WIDTH), lambda b, i: (b, cblk, 0))] + vspec
                  + [pl.BlockSpec((1, NA_HEADS, NA_VROWS, n_ctx), lambda b, i: (b, 0, 0, cblk)),
                     pl.BlockSpec((1,) + bias.shape[1:], lambda b, i: (variant(i), 0, 0, 0))]),
        out_specs=pl.BlockSpec((1, nq, NA_WIDTH), lambda b, i: (b, i, 0)),
        out_shape=jax.ShapeDtypeStruct((nb, t, NA_WIDTH), BF16),
        name="na_attn",
        compiler_params=_params(("parallel", "arbitrary")),
    )(qt, k, k, k, k, vt, vt, vt, vt, bias)


def _hyb_out_kernel(yf_ref, yb_ref, z_ref, a_ref, h_ref, mod_ref, nw_ref, wy_ref, wa_ref, g_ref, b_ref, o_ref):
    y = (yf_ref[0].astype(F32) + yb_ref[0].astype(F32)) * _silu(z_ref[0].astype(F32))
    y = y * lax.rsqrt(jnp.mean(y * y, axis=-1, keepdims=True) + RMS_EPS) * nw_ref[...]
    o = _dot(y.astype(BF16), wy_ref[...]) + _dot(a_ref[0], wa_ref[...])
    gate = mod_ref[0][2:3, :]
    o_ref[0] = _layer_norm(ALPHA * h_ref[0] + gate * o, g_ref[...], b_ref[...])


def _hyb_out(yf, yb, pm, att, h, mod, norm_w, wy, wa, ln_g, ln_b, nxt):
    nb, t, _ = h.shape
    nt = t // TM
    zcol = SSD_CONV_DIM // SSD_INNER
    return pl.pallas_call(
        _hyb_out_kernel,
        grid=(nb, nt),
        in_specs=[pl.BlockSpec((1, TM, SSD_INNER), lambda b, i: (b, i, 0)),
                  pl.BlockSpec((1, TM, SSD_INNER), lambda b, i: (b, i, 0)),
                  pl.BlockSpec((1, TM, SSD_INNER), lambda b, i: (b, i, zcol)),
                  pl.BlockSpec((1, TM, NA_WIDTH), lambda b, i: (b, i, 0)),
                  pl.BlockSpec((1, TM, D_MODEL), lambda b, i: (b, i, 0)),
                  _mod_spec(nxt, nb),
                  _resident(norm_w.shape), _resident(wy.shape), _resident(wa.shape),
                  _resident(ln_g.shape), _resident(ln_b.shape)],
        out_specs=pl.BlockSpec((1, TM, D_MODEL), lambda b, i: (b, i, 0)),
        out_shape=jax.ShapeDtypeStruct((nb, t, D_MODEL), F32),
        name="hyb_out",
        compiler_params=_params(("parallel", "parallel")),
    )(yf, yb, pm, att, h, mod, norm_w, wy, wa, ln_g, ln_b)


def _ffn_kernel(hm_ref, hp_ref, hn_ref, mod_ref, wup_ref, cw_ref, cb_ref, wdn_ref, g_ref, b_ref, o_ref,
                *scratch, nxt, nt):
    per_tile = len(scratch) // FFN_BPS
    hv_ref = [scratch[bi * per_tile:bi * per_tile + 2] for bi in range(FFN_BPS)]
    hg_ref = [scratch[bi * per_tile + 2:bi * per_tile + 4] for bi in range(FFN_BPS)]
    acc_ref = [scratch[bi * per_tile + 4] for bi in range(FFN_BPS)]
    perm_ref = [scratch[bi * per_tile + 5] for bi in range(FFN_BPS)]
    i = pl.program_id(1)
    seg_start = jnp.logical_or(i == 0, i == nxt)
    seg_end = jnp.logical_or(i == nxt - 1, i == nt - 1)
    sl = HALO_F32
    ngrp = TM // sl
    ncb = D_MODEL // LANES
    sub_d = lax.broadcasted_iota(jnp.int32, (2 * sl, D_MODEL), 0)
    sub_c = lax.broadcasted_iota(jnp.int32, (sl, FFN_CH), 0)

    def load_tile(bi):
        m = mod_ref[bi]
        shift, scale = m[3:4, :], m[4:5, :]
        for cb in range(ncb):
            perm_ref[bi][cb] = hm_ref[bi, :, cb * LANES:(cb + 1) * LANES]
        hm = jnp.concatenate(
            [jnp.concatenate([perm_ref[bi][cb, pl.ds(a, sl, stride=ngrp), :] for a in range(ngrp)], axis=0)
             for cb in range(ncb)], axis=1)
        prev_row = jnp.where(seg_start, 0.0, hp_ref[bi, sl - 1:sl, :] * (1.0 + scale) + shift)
        next_row = jnp.where(seg_end, 0.0, hn_ref[bi, 0:1, :] * (1.0 + scale) + shift)
        halo = jnp.where(sub_d == 0, prev_row, jnp.where(sub_d == 1, next_row, 0.0))
        uext = jnp.concatenate([hm * (1.0 + scale) + shift, halo], axis=0).astype(BF16)
        return hm, uext

    def conv(ref, col0):
        w = cw_ref[:, col0:col0 + FFN_CH]
        before_first = jnp.where(sub_c == 0, ref[TM:TM + 1, :], pltpu.roll(ref[TM - sl:TM, :], 1, axis=0))
        after_last = jnp.where(sub_c == sl - 1, ref[TM + 1:TM + 2, :], pltpu.roll(ref[0:sl, :], sl - 1, axis=0))
        prev = jnp.concatenate([before_first, ref[0:TM - sl, :]], axis=0)
        nxt_ = jnp.concatenate([ref[sl:TM, :], after_last], axis=0)
        return w[0:1, :] * prev + w[1:2, :] * ref[0:TM, :] + w[2:3, :] * nxt_ + cb_ref[:, col0:col0 + FFN_CH]

    n_ch = D_FF // FFN_CH

    def up_proj(bi, uext, c):
        hv_ref[bi][c % 2][...] = _dot(uext, wup_ref[:, c * FFN_CH:(c + 1) * FFN_CH])
        hg_ref[bi][c % 2][...] = _dot(uext, wup_ref[:, D_FF + c * FFN_CH:D_FF + (c + 1) * FFN_CH])

    def chunk(bi, uext, c):
        v0 = c * FFN_CH
        if c + 1 < n_ch:
            up_proj(bi, uext, c + 1)
        act = (_silu(conv(hg_ref[bi][c % 2], D_FF + v0)) * conv(hv_ref[bi][c % 2], v0)).astype(BF16)
        contrib = _dot(act, wdn_ref[v0:v0 + FFN_CH, :])
        if c == 0:
            acc_ref[bi][...] = contrib
        else:
            acc_ref[bi][...] += contrib

    def finish(bi, hm):
        gate = mod_ref[bi][5:6, :]
        out = _layer_norm(ALPHA * hm + gate * acc_ref[bi][...], g_ref[...], b_ref[...])
        for cb in range(ncb):
            for a in range(ngrp):
                perm_ref[bi][cb, pl.ds(a, sl, stride=ngrp), :] = out[a * sl:(a + 1) * sl, cb * LANES:(cb + 1) * LANES]
            o_ref[bi, :, cb * LANES:(cb + 1) * LANES] = perm_ref[bi][cb]

    tiles = [load_tile(bi) for bi in range(FFN_BPS)]
    for bi, (hm, uext) in enumerate(tiles):
        up_proj(bi, uext, 0)
        for c in range(n_ch):
            chunk(bi, uext, c)
        finish(bi, hm)


def _ffn(h, mod, w_up, conv_w, conv_b, w_dn, ln_g, ln_b, nxt, n_out_tiles):
    nb, t, _ = h.shape
    nt = t // TM
    hb = TM // HALO_F32
    last_hb = t // HALO_F32 - 1
    bps = FFN_BPS
    assert nb % bps == 0
    mod_spec = pl.BlockSpec((bps, 6, D_MODEL), lambda b, i: (jnp.where(i >= nxt, nb // bps, b), 0, 0))
    return pl.pallas_call(
        functools.partial(_ffn_kernel, nxt=nxt, nt=nt),
        grid=(nb // bps, n_out_tiles),
        in_specs=[pl.BlockSpec((bps, TM, D_MODEL), lambda b, i: (b, i, 0)),
                  pl.BlockSpec((bps, HALO_F32, D_MODEL), lambda b, i: (b, jnp.maximum(i * hb - 1, 0), 0)),
                  pl.BlockSpec((bps, HALO_F32, D_MODEL), lambda b, i: (b, jnp.minimum((i + 1) * hb, last_hb), 0)),
                  mod_spec,
                  _resident(w_up.shape), _resident(conv_w.shape), _resident(conv_b.shape),
                  _resident(w_dn.shape), _resident(ln_g.shape), _resident(ln_b.shape)],
        out_specs=pl.BlockSpec((bps, TM, D_MODEL), lambda b, i: (b, i, 0)),
        out_shape=jax.ShapeDtypeStruct((nb, n_out_tiles * TM, D_MODEL), F32),
        scratch_shapes=bps * ([pltpu.VMEM((TM + 2 * HALO_F32, FFN_CH), F32)] * 4
                              + [pltpu.VMEM((TM, D_MODEL), F32),
                                 pltpu.VMEM((D_MODEL // LANES, TM, LANES), F32)]),
        name="conv_ffn",
        compiler_params=_params(("parallel", "parallel")),
    )(h, h, h, mod, w_up, conv_w, conv_b, w_dn, ln_g, ln_b)


def _diff_proj_kernel(h_ref, mod_ref, w_ref, cos_ref, sin_ref, cost_ref, sint_ref, qt_ref, k_ref, vt_ref):
    m = mod_ref[0]
    u = (h_ref[0] * (1.0 + m[1:2, :]) + m[0:1, :]).astype(BF16)
    half = LANES // 2
    for n0 in range(0, D_MODEL, HYB_NCH):
        y_t = _dot(u, w_ref[:, n0:n0 + HYB_NCH]).T
        for p0 in range(0, HYB_NCH, LANES):
            blk = y_t[p0:p0 + LANES, :]
            partner = jnp.concatenate([blk[half:, :], blk[:half, :]], axis=0)
            qt_ref[0, n0 + p0:n0 + p0 + LANES, :] = (blk * cost_ref[...] + partner * sint_ref[...]).astype(BF16)
    for n0 in range(0, D_MODEL, HYB_NCH):
        y = _dot(u, w_ref[:, D_MODEL + n0:D_MODEL + n0 + HYB_NCH])
        parts = []
        for p0 in range(0, HYB_NCH, LANES):
            yp = y[:, p0:p0 + LANES]
            parts.append(yp * cos_ref[...] + pltpu.roll(yp, half, axis=1) * sin_ref[...])
        k_ref[0, :, n0:n0 + HYB_NCH] = jnp.concatenate(parts, axis=1).astype(BF16)
    hd = 2 * DIFF_SUB
    for n0 in range(0, D_MODEL, HYB_NCH):
        y_t = _dot(u, w_ref[:, 2 * D_MODEL + n0:2 * D_MODEL + n0 + HYB_NCH]).T
        for p0 in range(0, HYB_NCH, hd):
            vt_ref[0, (n0 + p0) // hd, 0:hd, :] = y_t[p0:p0 + hd, :].astype(BF16)
    vt_ref[0, :, hd:DIFF_VROWS, :] = jnp.ones((DIFF_HEADS, DIFF_VROWS - hd, TM), BF16)


def _diff_proj(h, mod, w, cos_t, sin_t, nxt):
    nb, t, _ = h.shape
    nt = t // TM
    return pl.pallas_call(
        _diff_proj_kernel,
        grid=(nb, nt),
        in_specs=[pl.BlockSpec((1, TM, D_MODEL), lambda b, i: (b, i, 0)),
                  _mod_spec(nxt, nb),
                  _resident(w.shape),
                  pl.BlockSpec((TM, LANES), lambda b, i: (i, 0)),
                  pl.BlockSpec((TM, LANES), lambda b, i: (i, 0)),
                  pl.BlockSpec((LANES, TM), lambda b, i: (0, i)),
                  pl.BlockSpec((LANES, TM), lambda b, i: (0, i))],
        out_specs=[pl.BlockSpec((1, D_MODEL, TM), lambda b, i: (b, 0, i)),
                   pl.BlockSpec((1, TM, D_MODEL), lambda b, i: (b, i, 0)),
                   pl.BlockSpec((1, DIFF_HEADS, DIFF_VROWS, TM), lambda b, i: (b, 0, 0, i))],
        out_shape=[jax.ShapeDtypeStruct((nb, D_MODEL, t), BF16),
                   jax.ShapeDtypeStruct((nb, t, D_MODEL), BF16),
                   jax.ShapeDtypeStruct((nb, DIFF_HEADS, DIFF_VROWS, t), BF16)],
        name="diff_proj",
        compiler_params=_params(("parallel", "parallel")),
    )(h, mod, w, cos_t, sin_t, cos_t.T, sin_t.T)


def _diff_attn_kernel(qt_ref, k_ref, vt_ref, lam_ref, sw_ref, o_ref, kmax_ref, *, n_lat, n_ctx, nxt, lam_init):
    i = pl.program_id(2)
    lp = lam_ref[...]
    lam = (jnp.exp(jnp.sum(lp[0:1, :] * lp[1:2, :], axis=1, keepdims=True))
           - jnp.exp(jnp.sum(lp[2:3, :] * lp[3:4, :], axis=1, keepdims=True)) + lam_init)
    hd = 2 * DIFF_SUB
    tq = qt_ref.shape[-1]
    row = lax.broadcasted_iota(jnp.int32, (hd, tq), 0)
    sub0 = (row % DIFF_SUB) < (DIFF_SUB // 2)
    zero = jnp.zeros((hd, tq), BF16)
    heads = range(DIFF_HPS)

    @pl.when(i == 0)
    def _():
        for hh in heads:
            kf = k_ref[0, :, hh * hd:(hh + 1) * hd].astype(F32)
            kmax_ref[hh] = jnp.sqrt(jnp.max(jnp.sum(kf * kf, axis=1, keepdims=True), axis=0, keepdims=True))

    qms, bounds = [], []
    for hh in heads:
        qt = qt_ref[0, hh * hd:(hh + 1) * hd, :]
        qms.append((jnp.where(sub0, qt, zero), jnp.where(sub0, zero, qt)))
        bounds.append([jnp.sqrt(jnp.sum(jnp.square(qm.astype(F32)), axis=0, keepdims=True)) * kmax_ref[hh]
                       for qm in qms[hh]])

    def finish(hh, acc):
        outs = [a[:hd, :] * (1.0 / a[hd:hd + 1, :]) for a in acc]
        o = outs[0] - lam * outs[1]
        o = o * lax.rsqrt(jnp.mean(o * o, axis=0, keepdims=True) + RMS_EPS) * sw_ref[...] * (1.0 - lam_init)
        o_ref[0, :, hh * hd:(hh + 1) * hd] = o.T.astype(BF16)

    def run(k0, nk):
        chunks = [(k0 + c0, min(DIFF_KCH, nk - c0)) for c0 in range(0, nk, DIFF_KCH)]

        def attend(hh, shifted):
            def scores(ci):
                c0, n = chunks[ci]
                return [_dot(k_ref[0, c0:c0 + n, hh * hd:(hh + 1) * hd], qm) for qm in qms[hh]]

            s_next = scores(0)
            mx = [None, None]
            acc = [None, None]
            for ci, (c0, n) in enumerate(chunks):
                s_cur = s_next
                if ci + 1 < len(chunks):
                    s_next = scores(ci + 1)
                vv = vt_ref[0, hh, :, c0:c0 + n]
                for sub in range(2):
                    if shifted:
                        pv = _dot(vv, jnp.exp2(s_cur[sub] - bounds[hh][sub]).astype(BF16))
                        acc[sub] = pv if ci == 0 else acc[sub] + pv
                    else:
                        cm = jnp.max(s_cur[sub], axis=0, keepdims=True)
                        m_new = cm if ci == 0 else jnp.maximum(mx[sub], cm)
                        pv = _dot(vv, jnp.exp2(s_cur[sub] - m_new).astype(BF16))
                        acc[sub] = pv if ci == 0 else acc[sub] * jnp.exp2(mx[sub] - m_new) + pv
                        mx[sub] = m_new
            return acc

        oks = []
        for hh in heads:
            acc = attend(hh, True)
            sums = jnp.concatenate([a[hd:hd + 1, :] for a in acc], axis=0)
            oks.append(jnp.logical_and(jnp.min(sums) > SUM_LO, jnp.max(sums) < SUM_HI))
            finish(hh, acc)

        for hh in heads:
            @pl.when(jnp.logical_not(oks[hh]))
            def _(hh=hh):
                finish(hh, attend(hh, False))

    @pl.when(i < nxt)
    def _():
        run(0, n_lat + n_ctx)

    @pl.when(i >= nxt)
    def _():
        run(n_lat, n_ctx)


def _diff_attn(qt, k, vt, lam_p, subln_b, n_lat, lam_init, n_q_tiles, tq):
    nb, t, _ = k.shape
    n_ctx = t - n_lat
    nxt = n_lat // tq
    hd = 2 * DIFF_SUB
    return pl.pallas_call(
        functools.partial(_diff_attn_kernel, n_lat=n_lat, n_ctx=n_ctx, nxt=nxt, lam_init=lam_init),
        grid=(nb, DIFF_HEADS // DIFF_HPS, n_q_tiles),
        in_specs=[pl.BlockSpec((1, DIFF_HPS * hd, tq), lambda b, hh, i: (b, hh, i)),
                  pl.BlockSpec((1, t, DIFF_HPS * hd), lambda b, hh, i: (b, 0, hh)),
                  pl.BlockSpec((1, DIFF_HPS, DIFF_VROWS, t), lambda b, hh, i: (b, hh, 0, 0)),
                  _resident(lam_p.shape), _resident(subln_b.shape)],
        out_specs=pl.BlockSpec((1, tq, DIFF_HPS * hd), lambda b, hh, i: (b, i, hh)),
        out_shape=jax.ShapeDtypeStruct((nb, n_q_tiles * tq, D_MODEL), BF16),
        scratch_shapes=[pltpu.VMEM((DIFF_HPS, 1, 1), F32)],
        name="diff_attn",
        compiler_params=_params(("parallel", "parallel", "arbitrary")),
    )(qt, k, vt, lam_p, subln_b)


def _diff_out_kernel(a_ref, h_ref, mod_ref, w_ref, g_ref, b_ref, o_ref):
    o = _dot(a_ref[0], w_ref[...])
    gate = mod_ref[0][2:3, :]
    o_ref[0] = _layer_norm(ALPHA * h_ref[0] + gate * o, g_ref[...], b_ref[...])


def _diff_out(att, h, mod, w, ln_g, ln_b, nxt, n_tiles):
    nb, t, _ = h.shape
    return pl.pallas_call(
        _diff_out_kernel,
        grid=(nb, n_tiles),
        in_specs=[pl.BlockSpec((1, TM, D_MODEL), lambda b, i: (b, i, 0)),
                  pl.BlockSpec((1, TM, D_MODEL), lambda b, i: (b, i, 0)),
                  _mod_spec(nxt, nb),
                  _resident(w.shape), _resident(ln_g.shape), _resident(ln_b.shape)],
        out_specs=pl.BlockSpec((1, TM, D_MODEL), lambda b, i: (b, i, 0)),
        out_shape=jax.ShapeDtypeStruct((nb, n_tiles * TM, D_MODEL), F32),
        name="diff_out",
        compiler_params=_params(("parallel", "parallel")),
    )(att, h, mod, w, ln_g, ln_b)


def _rope_tables(n_lat, n_ctx):
    tok = np.arange(n_lat)
    row = (tok // GRID_W).astype(np.float32)
    col = (tok % GRID_W).astype(np.float32)
    n_freq = DIFF_SUB // 4
    inv = jnp.asarray(ROPE_BASE, F32) ** (-jnp.arange(n_freq, dtype=F32) / n_freq)
    ang = jnp.concatenate([jnp.asarray(row)[:, None] * inv, jnp.asarray(col)[:, None] * inv], axis=-1)
    cos = jnp.cos(ang)
    sin = jnp.sin(ang)
    cos_t = jnp.concatenate([cos, cos, cos, cos], axis=-1)
    sin_t = jnp.concatenate([-sin, -sin, sin, sin], axis=-1)
    cos_t = jnp.concatenate([cos_t, jnp.ones((n_ctx, LANES), F32)], axis=0)
    sin_t = jnp.concatenate([sin_t, jnp.zeros((n_ctx, LANES), F32)], axis=0)
    return cos_t, sin_t


def _diff_head_perm():
    half = DIFF_SUB // 2
    perm = []
    for hh in range(DIFF_HEADS):
        base = hh * 2 * DIFF_SUB
        for part in range(2):
            for sub in range(2):
                start = base + sub * DIFF_SUB + part * half
                perm.extend(range(start, start + half))
    return np.asarray(perm)


def kernel(x, c, ctx, c_ctx, ada_w, ada_b, ln1_g, ln1_b, ln2_g, ln2_b, ffn_w_up, ffn_conv_w, ffn_conv_b, ffn_w_down, hyb_w_in, ssd_conv_w, ssd_conv_b, ssd_a_log, ssd_dt_bias, ssd_d, ssd_norm_w, na_rpb, hyb_w_out, diff_w_in, diff_lambda, diff_subln_w, diff_w_out):
    nb, n_lat, d = x.shape
    n_ctx = ctx.shape[1]
    assert d == D_MODEL and n_lat % TM == 0 and n_ctx % TM == 0 and n_lat % (NA_WIN_ROWS * GRID_W) == 0
    t = n_lat + n_ctx
    nt = t // TM
    nxt = n_lat // TM

    h = jnp.concatenate([x, ctx], axis=1)
    cond_rows = -(-(nb + FFN_BPS) // 8) * 8
    cond = jnp.concatenate([c, jnp.broadcast_to(c_ctx[None, :], (FFN_BPS, d)),
                            jnp.zeros((cond_rows - nb - FFN_BPS, d), F32)], axis=0)
    mod_all = _ada_mod(cond, ada_w, ada_b).reshape(DEPTH, cond_rows, 6, d)
    cos_t, sin_t = _rope_tables(n_lat, n_ctx)
    perm = _diff_head_perm()
    na_variants, na_vid = _na_variants(n_lat // GRID_W)
    expand = jnp.asarray(np.kron(np.eye(LANES, SSD_H, dtype=np.float32), np.ones((1, SSD_P), np.float32)))

    for i in range(DEPTH):
        last = i == DEPTH - 1
        j = i // 2
        mod = mod_all[i]
        n_tiles = nxt if last else nt
        row = lambda v: v.reshape(1, -1)
        if i % 2 == 0:
            w_in = hyb_w_in[j]
            o_xbc = SSD_INNER
            o_dt = o_xbc + SSD_CONV_DIM
            o_q = o_dt + 2 * SSD_H
            w_main = jnp.concatenate([w_in[:, o_xbc:o_dt], w_in[:, :SSD_INNER],
                                      w_in[:, o_q:o_q + NA_WIDTH] * NA_DH ** -0.5,
                                      w_in[:, o_q + NA_WIDTH:]], axis=1).astype(BF16)
            w_dt = jnp.zeros((d, 2 * LANES), F32)
            w_dt = w_dt.at[:, :SSD_H].set(w_in[:, o_dt:o_dt + SSD_H])
            w_dt = w_dt.at[:, LANES:LANES + SSD_H].set(w_in[:, o_dt + SSD_H:o_q]).astype(BF16)
            pm, dt, na_qt, na_k, na_vt = _hyb_proj(h, mod, w_main, w_dt, ssd_conv_w[j], row(ssd_conv_b[j]), nxt)

            pad = lambda v: jnp.zeros((2, 1, LANES), F32).at[:, 0, :SSD_H].set(v)
            arow = pad(-jnp.exp(ssd_a_log[j].astype(F32)))
            brow = pad(ssd_dt_bias[j].astype(F32))
            dsk = jnp.repeat(ssd_d[j].astype(F32), SSD_P, axis=-1).reshape(2, 1, SSD_INNER)
            yf, yb = _ssd(pm, dt, arow, brow, dsk, expand, n_lat)
            att = _na(na_qt, na_k, na_vt, _na_bias_table(na_rpb[j], na_variants), na_vid, n_lat)
            w_out = hyb_w_out[j].astype(BF16)
            h = _hyb_out(yf, yb, pm, att, h, mod, row(ssd_norm_w[j]), w_out[:SSD_INNER], w_out[SSD_INNER:],
                         row(ln1_g[i]), row(ln1_b[i]), nxt)
        else:
            lam_init = 0.8 - 0.6 * math.exp(-0.3 * i)
            w_in = diff_w_in[j]
            wq = w_in[:, :d][:, perm] * (DIFF_SUB ** -0.5 * LOG2E)
            wk = w_in[:, d:2 * d][:, perm]
            w_qkv = jnp.concatenate([wq, wk, w_in[:, 2 * d:]], axis=1).astype(BF16)
            d_qt, d_k, d_vt = _diff_proj(h, mod, w_qkv, cos_t, sin_t, nxt)
            subln_b = jnp.broadcast_to(diff_subln_w[j].astype(F32)[:, None], (2 * DIFF_SUB, TM))
            att = _diff_attn(d_qt, d_k, d_vt, diff_lambda[j].astype(F32), subln_b, n_lat, lam_init, n_tiles, TM)
            h = _diff_out(att, h, mod, diff_w_out[j].astype(BF16), row(ln1_g[i]), row(ln1_b[i]), nxt, n_tiles)
        h = _ffn(h, mod, ffn_w_up[i].astype(BF16), ffn_conv_w[i], row(ffn_conv_b[i]),
                 ffn_w_down[i].astype(BF16), row(ln2_g[i]), row(ln2_b[i]), nxt, n_tiles)
    return h
```

```python
import functools
import math

import numpy as np
import jax
import jax.numpy as jnp
from jax import lax
from jax.experimental import pallas as pl
from jax.experimental.pallas import tpu as pltpu

F32 = jnp.float32
BF16 = jnp.bfloat16

D_MODEL = 1024
DEPTH = 4
GRID_W = 64
SSD_P = 64
SSD_H = 16
SSD_G = 4
SSD_R = SSD_H // SSD_G
SSD_N = 128
SSD_INNER = SSD_H * SSD_P
SSD_GN = SSD_G * SSD_N
SSD_CONV_W = 5
SSD_CONV_DIM = SSD_INNER + 2 * SSD_GN
SSD_CHUNK = 128
SSD_BPS = 2
NA_HEADS = 8
NA_DH = 64
NA_WIDTH = NA_HEADS * NA_DH
NA_WIN_ROWS = 8
NA_WIN_COLS = 16
DIFF_HEADS = 8
DIFF_SUB = 64
ROPE_BASE = 10000.0
D_FF = 2816
FFN_CH = 256
FFN_BPS = 2
ALPHA = (2.0 * DEPTH) ** 0.25
LN_EPS = 1e-5
RMS_EPS = 1e-5

LANES = 128
TM = 256
HALO_F32 = 8
HALO_BF16 = 16
VMEM_LIMIT = 56 * 1024 * 1024
HYB_SSD = SSD_CONV_DIM + SSD_INNER
HYB_MAIN = HYB_SSD + 3 * NA_WIDTH
NA_VROWS = NA_DH + HALO_BF16
NA_BLK_ROWS = 4
NA_KEY_BLKS = 3
DIFF_VROWS = 2 * DIFF_SUB + HALO_BF16
DIFF_KCH = 512
DIFF_HPS = 4
LOG2E = 1.4426950408889634
SUM_LO = 2.0 ** -90
SUM_HI = 2.0 ** 100
HYB_NCH = 512


def _params(sem, vmem=VMEM_LIMIT):
    return pltpu.CompilerParams(dimension_semantics=sem, vmem_limit_bytes=vmem)


def _resident(shape):
    nd = len(shape)
    return pl.BlockSpec(shape, lambda *_: (0,) * nd, pipeline_mode=pl.Buffered(1))


def _dot(a, b):
    return jnp.dot(a, b, preferred_element_type=F32)


def _dot_nt(a, b):
    return lax.dot_general(a, b, (((1,), (1,)), ((), ())), preferred_element_type=F32)


def _silu(v):
    return v * jax.nn.sigmoid(v)


def _layer_norm(r, g, b):
    mu = jnp.mean(r, axis=-1, keepdims=True)
    xc = r - mu
    var = jnp.mean(xc * xc, axis=-1, keepdims=True)
    return xc * lax.rsqrt(var + LN_EPS) * g + b


def _mod_spec(nxt, nb):
    return pl.BlockSpec((1, 6, D_MODEL), lambda b, i: (jnp.where(i >= nxt, nb, b), 0, 0))


def _ada_kernel(c_ref, w_ref, b_ref, o_ref):
    s = _silu(c_ref[...]).astype(BF16)
    o_ref[0] = _dot(s, w_ref[0].astype(BF16)) + b_ref[0]


def _ada_mod(cond, ada_w, ada_b):
    rows = cond.shape[0]
    n = ada_w.shape[-1]
    tn = n // 4
    return pl.pallas_call(
        _ada_kernel,
        grid=(DEPTH, n // tn),
        in_specs=[pl.BlockSpec((rows, D_MODEL), lambda l, j: (0, 0)),
                  pl.BlockSpec((1, D_MODEL, tn), lambda l, j: (l, 0, j)),
                  pl.BlockSpec((1, 1, tn), lambda l, j: (l, 0, j))],
        out_specs=pl.BlockSpec((1, rows, tn), lambda l, j: (l, 0, j)),
        out_shape=jax.ShapeDtypeStruct((DEPTH, rows, n), F32),
        name="ada_mod",
        compiler_params=_params(("parallel", "parallel")),
    )(cond, ada_w, ada_b.reshape(DEPTH, 1, n))


def _hyb_proj_kernel(h_ref, hp_ref, hn_ref, mod_ref, w_ref, wdt_ref, cw_ref, cb_ref,
                     o_ref, dt_ref, qt_ref, k_ref, vt_ref, ext_ref, *, nxt, nt):
    i = pl.program_id(1)
    m = mod_ref[0]
    shift, scale = m[0:1, :], m[1:2, :]
    seg_start = jnp.logical_or(i == 0, i == nxt)
    seg_end = jnp.logical_or(i == nxt - 1, i == nt - 1)
    um = h_ref[0] * (1.0 + scale) + shift
    u = um.astype(BF16)
    up = jnp.where(seg_start, 0.0, hp_ref[0] * (1.0 + scale) + shift)
    un = jnp.where(seg_end, 0.0, hn_ref[0] * (1.0 + scale) + shift)
    uext = jnp.concatenate([up, um, un], axis=0).astype(BF16)

    def xbc_proj(c):
        ext_ref[c % 2] = _dot(uext, w_ref[:, c * HYB_NCH:(c + 1) * HYB_NCH])

    n_ch = SSD_CONV_DIM // HYB_NCH
    xbc_proj(0)
    for c in range(n_ch):
        n0 = c * HYB_NCH
        if c + 1 < n_ch:
            xbc_proj(c + 1)
        acc = cb_ref[:, n0:n0 + HYB_NCH]
        for k in range(SSD_CONV_W):
            acc = acc + cw_ref[k:k + 1, n0:n0 + HYB_NCH] * ext_ref[c % 2, pl.ds(HALO_F32 - SSD_CONV_W // 2 + k, TM), :]
        o_ref[0, :, n0:n0 + HYB_NCH] = _silu(acc).astype(BF16)
    for n0 in range(SSD_CONV_DIM, HYB_SSD, HYB_NCH):
        o_ref[0, :, n0:n0 + HYB_NCH] = _dot(u, w_ref[:, n0:n0 + HYB_NCH]).astype(BF16)
    dt_ref[0] = _dot(u, wdt_ref[...])
    qt_ref[0] = _dot(u, w_ref[:, HYB_SSD:HYB_SSD + NA_WIDTH]).T.astype(BF16)
    k_ref[0] = _dot(u, w_ref[:, HYB_SSD + NA_WIDTH:HYB_SSD + 2 * NA_WIDTH]).astype(BF16)
    v_t = _dot(u, w_ref[:, HYB_SSD + 2 * NA_WIDTH:HYB_MAIN]).T
    for hh in range(NA_HEADS):
        vt_ref[0, hh, 0:NA_DH, :] = v_t[hh * NA_DH:(hh + 1) * NA_DH, :].astype(BF16)
        vt_ref[0, hh, NA_DH:NA_VROWS, :] = jnp.ones((NA_VROWS - NA_DH, TM), BF16)


def _hyb_proj(h, mod, w_main, w_dt, conv_w, conv_b, nxt):
    nb, t, _ = h.shape
    nt = t // TM
    hb = TM // HALO_F32
    last_hb = t // HALO_F32 - 1
    return pl.pallas_call(
        functools.partial(_hyb_proj_kernel, nxt=nxt, nt=nt),
        grid=(nb, nt),
        in_specs=[pl.BlockSpec((1, TM, D_MODEL), lambda b, i: (b, i, 0)),
                  pl.BlockSpec((1, HALO_F32, D_MODEL), lambda b, i: (b, jnp.maximum(i * hb - 1, 0), 0)),
                  pl.BlockSpec((1, HALO_F32, D_MODEL), lambda b, i: (b, jnp.minimum((i + 1) * hb, last_hb), 0)),
                  _mod_spec(nxt, nb),
                  _resident(w_main.shape),
                  _resident(w_dt.shape),
                  _resident(conv_w.shape),
                  _resident(conv_b.shape)],
        out_specs=[pl.BlockSpec((1, TM, HYB_SSD), lambda b, i: (b, i, 0)),
                   pl.BlockSpec((1, TM, 2 * LANES), lambda b, i: (b, i, 0)),
                   pl.BlockSpec((1, NA_WIDTH, TM), lambda b, i: (b, 0, i)),
                   pl.BlockSpec((1, TM, NA_WIDTH), lambda b, i: (b, i, 0)),
                   pl.BlockSpec((1, NA_HEADS, NA_VROWS, TM), lambda b, i: (b, 0, 0, i))],
        out_shape=[jax.ShapeDtypeStruct((nb, t, HYB_SSD), BF16),
                   jax.ShapeDtypeStruct((nb, t, 2 * LANES), F32),
                   jax.ShapeDtypeStruct((nb, NA_WIDTH, t), BF16),
                   jax.ShapeDtypeStruct((nb, t, NA_WIDTH), BF16),
                   jax.ShapeDtypeStruct((nb, NA_HEADS, NA_VROWS, t), BF16)],
        scratch_shapes=[pltpu.VMEM((2, TM + 2 * HALO_F32, HYB_NCH), F32)],
        name="hyb_proj",
        compiler_params=_params(("parallel", "parallel")),
    )(h, h, h, mod, w_main, w_dt, conv_w, conv_b)


def _ssd_kernel(uf_ref, ub_ref, dtf_ref, dtb_ref, arow_ref, brow_ref, dsk_ref, e_ref, yf_ref, yb_ref, st_ref):
    j = pl.program_id(1)

    @pl.when(j == 0)
    def _():
        st_ref[...] = jnp.zeros_like(st_ref)

    for bi in range(SSD_BPS):
        _ssd_chunk(0, bi, uf_ref, dtf_ref, arow_ref, brow_ref, dsk_ref, e_ref, yf_ref, st_ref)
        _ssd_chunk(1, bi, ub_ref, dtb_ref, arow_ref, brow_ref, dsk_ref, e_ref, yb_ref, st_ref)


def _ssd_chunk(d, bi, u_ref, dt_ref, arow_ref, brow_ref, dsk_ref, e_ref, y_ref, st_ref):
    q = SSD_CHUNK
    u = u_ref[bi].astype(F32)
    xs = u[:, :SSD_INNER]

    dtr = dt_ref[bi] + brow_ref[d]
    dtv = jnp.maximum(dtr, 0.0) + jnp.log1p(jnp.exp(-jnp.abs(dtr)))
    adt = dtv * arow_ref[d]
    ri = lax.broadcasted_iota(jnp.int32, (q, q), 0)
    ci = lax.broadcasted_iota(jnp.int32, (q, q), 1)
    tri = (ri >= ci) if d == 0 else (ri <= ci)
    cs = jnp.dot(tri.astype(F32), adt, precision=lax.Precision.HIGHEST, preferred_element_type=F32)
    cs_t = cs.T
    dt_t = dtv.T
    last = q - 1 if d == 0 else 0
    tot = cs[last:last + 1, :]
    tot_t = cs_t[:, last:last + 1]
    w_t = jnp.exp(tot_t - cs_t) * dt_t
    dec_row = jnp.dot(jnp.broadcast_to(jnp.exp(tot), (8, LANES)), e_ref[...],
                      precision=lax.Precision.HIGHEST, preferred_element_type=F32)[0:1]

    lane = lax.broadcasted_iota(jnp.int32, (q, LANES), 1)
    lo = lane < SSD_P
    dsk = dsk_ref[d]
    for g in range(SSD_G):
        bm = u[:, SSD_INNER + g * SSD_N:SSD_INNER + (g + 1) * SSD_N]
        cm = u[:, SSD_INNER + SSD_GN + g * SSD_N:SSD_INNER + SSD_GN + (g + 1) * SSD_N]
        cmb = cm.astype(BF16)
        cb = _dot_nt(cmb, bm.astype(BF16))
        bm_t = bm.T
        s_prev = st_ref[bi, d, g]
        y_off = _dot(cmb, s_prev.astype(BF16))
        s_parts = []
        for pr in range(SSD_R // 2):
            col0 = g * SSD_R * SSD_P + pr * LANES
            xs_pair = xs[:, col0:col0 + LANES]
            xsb = xs_pair.astype(BF16)
            yd, sc, colbs = [], [], []
            for sub in range(2):
                h = g * SSD_R + pr * 2 + sub
                colb = jnp.broadcast_to(cs[:, h:h + 1], (q, q))
                decay = jnp.exp(jnp.where(tri, colb - cs_t[h:h + 1, :], -jnp.inf))
                gmat = (cb * decay * dt_t[h:h + 1, :]).astype(BF16)
                yd.append(_dot(gmat, xsb))
                sc.append(_dot((bm_t * w_t[h:h + 1, :]).astype(BF16), xsb))
                colbs.append(colb)
            y_diag = jnp.where(lo, yd[0], yd[1])
            e_col = jnp.exp(jnp.where(lo, colbs[0], colbs[1]))
            y_pair = y_diag + y_off[:, pr * LANES:(pr + 1) * LANES] * e_col + dsk[:, col0:col0 + LANES] * xs_pair
            y_ref[bi, :, col0:col0 + LANES] = y_pair.astype(BF16)
            s_parts.append(jnp.where(lo, sc[0], sc[1]))
        g0 = g * SSD_R * SSD_P
        st_ref[bi, d, g] = s_prev * dec_row[:, g0:g0 + SSD_R * SSD_P] + jnp.concatenate(s_parts, axis=1)


def _ssd(pm, dt, arow, brow, dsk, expand, n_lat):
    nb, t, _ = pm.shape
    q = SSD_CHUNK
    nc = t // q
    nxc = n_lat // q

    def cf(j):
        return (j + nxc) % nc

    def cbk(j):
        return nc - 1 - j

    y_shape = jax.ShapeDtypeStruct((nb, t, SSD_INNER), BF16)
    bps = SSD_BPS
    assert nb % bps == 0
    return pl.pallas_call(
        _ssd_kernel,
        grid=(nb // bps, nc),
        in_specs=[pl.BlockSpec((bps, q, SSD_CONV_DIM), lambda b, j: (b, cf(j), 0)),
                  pl.BlockSpec((bps, q, SSD_CONV_DIM), lambda b, j: (b, cbk(j), 0)),
                  pl.BlockSpec((bps, q, LANES), lambda b, j: (b, cf(j), 0)),
                  pl.BlockSpec((bps, q, LANES), lambda b, j: (b, cbk(j), 1)),
                  _resident(arow.shape), _resident(brow.shape), _resident(dsk.shape), _resident(expand.shape)],
        out_specs=[pl.BlockSpec((bps, q, SSD_INNER), lambda b, j: (b, cf(j), 0)),
                   pl.BlockSpec((bps, q, SSD_INNER), lambda b, j: (b, cbk(j), 0))],
        out_shape=[y_shape, y_shape],
        scratch_shapes=[pltpu.VMEM((bps, 2, SSD_G, SSD_N, SSD_R * SSD_P), F32)],
        name="ssd_scan",
        compiler_params=_params(("parallel", "arbitrary")),
    )(pm, pm, dt, dt, arow, brow, dsk, expand)


def _na_kernel(qt_ref, k0_ref, k1_ref, k2_ref, kc_ref, v0_ref, v1_ref, v2_ref, vc_ref, bias_ref, o_ref, *, nblk):
    i = pl.program_id(1)
    nq = NA_BLK_ROWS * GRID_W
    zeros = jnp.zeros((NA_DH, nq), BF16)
    local = ((k0_ref, v0_ref), (k1_ref, v1_ref), (k2_ref, v2_ref))

    def run(with_local):
        def scores(hh):
            sl = slice((hh // 2) * LANES, (hh // 2 + 1) * LANES)
            qh = qt_ref[0, hh * NA_DH:(hh + 1) * NA_DH, :]
            qm = jnp.concatenate([qh, zeros] if hh % 2 == 0 else [zeros, qh], axis=0)
            sc = []
            if with_local:
                for jb, (kr, _) in enumerate(local):
                    sc.append(_dot(kr[0, :, sl], qm) + bias_ref[0, hh, jb * nq:(jb + 1) * nq, :])
            sc.append(_dot(kc_ref[0, :, sl], qm))
            return sc

        outs = []
        s_next = scores(0)
        for hh in range(NA_HEADS):
            s_cur = s_next
            if hh + 1 < NA_HEADS:
                s_next = scores(hh + 1)
            vals = ([vr[0, hh] for _, vr in local] if with_local else []) + [vc_ref[0, hh]]
            mx = functools.reduce(jnp.maximum, [jnp.max(s, axis=0, keepdims=True) for s in s_cur])
            o_t = functools.reduce(
                lambda a, b: a + b, [_dot(v, jnp.exp(s - mx).astype(BF16)) for s, v in zip(s_cur, vals)])
            outs.append(o_t[:NA_DH, :] * (1.0 / o_t[NA_DH:NA_DH + 1, :]))
        o_ref[0] = jnp.concatenate(outs, axis=0).T.astype(BF16)

    @pl.when(i < nblk)
    def _():
        run(True)

    @pl.when(i >= nblk)
    def _():
        run(False)


def _na_variants(rows):
    nblk = rows // NA_BLK_ROWS
    assert rows % NA_BLK_ROWS == 0 and nblk >= NA_KEY_BLKS
    keys, vid = [], []
    for b in range(nblk):
        ws = int(np.clip(b - 1, 0, nblk - NA_KEY_BLKS)) * NA_BLK_ROWS
        key = []
        for a in range(NA_BLK_ROWS):
            r = b * NA_BLK_ROWS + a
            r0 = int(np.clip(r - NA_WIN_ROWS // 2, 0, rows - NA_WIN_ROWS))
            assert 0 <= r0 - ws and r0 - ws + NA_WIN_ROWS <= NA_KEY_BLKS * NA_BLK_ROWS
            key.append((r0 - ws, r0 - r))
        key = tuple(key)
        if key not in keys:
            keys.append(key)
        vid.append(keys.index(key))
    return keys, vid


def _na_bias_table(rpb, variants):
    c = np.arange(GRID_W)[:, None]
    kc = np.arange(GRID_W)[None, :]
    c0 = np.clip(c - NA_WIN_COLS // 2, 0, GRID_W - NA_WIN_COLS)
    valid_c = (kc >= c0) & (kc < c0 + NA_WIN_COLS)
    dc = kc - c + NA_WIN_COLS - 1
    onehot = ((dc[None] == np.arange(2 * NA_WIN_COLS - 1)[:, None, None]) & valid_c[None]).astype(np.float32)
    tz = jnp.einsum('hrd,dck->hrck', rpb.astype(F32), jnp.asarray(onehot), precision=lax.Precision.HIGHEST)
    nkr = NA_KEY_BLKS * NA_BLK_ROWS
    ndr = 2 * NA_WIN_ROWS - 1
    tabs = []
    for key in variants:
        sel = np.zeros((NA_BLK_ROWS, nkr, ndr), np.float32)
        valid_r = np.zeros((NA_BLK_ROWS, nkr), bool)
        for a, (off, e) in enumerate(key):
            for w in range(NA_WIN_ROWS):
                sel[a, off + w, e + NA_WIN_ROWS - 1 + w] = 1.0
                valid_r[a, off + w] = True
        tab = jnp.einsum('ajr,hrck->hajck', jnp.asarray(sel), tz, precision=lax.Precision.HIGHEST)
        ok = jnp.asarray(valid_r[None, :, :, None, None] & valid_c[None, None, None, :, :])
        tab = jnp.where(ok, tab, -jnp.inf)
        tabs.append(tab.transpose(0, 2, 4, 1, 3).reshape(NA_HEADS, nkr * GRID_W, NA_BLK_ROWS * GRID_W))
    return jnp.stack(tabs)


def _na(qt, k, vt, bias, vid, n_lat):
    nb, t, _ = k.shape
    n_ctx = t - n_lat
    nq = NA_BLK_ROWS * GRID_W
    nblk = n_lat // nq
    assert n_ctx % nq == 0 and n_lat % n_ctx == 0
    steps = t // nq
    cblk = n_lat // n_ctx

    def wb(i):
        return jnp.clip(i - 1, 0, nblk - NA_KEY_BLKS)

    def variant(i):
        v = jnp.int32(0)
        for blk in range(1, nblk):
            v = jnp.where(i == blk, vid[blk], v)
        return v

    kspec = [pl.BlockSpec((1, nq, NA_WIDTH), functools.partial(lambda b, i, j: (b, wb(i) + j, 0), j=j))
             for j in range(NA_KEY_BLKS)]
    vspec = [pl.BlockSpec((1, NA_HEADS, NA_VROWS, nq), functools.partial(lambda b, i, j: (b, 0, 0, wb(i) + j), j=j))
             for j in range(NA_KEY_BLKS)]
    return pl.pallas_call(
        functools.partial(_na_kernel, nblk=nblk),
        grid=(nb, steps),
        in_specs=([pl.BlockSpec((1, NA_WIDTH, nq), lambda b, i: (b, 0, i))] + kspec
                  + [pl.BlockSpec((1, n_ctx, NA_WIDTH), lambda b, i: (b, cblk, 0))] + vspec
                  + [pl.BlockSpec((1, NA_HEADS, NA_VROWS, n_ctx), lambda b, i: (b, 0, 0, cblk)),
                     pl.BlockSpec((1,) + bias.shape[1:], lambda b, i: (variant(i), 0, 0, 0))]),
        out_specs=pl.BlockSpec((1, nq, NA_WIDTH), lambda b, i: (b, i, 0)),
        out_shape=jax.ShapeDtypeStruct((nb, t, NA_WIDTH), BF16),
        name="na_attn",
        compiler_params=_params(("parallel", "arbitrary")),
    )(qt, k, k, k, k, vt, vt, vt, vt, bias)


def _hyb_out_kernel(yf_ref, yb_ref, z_ref, a_ref, h_ref, mod_ref, nw_ref, wy_ref, wa_ref, g_ref, b_ref, o_ref):
    y = (yf_ref[0].astype(F32) + yb_ref[0].astype(F32)) * _silu(z_ref[0].astype(F32))
    y = y * lax.rsqrt(jnp.mean(y * y, axis=-1, keepdims=True) + RMS_EPS) * nw_ref[...]
    o = _dot(y.astype(BF16), wy_ref[...]) + _dot(a_ref[0], wa_ref[...])
    gate = mod_ref[0][2:3, :]
    o_ref[0] = _layer_norm(ALPHA * h_ref[0] + gate * o, g_ref[...], b_ref[...])


def _hyb_out(yf, yb, pm, att, h, mod, norm_w, wy, wa, ln_g, ln_b, nxt):
    nb, t, _ = h.shape
    nt = t // TM
    zcol = SSD_CONV_DIM // SSD_INNER
    return pl.pallas_call(
        _hyb_out_kernel,
        grid=(nb, nt),
        in_specs=[pl.BlockSpec((1, TM, SSD_INNER), lambda b, i: (b, i, 0)),
                  pl.BlockSpec((1, TM, SSD_INNER), lambda b, i: (b, i, 0)),
                  pl.BlockSpec((1, TM, SSD_INNER), lambda b, i: (b, i, zcol)),
                  pl.BlockSpec((1, TM, NA_WIDTH), lambda b, i: (b, i, 0)),
                  pl.BlockSpec((1, TM, D_MODEL), lambda b, i: (b, i, 0)),
                  _mod_spec(nxt, nb),
                  _resident(norm_w.shape), _resident(wy.shape), _resident(wa.shape),
                  _resident(ln_g.shape), _resident(ln_b.shape)],
        out_specs=pl.BlockSpec((1, TM, D_MODEL), lambda b, i: (b, i, 0)),
        out_shape=jax.ShapeDtypeStruct((nb, t, D_MODEL), F32),
        name="hyb_out",
        compiler_params=_params(("parallel", "parallel")),
    )(yf, yb, pm, att, h, mod, norm_w, wy, wa, ln_g, ln_b)


def _ffn_kernel(hm_ref, hp_ref, hn_ref, mod_ref, wup_ref, cw_ref, cb_ref, wdn_ref, g_ref, b_ref, p_ref, o_ref,
                *scratch, nxt, nt):
    per_tile = len(scratch) // FFN_BPS
    hv_ref = [scratch[bi * per_tile:bi * per_tile + 2] for bi in range(FFN_BPS)]
    hg_ref = [scratch[bi * per_tile + 2:bi * per_tile + 4] for bi in range(FFN_BPS)]
    acc_ref = [scratch[bi * per_tile + 4] for bi in range(FFN_BPS)]
    perm_ref = [scratch[bi * per_tile + 5] for bi in range(FFN_BPS)]
    i = pl.program_id(1)
    seg_start = jnp.logical_or(i == 0, i == nxt)
    seg_end = jnp.logical_or(i == nxt - 1, i == nt - 1)
    sl = HALO_F32
    ngrp = TM // sl
    ncb = D_MODEL // LANES
    sub_d = lax.broadcasted_iota(jnp.int32, (2 * sl, D_MODEL), 0)
    sub_c = lax.broadcasted_iota(jnp.int32, (sl, FFN_CH), 0)

    def load_tile(bi):
        m = mod_ref[bi]
        shift, scale = m[3:4, :], m[4:5, :]
        hm = hm_ref[bi]
        u_perm = _dot(p_ref[...], (hm * (1.0 + scale) + shift).astype(BF16)).astype(BF16)
        prev_row = jnp.where(seg_start, 0.0, hp_ref[bi, sl - 1:sl, :] * (1.0 + scale) + shift)
        next_row = jnp.where(seg_end, 0.0, hn_ref[bi, 0:1, :] * (1.0 + scale) + shift)
        halo = jnp.where(sub_d == 0, prev_row, jnp.where(sub_d == 1, next_row, 0.0))
        uext = jnp.concatenate([u_perm, halo.astype(BF16)], axis=0)
        return hm, uext

    def conv(ref, col0):
        w = cw_ref[:, col0:col0 + FFN_CH]
        before_first = jnp.where(sub_c == 0, ref[TM:TM + 1, :], pltpu.roll(ref[TM - sl:TM, :], 1, axis=0))
        after_last = jnp.where(sub_c == sl - 1, ref[TM + 1:TM + 2, :], pltpu.roll(ref[0:sl, :], sl - 1, axis=0))
        prev = jnp.concatenate([before_first, ref[0:TM - sl, :]], axis=0)
        nxt_ = jnp.concatenate([ref[sl:TM, :], after_last], axis=0)
        return w[0:1, :] * prev + w[1:2, :] * ref[0:TM, :] + w[2:3, :] * nxt_ + cb_ref[:, col0:col0 + FFN_CH]

    n_ch = D_FF // FFN_CH

    def up_proj(bi, uext, c):
        hv_ref[bi][c % 2][...] = _dot(uext, wup_ref[:, c * FFN_CH:(c + 1) * FFN_CH])
        hg_ref[bi][c % 2][...] = _dot(uext, wup_ref[:, D_FF + c * FFN_CH:D_FF + (c + 1) * FFN_CH])

    def chunk(bi, uext, c):
        v0 = c * FFN_CH
        if c + 1 < n_ch:
            up_proj(bi, uext, c + 1)
        act = (_silu(conv(hg_ref[bi][c % 2], D_FF + v0)) * conv(hv_ref[bi][c % 2], v0)).astype(BF16)
        contrib = _dot(act, wdn_ref[v0:v0 + FFN_CH, :])
        if c == 0:
            acc_ref[bi][...] = contrib
        else:
            acc_ref[bi][...] += contrib

    def finish(bi, hm):
        gate = mod_ref[bi][5:6, :]
        acc = acc_ref[bi][...]
        for cb in range(ncb):
            for a in range(ngrp):
                perm_ref[bi][cb, pl.ds(a, sl, stride=ngrp), :] = acc[a * sl:(a + 1) * sl, cb * LANES:(cb + 1) * LANES]
        f = jnp.concatenate([perm_ref[bi][cb] for cb in range(ncb)], axis=1)
        o_ref[bi] = _layer_norm(ALPHA * hm + gate * f, g_ref[...], b_ref[...])

    tiles = [load_tile(bi) for bi in range(FFN_BPS)]
    for bi, (hm, uext) in enumerate(tiles):
        up_proj(bi, uext, 0)
        for c in range(n_ch):
            chunk(bi, uext, c)
        finish(bi, hm)


def _ffn(h, mod, w_up, conv_w, conv_b, w_dn, ln_g, ln_b, nxt, n_out_tiles):
    nb, t, _ = h.shape
    nt = t // TM
    hb = TM // HALO_F32
    last_hb = t // HALO_F32 - 1
    bps = FFN_BPS
    assert nb % bps == 0
    ngrp = TM // HALO_F32
    src = (np.arange(TM) % HALO_F32) * ngrp + np.arange(TM) // HALO_F32
    perm = jnp.asarray(np.eye(TM, dtype=np.float32)[src], BF16)
    mod_spec = pl.BlockSpec((bps, 6, D_MODEL), lambda b, i: (jnp.where(i >= nxt, nb // bps, b), 0, 0))
    return pl.pallas_call(
        functools.partial(_ffn_kernel, nxt=nxt, nt=nt),
        grid=(nb // bps, n_out_tiles),
        in_specs=[pl.BlockSpec((bps, TM, D_MODEL), lambda b, i: (b, i, 0)),
                  pl.BlockSpec((bps, HALO_F32, D_MODEL), lambda b, i: (b, jnp.maximum(i * hb - 1, 0), 0)),
                  pl.BlockSpec((bps, HALO_F32, D_MODEL), lambda b, i: (b, jnp.minimum((i + 1) * hb, last_hb), 0)),
                  mod_spec,
                  _resident(w_up.shape), _resident(conv_w.shape), _resident(conv_b.shape),
                  _resident(w_dn.shape), _resident(ln_g.shape), _resident(ln_b.shape), _resident(perm.shape)],
        out_specs=pl.BlockSpec((bps, TM, D_MODEL), lambda b, i: (b, i, 0)),
        out_shape=jax.ShapeDtypeStruct((nb, n_out_tiles * TM, D_MODEL), F32),
        scratch_shapes=bps * ([pltpu.VMEM((TM + 2 * HALO_F32, FFN_CH), F32)] * 4
                              + [pltpu.VMEM((TM, D_MODEL), F32),
                                 pltpu.VMEM((D_MODEL // LANES, TM, LANES), F32)]),
        name="conv_ffn",
        compiler_params=_params(("parallel", "parallel")),
    )(h, h, h, mod, w_up, conv_w, conv_b, w_dn, ln_g, ln_b, perm)


def _diff_proj_kernel(h_ref, mod_ref, w_ref, cos_ref, sin_ref, cost_ref, sint_ref, qt_ref, k_ref, vt_ref):
    m = mod_ref[0]
    u = (h_ref[0] * (1.0 + m[1:2, :]) + m[0:1, :]).astype(BF16)
    half = LANES // 2
    for n0 in range(0, D_MODEL, HYB_NCH):
        y_t = _dot(u, w_ref[:, n0:n0 + HYB_NCH]).T
        for p0 in range(0, HYB_NCH, LANES):
            blk = y_t[p0:p0 + LANES, :]
            partner = jnp.concatenate([blk[half:, :], blk[:half, :]], axis=0)
            qt_ref[0, n0 + p0:n0 + p0 + LANES, :] = (blk * cost_ref[...] + partner * sint_ref[...]).astype(BF16)
    for n0 in range(0, D_MODEL, HYB_NCH):
        y = _dot(u, w_ref[:, D_MODEL + n0:D_MODEL + n0 + HYB_NCH])
        parts = []
        for p0 in range(0, HYB_NCH, LANES):
            yp = y[:, p0:p0 + LANES]
            parts.append(yp * cos_ref[...] + pltpu.roll(yp, half, axis=1) * sin_ref[...])
        k_ref[0, :, n0:n0 + HYB_NCH] = jnp.concatenate(parts, axis=1).astype(BF16)
    hd = 2 * DIFF_SUB
    for n0 in range(0, D_MODEL, HYB_NCH):
        y_t = _dot(u, w_ref[:, 2 * D_MODEL + n0:2 * D_MODEL + n0 + HYB_NCH]).T
        for p0 in range(0, HYB_NCH, hd):
            vt_ref[0, (n0 + p0) // hd, 0:hd, :] = y_t[p0:p0 + hd, :].astype(BF16)
    vt_ref[0, :, hd:DIFF_VROWS, :] = jnp.ones((DIFF_HEADS, DIFF_VROWS - hd, TM), BF16)


def _diff_proj(h, mod, w, cos_t, sin_t, nxt):
    nb, t, _ = h.shape
    nt = t // TM
    return pl.pallas_call(
        _diff_proj_kernel,
        grid=(nb, nt),
        in_specs=[pl.BlockSpec((1, TM, D_MODEL), lambda b, i: (b, i, 0)),
                  _mod_spec(nxt, nb),
                  _resident(w.shape),
                  pl.BlockSpec((TM, LANES), lambda b, i: (i, 0)),
                  pl.BlockSpec((TM, LANES), lambda b, i: (i, 0)),
                  pl.BlockSpec((LANES, TM), lambda b, i: (0, i)),
                  pl.BlockSpec((LANES, TM), lambda b, i: (0, i))],
        out_specs=[pl.BlockSpec((1, D_MODEL, TM), lambda b, i: (b, 0, i)),
                   pl.BlockSpec((1, TM, D_MODEL), lambda b, i: (b, i, 0)),
                   pl.BlockSpec((1, DIFF_HEADS, DIFF_VROWS, TM), lambda b, i: (b, 0, 0, i))],
        out_shape=[jax.ShapeDtypeStruct((nb, D_MODEL, t), BF16),
                   jax.ShapeDtypeStruct((nb, t, D_MODEL), BF16),
                   jax.ShapeDtypeStruct((nb, DIFF_HEADS, DIFF_VROWS, t), BF16)],
        name="diff_proj",
        compiler_params=_params(("parallel", "parallel")),
    )(h, mod, w, cos_t, sin_t, cos_t.T, sin_t.T)


def _diff_attn_kernel(qt_ref, k_ref, vt_ref, lam_ref, sw_ref, o_ref, kmax_ref, *, n_lat, n_ctx, nxt, lam_init):
    i = pl.program_id(2)
    lp = lam_ref[...]
    lam = (jnp.exp(jnp.sum(lp[0:1, :] * lp[1:2, :], axis=1, keepdims=True))
           - jnp.exp(jnp.sum(lp[2:3, :] * lp[3:4, :], axis=1, keepdims=True)) + lam_init)
    hd = 2 * DIFF_SUB
    tq = qt_ref.shape[-1]
    row = lax.broadcasted_iota(jnp.int32, (hd, tq), 0)
    sub0 = (row % DIFF_SUB) < (DIFF_SUB // 2)
    zero = jnp.zeros((hd, tq), BF16)
    heads = range(DIFF_HPS)

    @pl.when(i == 0)
    def _():
        for hh in heads:
            kf = k_ref[0, :, hh * hd:(hh + 1) * hd].astype(F32)
            kmax_ref[hh] = jnp.sqrt(jnp.max(jnp.sum(kf * kf, axis=1, keepdims=True), axis=0, keepdims=True))

    qms, bounds = [], []
    for hh in heads:
        qt = qt_ref[0, hh * hd:(hh + 1) * hd, :]
        qms.append((jnp.where(sub0, qt, zero), jnp.where(sub0, zero, qt)))
        bounds.append([jnp.sqrt(jnp.sum(jnp.square(qm.astype(F32)), axis=0, keepdims=True)) * kmax_ref[hh]
                       for qm in qms[hh]])

    def finish(hh, acc):
        outs = [a[:hd, :] * (1.0 / a[hd:hd + 1, :]) for a in acc]
        o = outs[0] - lam * outs[1]
        o = o * lax.rsqrt(jnp.mean(o * o, axis=0, keepdims=True) + RMS_EPS) * sw_ref[...] * (1.0 - lam_init)
        o_ref[0, :, hh * hd:(hh + 1) * hd] = o.T.astype(BF16)

    def run(k0, nk):
        chunks = [(k0 + c0, min(DIFF_KCH, nk - c0)) for c0 in range(0, nk, DIFF_KCH)]

        def attend(hh, shifted):
            def scores(ci):
                c0, n = chunks[ci]
                return [_dot(k_ref[0, c0:c0 + n, hh * hd:(hh + 1) * hd], qm) for qm in qms[hh]]

            s_next = scores(0)
            mx = [None, None]
            acc = [None, None]
            for ci, (c0, n) in enumerate(chunks):
                s_cur = s_next
                if ci + 1 < len(chunks):
                    s_next = scores(ci + 1)
                vv = vt_ref[0, hh, :, c0:c0 + n]
                for sub in range(2):
                    if shifted:
                        pv = _dot(vv, jnp.exp2(s_cur[sub] - bounds[hh][sub]).astype(BF16))
                        acc[sub] = pv if ci == 0 else acc[sub] + pv
                    else:
                        cm = jnp.max(s_cur[sub], axis=0, keepdims=True)
                        m_new = cm if ci == 0 else jnp.maximum(mx[sub], cm)
                        pv = _dot(vv, jnp.exp2(s_cur[sub] - m_new).astype(BF16))
                        acc[sub] = pv if ci == 0 else acc[sub] * jnp.exp2(mx[sub] - m_new) + pv
                        mx[sub] = m_new
            return acc

        oks = []
        for hh in heads:
            acc = attend(hh, True)
            sums = jnp.concatenate([a[hd:hd + 1, :] for a in acc], axis=0)
            oks.append(jnp.logical_and(jnp.min(sums) > SUM_LO, jnp.max(sums) < SUM_HI))
            finish(hh, acc)

        for hh in heads:
            @pl.when(jnp.logical_not(oks[hh]))
            def _(hh=hh):
                finish(hh, attend(hh, False))

    @pl.when(i < nxt)
    def _():
        run(0, n_lat + n_ctx)

    @pl.when(i >= nxt)
    def _():
        run(n_lat, n_ctx)


def _diff_attn(qt, k, vt, lam_p, subln_b, n_lat, lam_init, n_q_tiles, tq):
    nb, t, _ = k.shape
    n_ctx = t - n_lat
    nxt = n_lat // tq
    hd = 2 * DIFF_SUB
    return pl.pallas_call(
        functools.partial(_diff_attn_kernel, n_lat=n_lat, n_ctx=n_ctx, nxt=nxt, lam_init=lam_init),
        grid=(nb, DIFF_HEADS // DIFF_HPS, n_q_tiles),
        in_specs=[pl.BlockSpec((1, DIFF_HPS * hd, tq), lambda b, hh, i: (b, hh, i)),
                  pl.BlockSpec((1, t, DIFF_HPS * hd), lambda b, hh, i: (b, 0, hh)),
                  pl.BlockSpec((1, DIFF_HPS, DIFF_VROWS, t), lambda b, hh, i: (b, hh, 0, 0)),
                  _resident(lam_p.shape), _resident(subln_b.shape)],
        out_specs=pl.BlockSpec((1, tq, DIFF_HPS * hd), lambda b, hh, i: (b, i, hh)),
        out_shape=jax.ShapeDtypeStruct((nb, n_q_tiles * tq, D_MODEL), BF16),
        scratch_shapes=[pltpu.VMEM((DIFF_HPS, 1, 1), F32)],
        name="diff_attn",
        compiler_params=_params(("parallel", "parallel", "arbitrary")),
    )(qt, k, vt, lam_p, subln_b)


def _diff_out_kernel(a_ref, h_ref, mod_ref, w_ref, g_ref, b_ref, o_ref):
    o = _dot(a_ref[0], w_ref[...])
    gate = mod_ref[0][2:3, :]
    o_ref[0] = _layer_norm(ALPHA * h_ref[0] + gate * o, g_ref[...], b_ref[...])


def _diff_out(att, h, mod, w, ln_g, ln_b, nxt, n_tiles):
    nb, t, _ = h.shape
    return pl.pallas_call(
        _diff_out_kernel,
        grid=(nb, n_tiles),
        in_specs=[pl.BlockSpec((1, TM, D_MODEL), lambda b, i: (b, i, 0)),
                  pl.BlockSpec((1, TM, D_MODEL), lambda b, i: (b, i, 0)),
                  _mod_spec(nxt, nb),
                  _resident(w.shape), _resident(ln_g.shape), _resident(ln_b.shape)],
        out_specs=pl.BlockSpec((1, TM, D_MODEL), lambda b, i: (b, i, 0)),
        out_shape=jax.ShapeDtypeStruct((nb, n_tiles * TM, D_MODEL), F32),
        name="diff_out",
        compiler_params=_params(("parallel", "parallel")),
    )(att, h, mod, w, ln_g, ln_b)


def _rope_tables(n_lat, n_ctx):
    tok = np.arange(n_lat)
    row = (tok // GRID_W).astype(np.float32)
    col = (tok % GRID_W).astype(np.float32)
    n_freq = DIFF_SUB // 4
    inv = jnp.asarray(ROPE_BASE, F32) ** (-jnp.arange(n_freq, dtype=F32) / n_freq)
    ang = jnp.concatenate([jnp.asarray(row)[:, None] * inv, jnp.asarray(col)[:, None] * inv], axis=-1)
    cos = jnp.cos(ang)
    sin = jnp.sin(ang)
    cos_t = jnp.concatenate([cos, cos, cos, cos], axis=-1)
    sin_t = jnp.concatenate([-sin, -sin, sin, sin], axis=-1)
    cos_t = jnp.concatenate([cos_t, jnp.ones((n_ctx, LANES), F32)], axis=0)
    sin_t = jnp.concatenate([sin_t, jnp.zeros((n_ctx, LANES), F32)], axis=0)
    return cos_t, sin_t


def _diff_head_perm():
    half = DIFF_SUB // 2
    perm = []
    for hh in range(DIFF_HEADS):
        base = hh * 2 * DIFF_SUB
        for part in range(2):
            for sub in range(2):
                start = base + sub * DIFF_SUB + part * half
                perm.extend(range(start, start + half))
    return np.asarray(perm)


def kernel(x, c, ctx, c_ctx, ada_w, ada_b, ln1_g, ln1_b, ln2_g, ln2_b, ffn_w_up, ffn_conv_w, ffn_conv_b, ffn_w_down, hyb_w_in, ssd_conv_w, ssd_conv_b, ssd_a_log, ssd_dt_bias, ssd_d, ssd_norm_w, na_rpb, hyb_w_out, diff_w_in, diff_lambda, diff_subln_w, diff_w_out):
    nb, n_lat, d = x.shape
    n_ctx = ctx.shape[1]
    assert d == D_MODEL and n_lat % TM == 0 and n_ctx % TM == 0 and n_lat % (NA_WIN_ROWS * GRID_W) == 0
    t = n_lat + n_ctx
    nt = t // TM
    nxt = n_lat // TM

    h = jnp.concatenate([x, ctx], axis=1)
    cond_rows = -(-(nb + FFN_BPS) // 8) * 8
    cond = jnp.concatenate([c, jnp.broadcast_to(c_ctx[None, :], (FFN_BPS, d)),
                            jnp.zeros((cond_rows - nb - FFN_BPS, d), F32)], axis=0)
    mod_all = _ada_mod(cond, ada_w, ada_b).reshape(DEPTH, cond_rows, 6, d)
    cos_t, sin_t = _rope_tables(n_lat, n_ctx)
    perm = _diff_head_perm()
    na_variants, na_vid = _na_variants(n_lat // GRID_W)
    expand = jnp.asarray(np.kron(np.eye(LANES, SSD_H, dtype=np.float32), np.ones((1, SSD_P), np.float32)))

    for i in range(DEPTH):
        last = i == DEPTH - 1
        j = i // 2
        mod = mod_all[i]
        n_tiles = nxt if last else nt
        row = lambda v: v.reshape(1, -1)
        if i % 2 == 0:
            w_in = hyb_w_in[j]
            o_xbc = SSD_INNER
            o_dt = o_xbc + SSD_CONV_DIM
            o_q = o_dt + 2 * SSD_H
            w_main = jnp.concatenate([w_in[:, o_xbc:o_dt], w_in[:, :SSD_INNER],
                                      w_in[:, o_q:o_q + NA_WIDTH] * NA_DH ** -0.5,
                                      w_in[:, o_q + NA_WIDTH:]], axis=1).astype(BF16)
            w_dt = jnp.zeros((d, 2 * LANES), F32)
            w_dt = w_dt.at[:, :SSD_H].set(w_in[:, o_dt:o_dt + SSD_H])
            w_dt = w_dt.at[:, LANES:LANES + SSD_H].set(w_in[:, o_dt + SSD_H:o_q]).astype(BF16)
            pm, dt, na_qt, na_k, na_vt = _hyb_proj(h, mod, w_main, w_dt, ssd_conv_w[j], row(ssd_conv_b[j]), nxt)

            pad = lambda v: jnp.zeros((2, 1, LANES), F32).at[:, 0, :SSD_H].set(v)
            arow = pad(-jnp.exp(ssd_a_log[j].astype(F32)))
            brow = pad(ssd_dt_bias[j].astype(F32))
            dsk = jnp.repeat(ssd_d[j].astype(F32), SSD_P, axis=-1).reshape(2, 1, SSD_INNER)
            yf, yb = _ssd(pm, dt, arow, brow, dsk, expand, n_lat)
            att = _na(na_qt, na_k, na_vt, _na_bias_table(na_rpb[j], na_variants), na_vid, n_lat)
            w_out = hyb_w_out[j].astype(BF16)
            h = _hyb_out(yf, yb, pm, att, h, mod, row(ssd_norm_w[j]), w_out[:SSD_INNER], w_out[SSD_INNER:],
                         row(ln1_g[i]), row(ln1_b[i]), nxt)
        else:
            lam_init = 0.8 - 0.6 * math.exp(-0.3 * i)
            w_in = diff_w_in[j]
            wq = w_in[:, :d][:, perm] * (DIFF_SUB ** -0.5 * LOG2E)
            wk = w_in[:, d:2 * d][:, perm]
            w_qkv = jnp.concatenate([wq, wk, w_in[:, 2 * d:]], axis=1).astype(BF16)
            d_qt, d_k, d_vt = _diff_proj(h, mod, w_qkv, cos_t, sin_t, nxt)
            subln_b = jnp.broadcast_to(diff_subln_w[j].astype(F32)[:, None], (2 * DIFF_SUB, TM))
            att = _diff_attn(d_qt, d_k, d_vt, diff_lambda[j].astype(F32), subln_b, n_lat, lam_init, n_tiles, TM)
            h = _diff_out(att, h, mod, diff_w_out[j].astype(BF16), row(ln1_g[i]), row(ln1_b[i]), nxt, n_tiles)
        h = _ffn(h, mod, ffn_w_up[i].astype(BF16), ffn_conv_w[i], row(ffn_conv_b[i]),
                 ffn_w_down[i].astype(BF16), row(ln2_g[i]), row(ln2_b[i]), nxt, n_tiles)
    return h
```

```python
import functools
import math

import numpy as np
import jax
import jax.numpy as jnp
from jax import lax
from jax.experimental import pallas as pl
from jax.experimental.pallas import tpu as pltpu

F32 = jnp.float32
BF16 = jnp.bfloat16

D_MODEL = 1024
DEPTH = 4
GRID_W = 64
SSD_P = 64
SSD_H = 16
SSD_G = 4
SSD_R = SSD_H // SSD_G
SSD_N = 128
SSD_INNER = SSD_H * SSD_P
SSD_GN = SSD_G * SSD_N
SSD_CONV_W = 5
SSD_CONV_DIM = SSD_INNER + 2 * SSD_GN
SSD_CHUNK = 128
SSD_BPS = 2
NA_HEADS = 8
NA_DH = 64
NA_WIDTH = NA_HEADS * NA_DH
NA_WIN_ROWS = 8
NA_WIN_COLS = 16
DIFF_HEADS = 8
DIFF_SUB = 64
ROPE_BASE = 10000.0
D_FF = 2816
FFN_CH = 256
FFN_BPS = 2
ALPHA = (2.0 * DEPTH) ** 0.25
LN_EPS = 1e-5
RMS_EPS = 1e-5

LANES = 128
TM = 256
HALO_F32 = 8
HALO_BF16 = 16
VMEM_LIMIT = 56 * 1024 * 1024
HYB_SSD = SSD_CONV_DIM + SSD_INNER
HYB_MAIN = HYB_SSD + 3 * NA_WIDTH
NA_VROWS = NA_DH + HALO_BF16
NA_BLK_ROWS = 4
NA_KEY_BLKS = 3
DIFF_VROWS = 2 * DIFF_SUB
DIFF_KCH = 1024
DIFF_HPS = 4
LOG2E = 1.4426950408889634
SUM_LO = 2.0 ** -90
SUM_HI = 2.0 ** 100
HYB_NCH = 512


def _params(sem, vmem=VMEM_LIMIT):
    return pltpu.CompilerParams(dimension_semantics=sem, vmem_limit_bytes=vmem)


def _resident(shape):
    nd = len(shape)
    return pl.BlockSpec(shape, lambda *_: (0,) * nd, pipeline_mode=pl.Buffered(1))


def _dot(a, b):
    return jnp.dot(a, b, preferred_element_type=F32)


def _dot_nt(a, b):
    return lax.dot_general(a, b, (((1,), (1,)), ((), ())), preferred_element_type=F32)


def _silu(v):
    return v * jax.nn.sigmoid(v)


def _layer_norm(r, g, b):
    mu = jnp.mean(r, axis=-1, keepdims=True)
    xc = r - mu
    var = jnp.mean(xc * xc, axis=-1, keepdims=True)
    return xc * lax.rsqrt(var + LN_EPS) * g + b


def _mod_spec(nxt, nb):
    return pl.BlockSpec((1, 6, D_MODEL), lambda b, i: (jnp.where(i >= nxt, nb, b), 0, 0))


def _ada_kernel(c_ref, w_ref, b_ref, o_ref):
    s = _silu(c_ref[...]).astype(BF16)
    o_ref[0] = _dot(s, w_ref[0].astype(BF16)) + b_ref[0]


def _ada_mod(cond, ada_w, ada_b):
    rows = cond.shape[0]
    n = ada_w.shape[-1]
    tn = n // 4
    return pl.pallas_call(
        _ada_kernel,
        grid=(DEPTH, n // tn),
        in_specs=[pl.BlockSpec((rows, D_MODEL), lambda l, j: (0, 0)),
                  pl.BlockSpec((1, D_MODEL, tn), lambda l, j: (l, 0, j)),
                  pl.BlockSpec((1, 1, tn), lambda l, j: (l, 0, j))],
        out_specs=pl.BlockSpec((1, rows, tn), lambda l, j: (l, 0, j)),
        out_shape=jax.ShapeDtypeStruct((DEPTH, rows, n), F32),
        name="ada_mod",
        compiler_params=_params(("parallel", "parallel")),
    )(cond, ada_w, ada_b.reshape(DEPTH, 1, n))


def _hyb_proj_kernel(h_ref, hp_ref, hn_ref, mod_ref, w_ref, wdt_ref, cw_ref, cb_ref,
                     o_ref, dt_ref, qt_ref, k_ref, vt_ref, ext_ref, *, nxt, nt):
    i = pl.program_id(1)
    m = mod_ref[0]
    shift, scale = m[0:1, :], m[1:2, :]
    seg_start = jnp.logical_or(i == 0, i == nxt)
    seg_end = jnp.logical_or(i == nxt - 1, i == nt - 1)
    um = h_ref[0] * (1.0 + scale) + shift
    u = um.astype(BF16)
    up = jnp.where(seg_start, 0.0, hp_ref[0] * (1.0 + scale) + shift)
    un = jnp.where(seg_end, 0.0, hn_ref[0] * (1.0 + scale) + shift)
    uext = jnp.concatenate([up, um, un], axis=0).astype(BF16)

    def xbc_proj(c):
        ext_ref[c % 2] = _dot(uext, w_ref[:, c * HYB_NCH:(c + 1) * HYB_NCH])

    n_ch = SSD_CONV_DIM // HYB_NCH
    xbc_proj(0)
    for c in range(n_ch):
        n0 = c * HYB_NCH
        if c + 1 < n_ch:
            xbc_proj(c + 1)
        acc = cb_ref[:, n0:n0 + HYB_NCH]
        for k in range(SSD_CONV_W):
            acc = acc + cw_ref[k:k + 1, n0:n0 + HYB_NCH] * ext_ref[c % 2, pl.ds(HALO_F32 - SSD_CONV_W // 2 + k, TM), :]
        o_ref[0, :, n0:n0 + HYB_NCH] = _silu(acc).astype(BF16)
    for n0 in range(SSD_CONV_DIM, HYB_SSD, HYB_NCH):
        o_ref[0, :, n0:n0 + HYB_NCH] = _dot(u, w_ref[:, n0:n0 + HYB_NCH]).astype(BF16)
    dt_ref[0] = _dot(u, wdt_ref[...])
    qt_ref[0] = _dot(u, w_ref[:, HYB_SSD:HYB_SSD + NA_WIDTH]).T.astype(BF16)
    k_ref[0] = _dot(u, w_ref[:, HYB_SSD + NA_WIDTH:HYB_SSD + 2 * NA_WIDTH]).astype(BF16)
    v_t = _dot(u, w_ref[:, HYB_SSD + 2 * NA_WIDTH:HYB_MAIN]).T
    for hh in range(NA_HEADS):
        vt_ref[0, hh, 0:NA_DH, :] = v_t[hh * NA_DH:(hh + 1) * NA_DH, :].astype(BF16)
        vt_ref[0, hh, NA_DH:NA_VROWS, :] = jnp.ones((NA_VROWS - NA_DH, TM), BF16)


def _hyb_proj(h, mod, w_main, w_dt, conv_w, conv_b, nxt):
    nb, t, _ = h.shape
    nt = t // TM
    hb = TM // HALO_F32
    last_hb = t // HALO_F32 - 1
    return pl.pallas_call(
        functools.partial(_hyb_proj_kernel, nxt=nxt, nt=nt),
        grid=(nb, nt),
        in_specs=[pl.BlockSpec((1, TM, D_MODEL), lambda b, i: (b, i, 0)),
                  pl.BlockSpec((1, HALO_F32, D_MODEL), lambda b, i: (b, jnp.maximum(i * hb - 1, 0), 0)),
                  pl.BlockSpec((1, HALO_F32, D_MODEL), lambda b, i: (b, jnp.minimum((i + 1) * hb, last_hb), 0)),
                  _mod_spec(nxt, nb),
                  _resident(w_main.shape),
                  _resident(w_dt.shape),
                  _resident(conv_w.shape),
                  _resident(conv_b.shape)],
        out_specs=[pl.BlockSpec((1, TM, HYB_SSD), lambda b, i: (b, i, 0)),
                   pl.BlockSpec((1, TM, 2 * LANES), lambda b, i: (b, i, 0)),
                   pl.BlockSpec((1, NA_WIDTH, TM), lambda b, i: (b, 0, i)),
                   pl.BlockSpec((1, TM, NA_WIDTH), lambda b, i: (b, i, 0)),
                   pl.BlockSpec((1, NA_HEADS, NA_VROWS, TM), lambda b, i: (b, 0, 0, i))],
        out_shape=[jax.ShapeDtypeStruct((nb, t, HYB_SSD), BF16),
                   jax.ShapeDtypeStruct((nb, t, 2 * LANES), F32),
                   jax.ShapeDtypeStruct((nb, NA_WIDTH, t), BF16),
                   jax.ShapeDtypeStruct((nb, t, NA_WIDTH), BF16),
                   jax.ShapeDtypeStruct((nb, NA_HEADS, NA_VROWS, t), BF16)],
        scratch_shapes=[pltpu.VMEM((2, TM + 2 * HALO_F32, HYB_NCH), F32)],
        name="hyb_proj",
        compiler_params=_params(("parallel", "parallel")),
    )(h, h, h, mod, w_main, w_dt, conv_w, conv_b)


def _ssd_kernel(uf_ref, ub_ref, dtf_ref, dtb_ref, arow_ref, brow_ref, dsk_ref, e_ref, yf_ref, yb_ref, st_ref):
    j = pl.program_id(1)

    @pl.when(j == 0)
    def _():
        st_ref[...] = jnp.zeros_like(st_ref)

    for bi in range(SSD_BPS):
        _ssd_chunk(0, bi, uf_ref, dtf_ref, arow_ref, brow_ref, dsk_ref, e_ref, yf_ref, st_ref)
        _ssd_chunk(1, bi, ub_ref, dtb_ref, arow_ref, brow_ref, dsk_ref, e_ref, yb_ref, st_ref)


def _ssd_chunk(d, bi, u_ref, dt_ref, arow_ref, brow_ref, dsk_ref, e_ref, y_ref, st_ref):
    q = SSD_CHUNK
    u = u_ref[bi].astype(F32)
    xs = u[:, :SSD_INNER]

    dtr = dt_ref[bi] + brow_ref[d]
    dtv = jnp.maximum(dtr, 0.0) + jnp.log1p(jnp.exp(-jnp.abs(dtr)))
    adt = dtv * arow_ref[d]
    ri = lax.broadcasted_iota(jnp.int32, (q, q), 0)
    ci = lax.broadcasted_iota(jnp.int32, (q, q), 1)
    tri = (ri >= ci) if d == 0 else (ri <= ci)
    cs = jnp.dot(tri.astype(F32), adt, precision=lax.Precision.HIGHEST, preferred_element_type=F32)
    cs_t = cs.T
    dt_t = dtv.T
    last = q - 1 if d == 0 else 0
    tot = cs[last:last + 1, :]
    tot_t = cs_t[:, last:last + 1]
    w_t = jnp.exp(tot_t - cs_t) * dt_t
    dec_row = jnp.dot(jnp.broadcast_to(jnp.exp(tot), (8, LANES)), e_ref[...],
                      precision=lax.Precision.HIGHEST, preferred_element_type=F32)[0:1]

    lane = lax.broadcasted_iota(jnp.int32, (q, LANES), 1)
    lo = lane < SSD_P
    dsk = dsk_ref[d]
    for g in range(SSD_G):
        bm = u[:, SSD_INNER + g * SSD_N:SSD_INNER + (g + 1) * SSD_N]
        cm = u[:, SSD_INNER + SSD_GN + g * SSD_N:SSD_INNER + SSD_GN + (g + 1) * SSD_N]
        cmb = cm.astype(BF16)
        cb = _dot_nt(cmb, bm.astype(BF16))
        bm_t = bm.T
        s_prev = st_ref[bi, d, g]
        y_off = _dot(cmb, s_prev.astype(BF16))
        s_parts = []
        for pr in range(SSD_R // 2):
            col0 = g * SSD_R * SSD_P + pr * LANES
            xs_pair = xs[:, col0:col0 + LANES]
            xsb = xs_pair.astype(BF16)
            yd, sc, colbs = [], [], []
            for sub in range(2):
                h = g * SSD_R + pr * 2 + sub
                colb = jnp.broadcast_to(cs[:, h:h + 1], (q, q))
                decay = jnp.exp(jnp.where(tri, colb - cs_t[h:h + 1, :], -jnp.inf))
                gmat = (cb * decay * dt_t[h:h + 1, :]).astype(BF16)
                yd.append(_dot(gmat, xsb))
                sc.append(_dot((bm_t * w_t[h:h + 1, :]).astype(BF16), xsb))
                colbs.append(colb)
            y_diag = jnp.where(lo, yd[0], yd[1])
            e_col = jnp.exp(jnp.where(lo, colbs[0], colbs[1]))
            y_pair = y_diag + y_off[:, pr * LANES:(pr + 1) * LANES] * e_col + dsk[:, col0:col0 + LANES] * xs_pair
            y_ref[bi, :, col0:col0 + LANES] = y_pair.astype(BF16)
            s_parts.append(jnp.where(lo, sc[0], sc[1]))
        g0 = g * SSD_R * SSD_P
        st_ref[bi, d, g] = s_prev * dec_row[:, g0:g0 + SSD_R * SSD_P] + jnp.concatenate(s_parts, axis=1)


def _ssd(pm, dt, arow, brow, dsk, expand, n_lat):
    nb, t, _ = pm.shape
    q = SSD_CHUNK
    nc = t // q
    nxc = n_lat // q

    def cf(j):
        return (j + nxc) % nc

    def cbk(j):
        return nc - 1 - j

    y_shape = jax.ShapeDtypeStruct((nb, t, SSD_INNER), BF16)
    bps = SSD_BPS
    assert nb % bps == 0
    return pl.pallas_call(
        _ssd_kernel,
        grid=(nb // bps, nc),
        in_specs=[pl.BlockSpec((bps, q, SSD_CONV_DIM), lambda b, j: (b, cf(j), 0)),
                  pl.BlockSpec((bps, q, SSD_CONV_DIM), lambda b, j: (b, cbk(j), 0)),
                  pl.BlockSpec((bps, q, LANES), lambda b, j: (b, cf(j), 0)),
                  pl.BlockSpec((bps, q, LANES), lambda b, j: (b, cbk(j), 1)),
                  _resident(arow.shape), _resident(brow.shape), _resident(dsk.shape), _resident(expand.shape)],
        out_specs=[pl.BlockSpec((bps, q, SSD_INNER), lambda b, j: (b, cf(j), 0)),
                   pl.BlockSpec((bps, q, SSD_INNER), lambda b, j: (b, cbk(j), 0))],
        out_shape=[y_shape, y_shape],
        scratch_shapes=[pltpu.VMEM((bps, 2, SSD_G, SSD_N, SSD_R * SSD_P), F32)],
        name="ssd_scan",
        compiler_params=_params(("parallel", "arbitrary")),
    )(pm, pm, dt, dt, arow, brow, dsk, expand)


def _na_kernel(qt_ref, k0_ref, k1_ref, k2_ref, kc_ref, v0_ref, v1_ref, v2_ref, vc_ref, bias_ref, o_ref, *, nblk):
    i = pl.program_id(1)
    nq = NA_BLK_ROWS * GRID_W
    zeros = jnp.zeros((NA_DH, nq), BF16)
    local = ((k0_ref, v0_ref), (k1_ref, v1_ref), (k2_ref, v2_ref))

    def run(with_local):
        def scores(hh):
            sl = slice((hh // 2) * LANES, (hh // 2 + 1) * LANES)
            qh = qt_ref[0, hh * NA_DH:(hh + 1) * NA_DH, :]
            qm = jnp.concatenate([qh, zeros] if hh % 2 == 0 else [zeros, qh], axis=0)
            sc = []
            if with_local:
                for jb, (kr, _) in enumerate(local):
                    sc.append(_dot(kr[0, :, sl], qm) + bias_ref[0, hh, jb * nq:(jb + 1) * nq, :])
            sc.append(_dot(kc_ref[0, :, sl], qm))
            return sc

        outs = []
        s_next = scores(0)
        for hh in range(NA_HEADS):
            s_cur = s_next
            if hh + 1 < NA_HEADS:
                s_next = scores(hh + 1)
            vals = ([vr[0, hh] for _, vr in local] if with_local else []) + [vc_ref[0, hh]]
            mx = functools.reduce(jnp.maximum, [jnp.max(s, axis=0, keepdims=True) for s in s_cur])
            o_t = functools.reduce(
                lambda a, b: a + b, [_dot(v, jnp.exp(s - mx).astype(BF16)) for s, v in zip(s_cur, vals)])
            outs.append(o_t[:NA_DH, :] * (1.0 / o_t[NA_DH:NA_DH + 1, :]))
        o_ref[0] = jnp.concatenate(outs, axis=0).T.astype(BF16)

    @pl.when(i < nblk)
    def _():
        run(True)

    @pl.when(i >= nblk)
    def _():
        run(False)


def _na_variants(rows):
    nblk = rows // NA_BLK_ROWS
    assert rows % NA_BLK_ROWS == 0 and nblk >= NA_KEY_BLKS
    keys, vid = [], []
    for b in range(nblk):
        ws = int(np.clip(b - 1, 0, nblk - NA_KEY_BLKS)) * NA_BLK_ROWS
        key = []
        for a in range(NA_BLK_ROWS):
            r = b * NA_BLK_ROWS + a
            r0 = int(np.clip(r - NA_WIN_ROWS // 2, 0, rows - NA_WIN_ROWS))
            assert 0 <= r0 - ws and r0 - ws + NA_WIN_ROWS <= NA_KEY_BLKS * NA_BLK_ROWS
            key.append((r0 - ws, r0 - r))
        key = tuple(key)
        if key not in keys:
            keys.append(key)
        vid.append(keys.index(key))
    return keys, vid


def _na_bias_table(rpb, variants):
    c = np.arange(GRID_W)[:, None]
    kc = np.arange(GRID_W)[None, :]
    c0 = np.clip(c - NA_WIN_COLS // 2, 0, GRID_W - NA_WIN_COLS)
    valid_c = (kc >= c0) & (kc < c0 + NA_WIN_COLS)
    dc = kc - c + NA_WIN_COLS - 1
    onehot = ((dc[None] == np.arange(2 * NA_WIN_COLS - 1)[:, None, None]) & valid_c[None]).astype(np.float32)
    tz = jnp.einsum('hrd,dck->hrck', rpb.astype(F32), jnp.asarray(onehot), precision=lax.Precision.HIGHEST)
    nkr = NA_KEY_BLKS * NA_BLK_ROWS
    ndr = 2 * NA_WIN_ROWS - 1
    tabs = []
    for key in variants:
        sel = np.zeros((NA_BLK_ROWS, nkr, ndr), np.float32)
        valid_r = np.zeros((NA_BLK_ROWS, nkr), bool)
        for a, (off, e) in enumerate(key):
            for w in range(NA_WIN_ROWS):
                sel[a, off + w, e + NA_WIN_ROWS - 1 + w] = 1.0
                valid_r[a, off + w] = True
        tab = jnp.einsum('ajr,hrck->hajck', jnp.asarray(sel), tz, precision=lax.Precision.HIGHEST)
        ok = jnp.asarray(valid_r[None, :, :, None, None] & valid_c[None, None, None, :, :])
        tab = jnp.where(ok, tab, -jnp.inf)
        tabs.append(tab.transpose(0, 2, 4, 1, 3).reshape(NA_HEADS, nkr * GRID_W, NA_BLK_ROWS * GRID_W))
    return jnp.stack(tabs)


def _na(qt, k, vt, bias, vid, n_lat):
    nb, t, _ = k.shape
    n_ctx = t - n_lat
    nq = NA_BLK_ROWS * GRID_W
    nblk = n_lat // nq
    assert n_ctx % nq == 0 and n_lat % n_ctx == 0
    steps = t // nq
    cblk = n_lat // n_ctx

    def wb(i):
        return jnp.clip(i - 1, 0, nblk - NA_KEY_BLKS)

    def variant(i):
        v = jnp.int32(0)
        for blk in range(1, nblk):
            v = jnp.where(i == blk, vid[blk], v)
        return v

    kspec = [pl.BlockSpec((1, nq, NA_WIDTH), functools.partial(lambda b, i, j: (b, wb(i) + j, 0), j=j))
             for j in range(NA_KEY_BLKS)]
    vspec = [pl.BlockSpec((1, NA_HEADS, NA_VROWS, nq), functools.partial(lambda b, i, j: (b, 0, 0, wb(i) + j), j=j))
             for j in range(NA_KEY_BLKS)]
    return pl.pallas_call(
        functools.partial(_na_kernel, nblk=nblk),
        grid=(nb, steps),
        in_specs=([pl.BlockSpec((1, NA_WIDTH, nq), lambda b, i: (b, 0, i))] + kspec
                  + [pl.BlockSpec((1, n_ctx, NA_WIDTH), lambda b, i: (b, cblk, 0))] + vspec
                  + [pl.BlockSpec((1, NA_HEADS, NA_VROWS, n_ctx), lambda b, i: (b, 0, 0, cblk)),
                     pl.BlockSpec((1,) + bias.shape[1:], lambda b, i: (variant(i), 0, 0, 0))]),
        out_specs=pl.BlockSpec((1, nq, NA_WIDTH), lambda b, i: (b, i, 0)),
        out_shape=jax.ShapeDtypeStruct((nb, t, NA_WIDTH), BF16),
        name="na_attn",
        compiler_params=_params(("parallel", "arbitrary")),
    )(qt, k, k, k, k, vt, vt, vt, vt, bias)


def _hyb_out_kernel(yf_ref, yb_ref, z_ref, a_ref, h_ref, mod_ref, nw_ref, wy_ref, wa_ref, g_ref, b_ref, o_ref):
    y = (yf_ref[0].astype(F32) + yb_ref[0].astype(F32)) * _silu(z_ref[0].astype(F32))
    y = y * lax.rsqrt(jnp.mean(y * y, axis=-1, keepdims=True) + RMS_EPS) * nw_ref[...]
    o = _dot(y.astype(BF16), wy_ref[...]) + _dot(a_ref[0], wa_ref[...])
    gate = mod_ref[0][2:3, :]
    o_ref[0] = _layer_norm(ALPHA * h_ref[0] + gate * o, g_ref[...], b_ref[...])


def _hyb_out(yf, yb, pm, att, h, mod, norm_w, wy, wa, ln_g, ln_b, nxt):
    nb, t, _ = h.shape
    nt = t // TM
    zcol = SSD_CONV_DIM // SSD_INNER
    return pl.pallas_call(
        _hyb_out_kernel,
        grid=(nb, nt),
        in_specs=[pl.BlockSpec((1, TM, SSD_INNER), lambda b, i: (b, i, 0)),
                  pl.BlockSpec((1, TM, SSD_INNER), lambda b, i: (b, i, 0)),
                  pl.BlockSpec((1, TM, SSD_INNER), lambda b, i: (b, i, zcol)),
                  pl.BlockSpec((1, TM, NA_WIDTH), lambda b, i: (b, i, 0)),
                  pl.BlockSpec((1, TM, D_MODEL), lambda b, i: (b, i, 0)),
                  _mod_spec(nxt, nb),
                  _resident(norm_w.shape), _resident(wy.shape), _resident(wa.shape),
                  _resident(ln_g.shape), _resident(ln_b.shape)],
        out_specs=pl.BlockSpec((1, TM, D_MODEL), lambda b, i: (b, i, 0)),
        out_shape=jax.ShapeDtypeStruct((nb, t, D_MODEL), F32),
        name="hyb_out",
        compiler_params=_params(("parallel", "parallel")),
    )(yf, yb, pm, att, h, mod, norm_w, wy, wa, ln_g, ln_b)


def _ffn_kernel(hm_ref, hp_ref, hn_ref, mod_ref, wup_ref, cw_ref, cb_ref, wdn_ref, g_ref, b_ref, p_ref, o_ref,
                *scratch, nxt, nt):
    per_tile = len(scratch) // FFN_BPS
    hv_ref = [scratch[bi * per_tile:bi * per_tile + 2] for bi in range(FFN_BPS)]
    hg_ref = [scratch[bi * per_tile + 2:bi * per_tile + 4] for bi in range(FFN_BPS)]
    acc_ref = [scratch[bi * per_tile + 4] for bi in range(FFN_BPS)]
    perm_ref = [scratch[bi * per_tile + 5] for bi in range(FFN_BPS)]
    i = pl.program_id(1)
    seg_start = jnp.logical_or(i == 0, i == nxt)
    seg_end = jnp.logical_or(i == nxt - 1, i == nt - 1)
    sl = HALO_F32
    ngrp = TM // sl
    ncb = D_MODEL // LANES
    sub_d = lax.broadcasted_iota(jnp.int32, (2 * sl, D_MODEL), 0)
    sub_c = lax.broadcasted_iota(jnp.int32, (sl, FFN_CH), 0)

    def load_tile(bi):
        m = mod_ref[bi]
        shift, scale = m[3:4, :], m[4:5, :]
        hm = hm_ref[bi]
        u_perm = _dot(p_ref[...], (hm * (1.0 + scale) + shift).astype(BF16)).astype(BF16)
        prev_row = jnp.where(seg_start, 0.0, hp_ref[bi, sl - 1:sl, :] * (1.0 + scale) + shift)
        next_row = jnp.where(seg_end, 0.0, hn_ref[bi, 0:1, :] * (1.0 + scale) + shift)
        halo = jnp.where(sub_d == 0, prev_row, jnp.where(sub_d == 1, next_row, 0.0))
        uext = jnp.concatenate([u_perm, halo.astype(BF16)], axis=0)
        return hm, uext

    def conv(ref, col0):
        w = cw_ref[:, col0:col0 + FFN_CH]
        before_first = jnp.where(sub_c == 0, ref[TM:TM + 1, :], pltpu.roll(ref[TM - sl:TM, :], 1, axis=0))
        after_last = jnp.where(sub_c == sl - 1, ref[TM + 1:TM + 2, :], pltpu.roll(ref[0:sl, :], sl - 1, axis=0))
        prev = jnp.concatenate([before_first, ref[0:TM - sl, :]], axis=0)
        nxt_ = jnp.concatenate([ref[sl:TM, :], after_last], axis=0)
        return w[0:1, :] * prev + w[1:2, :] * ref[0:TM, :] + w[2:3, :] * nxt_ + cb_ref[:, col0:col0 + FFN_CH]

    n_ch = D_FF // FFN_CH

    def up_proj(bi, uext, c):
        hv_ref[bi][c % 2][...] = _dot(uext, wup_ref[:, c * FFN_CH:(c + 1) * FFN_CH])
        hg_ref[bi][c % 2][...] = _dot(uext, wup_ref[:, D_FF + c * FFN_CH:D_FF + (c + 1) * FFN_CH])

    def chunk(bi, uext, c):
        v0 = c * FFN_CH
        if c + 1 < n_ch:
            up_proj(bi, uext, c + 1)
        act = (_silu(conv(hg_ref[bi][c % 2], D_FF + v0)) * conv(hv_ref[bi][c % 2], v0)).astype(BF16)
        contrib = _dot(act, wdn_ref[v0:v0 + FFN_CH, :])
        if c == 0:
            acc_ref[bi][...] = contrib
        else:
            acc_ref[bi][...] += contrib

    def finish(bi, hm):
        gate = mod_ref[bi][5:6, :]
        acc = acc_ref[bi][...]
        for cb in range(ncb):
            for a in range(ngrp):
                perm_ref[bi][cb, pl.ds(a, sl, stride=ngrp), :] = acc[a * sl:(a + 1) * sl, cb * LANES:(cb + 1) * LANES]
        f = jnp.concatenate([perm_ref[bi][cb] for cb in range(ncb)], axis=1)
        o_ref[bi] = _layer_norm(ALPHA * hm + gate * f, g_ref[...], b_ref[...])

    tiles = [load_tile(bi) for bi in range(FFN_BPS)]
    for bi, (hm, uext) in enumerate(tiles):
        up_proj(bi, uext, 0)
        for c in range(n_ch):
            chunk(bi, uext, c)
        finish(bi, hm)


def _ffn(h, mod, w_up, conv_w, conv_b, w_dn, ln_g, ln_b, nxt, n_out_tiles):
    nb, t, _ = h.shape
    nt = t // TM
    hb = TM // HALO_F32
    last_hb = t // HALO_F32 - 1
    bps = FFN_BPS
    assert nb % bps == 0
    ngrp = TM // HALO_F32
    src = (np.arange(TM) % HALO_F32) * ngrp + np.arange(TM) // HALO_F32
    perm = jnp.asarray(np.eye(TM, dtype=np.float32)[src], BF16)
    mod_spec = pl.BlockSpec((bps, 6, D_MODEL), lambda b, i: (jnp.where(i >= nxt, nb // bps, b), 0, 0))
    return pl.pallas_call(
        functools.partial(_ffn_kernel, nxt=nxt, nt=nt),
        grid=(nb // bps, n_out_tiles),
        in_specs=[pl.BlockSpec((bps, TM, D_MODEL), lambda b, i: (b, i, 0)),
                  pl.BlockSpec((bps, HALO_F32, D_MODEL), lambda b, i: (b, jnp.maximum(i * hb - 1, 0), 0)),
                  pl.BlockSpec((bps, HALO_F32, D_MODEL), lambda b, i: (b, jnp.minimum((i + 1) * hb, last_hb), 0)),
                  mod_spec,
                  _resident(w_up.shape), _resident(conv_w.shape), _resident(conv_b.shape),
                  _resident(w_dn.shape), _resident(ln_g.shape), _resident(ln_b.shape), _resident(perm.shape)],
        out_specs=pl.BlockSpec((bps, TM, D_MODEL), lambda b, i: (b, i, 0)),
        out_shape=jax.ShapeDtypeStruct((nb, n_out_tiles * TM, D_MODEL), F32),
        scratch_shapes=bps * ([pltpu.VMEM((TM + 2 * HALO_F32, FFN_CH), F32)] * 4
                              + [pltpu.VMEM((TM, D_MODEL), F32),
                                 pltpu.VMEM((D_MODEL // LANES, TM, LANES), F32)]),
        name="conv_ffn",
        compiler_params=_params(("parallel", "parallel")),
    )(h, h, h, mod, w_up, conv_w, conv_b, w_dn, ln_g, ln_b, perm)


def _diff_proj_kernel(h_ref, mod_ref, w_ref, cos_ref, sin_ref, cost_ref, sint_ref, qt_ref, k_ref, vt_ref):
    m = mod_ref[0]
    u = (h_ref[0] * (1.0 + m[1:2, :]) + m[0:1, :]).astype(BF16)
    half = LANES // 2
    for n0 in range(0, D_MODEL, HYB_NCH):
        y_t = _dot(u, w_ref[:, n0:n0 + HYB_NCH]).T
        for p0 in range(0, HYB_NCH, LANES):
            blk = y_t[p0:p0 + LANES, :]
            partner = jnp.concatenate([blk[half:, :], blk[:half, :]], axis=0)
            qt_ref[0, n0 + p0:n0 + p0 + LANES, :] = (blk * cost_ref[...] + partner * sint_ref[...]).astype(BF16)
    for n0 in range(0, D_MODEL, HYB_NCH):
        y = _dot(u, w_ref[:, D_MODEL + n0:D_MODEL + n0 + HYB_NCH])
        parts = []
        for p0 in range(0, HYB_NCH, LANES):
            yp = y[:, p0:p0 + LANES]
            parts.append(yp * cos_ref[...] + pltpu.roll(yp, half, axis=1) * sin_ref[...])
        k_ref[0, :, n0:n0 + HYB_NCH] = jnp.concatenate(parts, axis=1).astype(BF16)
    hd = 2 * DIFF_SUB
    for n0 in range(0, D_MODEL, HYB_NCH):
        y_t = _dot(u, w_ref[:, 2 * D_MODEL + n0:2 * D_MODEL + n0 + HYB_NCH]).T
        for p0 in range(0, HYB_NCH, hd):
            vt_ref[0, (n0 + p0) // hd, 0:hd, :] = y_t[p0:p0 + hd, :].astype(BF16)


def _diff_proj(h, mod, w, cos_t, sin_t, nxt):
    nb, t, _ = h.shape
    nt = t // TM
    return pl.pallas_call(
        _diff_proj_kernel,
        grid=(nb, nt),
        in_specs=[pl.BlockSpec((1, TM, D_MODEL), lambda b, i: (b, i, 0)),
                  _mod_spec(nxt, nb),
                  _resident(w.shape),
                  pl.BlockSpec((TM, LANES), lambda b, i: (i, 0)),
                  pl.BlockSpec((TM, LANES), lambda b, i: (i, 0)),
                  pl.BlockSpec((LANES, TM), lambda b, i: (0, i)),
                  pl.BlockSpec((LANES, TM), lambda b, i: (0, i))],
        out_specs=[pl.BlockSpec((1, D_MODEL, TM), lambda b, i: (b, 0, i)),
                   pl.BlockSpec((1, TM, D_MODEL), lambda b, i: (b, i, 0)),
                   pl.BlockSpec((1, DIFF_HEADS, DIFF_VROWS, TM), lambda b, i: (b, 0, 0, i))],
        out_shape=[jax.ShapeDtypeStruct((nb, D_MODEL, t), BF16),
                   jax.ShapeDtypeStruct((nb, t, D_MODEL), BF16),
                   jax.ShapeDtypeStruct((nb, DIFF_HEADS, DIFF_VROWS, t), BF16)],
        name="diff_proj",
        compiler_params=_params(("parallel", "parallel")),
    )(h, mod, w, cos_t, sin_t, cos_t.T, sin_t.T)


def _diff_attn_kernel(qt_ref, k_ref, vt_ref, lam_ref, sw_ref, o_ref, kmax_ref, *, n_lat, n_ctx, nxt, lam_init):
    i = pl.program_id(2)
    lp = lam_ref[...]
    lam = (jnp.exp(jnp.sum(lp[0:1, :] * lp[1:2, :], axis=1, keepdims=True))
           - jnp.exp(jnp.sum(lp[2:3, :] * lp[3:4, :], axis=1, keepdims=True)) + lam_init)
    hd = 2 * DIFF_SUB
    tq = qt_ref.shape[-1]
    row = lax.broadcasted_iota(jnp.int32, (hd, tq), 0)
    sub0 = (row % DIFF_SUB) < (DIFF_SUB // 2)
    zero = jnp.zeros((hd, tq), BF16)
    heads = range(DIFF_HPS)

    @pl.when(i == 0)
    def _():
        for hh in heads:
            kf = k_ref[0, :, hh * hd:(hh + 1) * hd].astype(F32)
            kmax_ref[hh] = jnp.sqrt(jnp.max(jnp.sum(kf * kf, axis=1, keepdims=True), axis=0, keepdims=True))

    qms, bounds = [], []
    for hh in heads:
        qt = qt_ref[0, hh * hd:(hh + 1) * hd, :]
        qms.append((jnp.where(sub0, qt, zero), jnp.where(sub0, zero, qt)))
        bounds.append([jnp.sqrt(jnp.sum(jnp.square(qm.astype(F32)), axis=0, keepdims=True)) * kmax_ref[hh]
                       for qm in qms[hh]])

    def finish(hh, acc, den):
        outs = [a * (1.0 / d) for a, d in zip(acc, den)]
        o = outs[0] - lam * outs[1]
        o = o * lax.rsqrt(jnp.mean(o * o, axis=0, keepdims=True) + RMS_EPS) * sw_ref[...] * (1.0 - lam_init)
        o_ref[0, :, hh * hd:(hh + 1) * hd] = o.T.astype(BF16)

    def run(k0, nk):
        chunks = [(k0 + c0, min(DIFF_KCH, nk - c0)) for c0 in range(0, nk, DIFF_KCH)]

        def attend(hh, shifted):
            def scores(ci):
                c0, n = chunks[ci]
                return [_dot(k_ref[0, c0:c0 + n, hh * hd:(hh + 1) * hd], qm) for qm in qms[hh]]

            s_next = scores(0)
            mx = [None, None]
            acc = [None, None]
            den = [None, None]
            for ci, (c0, n) in enumerate(chunks):
                s_cur = s_next
                if ci + 1 < len(chunks):
                    s_next = scores(ci + 1)
                vv = vt_ref[0, hh, :, c0:c0 + n]
                for sub in range(2):
                    if shifted:
                        p = jnp.exp2(s_cur[sub] - bounds[hh][sub])
                        pv = _dot(vv, p.astype(BF16))
                        ps = jnp.sum(p, axis=0, keepdims=True)
                        acc[sub] = pv if ci == 0 else acc[sub] + pv
                        den[sub] = ps if ci == 0 else den[sub] + ps
                    else:
                        cm = jnp.max(s_cur[sub], axis=0, keepdims=True)
                        m_new = cm if ci == 0 else jnp.maximum(mx[sub], cm)
                        p = jnp.exp2(s_cur[sub] - m_new)
                        pv = _dot(vv, p.astype(BF16))
                        ps = jnp.sum(p, axis=0, keepdims=True)
                        if ci > 0:
                            alpha = jnp.exp2(mx[sub] - m_new)
                            pv, ps = acc[sub] * alpha + pv, den[sub] * alpha + ps
                        acc[sub], den[sub], mx[sub] = pv, ps, m_new
            return acc, den

        oks = []
        for hh in heads:
            acc, den = attend(hh, True)
            sums = jnp.concatenate(den, axis=0)
            oks.append(jnp.logical_and(jnp.min(sums) > SUM_LO, jnp.max(sums) < SUM_HI))
            finish(hh, acc, den)

        for hh in heads:
            @pl.when(jnp.logical_not(oks[hh]))
            def _(hh=hh):
                finish(hh, *attend(hh, False))

    @pl.when(i < nxt)
    def _():
        run(0, n_lat + n_ctx)

    @pl.when(i >= nxt)
    def _():
        run(n_lat, n_ctx)


def _diff_attn(qt, k, vt, lam_p, subln_b, n_lat, lam_init, n_q_tiles, tq):
    nb, t, _ = k.shape
    n_ctx = t - n_lat
    nxt = n_lat // tq
    hd = 2 * DIFF_SUB
    return pl.pallas_call(
        functools.partial(_diff_attn_kernel, n_lat=n_lat, n_ctx=n_ctx, nxt=nxt, lam_init=lam_init),
        grid=(nb, DIFF_HEADS // DIFF_HPS, n_q_tiles),
        in_specs=[pl.BlockSpec((1, DIFF_HPS * hd, tq), lambda b, hh, i: (b, hh, i)),
                  pl.BlockSpec((1, t, DIFF_HPS * hd), lambda b, hh, i: (b, 0, hh)),
                  pl.BlockSpec((1, DIFF_HPS, DIFF_VROWS, t), lambda b, hh, i: (b, hh, 0, 0)),
                  _resident(lam_p.shape), _resident(subln_b.shape)],
        out_specs=pl.BlockSpec((1, tq, DIFF_HPS * hd), lambda b, hh, i: (b, i, hh)),
        out_shape=jax.ShapeDtypeStruct((nb, n_q_tiles * tq, D_MODEL), BF16),
        scratch_shapes=[pltpu.VMEM((DIFF_HPS, 1, 1), F32)],
        name="diff_attn",
        compiler_params=_params(("parallel", "parallel", "arbitrary")),
    )(qt, k, vt, lam_p, subln_b)


def _diff_out_kernel(a_ref, h_ref, mod_ref, w_ref, g_ref, b_ref, o_ref):
    o = _dot(a_ref[0], w_ref[...])
    gate = mod_ref[0][2:3, :]
    o_ref[0] = _layer_norm(ALPHA * h_ref[0] + gate * o, g_ref[...], b_ref[...])


def _diff_out(att, h, mod, w, ln_g, ln_b, nxt, n_tiles):
    nb, t, _ = h.shape
    return pl.pallas_call(
        _diff_out_kernel,
        grid=(nb, n_tiles),
        in_specs=[pl.BlockSpec((1, TM, D_MODEL), lambda b, i: (b, i, 0)),
                  pl.BlockSpec((1, TM, D_MODEL), lambda b, i: (b, i, 0)),
                  _mod_spec(nxt, nb),
                  _resident(w.shape), _resident(ln_g.shape), _resident(ln_b.shape)],
        out_specs=pl.BlockSpec((1, TM, D_MODEL), lambda b, i: (b, i, 0)),
        out_shape=jax.ShapeDtypeStruct((nb, n_tiles * TM, D_MODEL), F32),
        name="diff_out",
        compiler_params=_params(("parallel", "parallel")),
    )(att, h, mod, w, ln_g, ln_b)


def _rope_tables(n_lat, n_ctx):
    tok = np.arange(n_lat)
    row = (tok // GRID_W).astype(np.float32)
    col = (tok % GRID_W).astype(np.float32)
    n_freq = DIFF_SUB // 4
    inv = jnp.asarray(ROPE_BASE, F32) ** (-jnp.arange(n_freq, dtype=F32) / n_freq)
    ang = jnp.concatenate([jnp.asarray(row)[:, None] * inv, jnp.asarray(col)[:, None] * inv], axis=-1)
    cos = jnp.cos(ang)
    sin = jnp.sin(ang)
    cos_t = jnp.concatenate([cos, cos, cos, cos], axis=-1)
    sin_t = jnp.concatenate([-sin, -sin, sin, sin], axis=-1)
    cos_t = jnp.concatenate([cos_t, jnp.ones((n_ctx, LANES), F32)], axis=0)
    sin_t = jnp.concatenate([sin_t, jnp.zeros((n_ctx, LANES), F32)], axis=0)
    return cos_t, sin_t


def _diff_head_perm():
    half = DIFF_SUB // 2
    perm = []
    for hh in range(DIFF_HEADS):
        base = hh * 2 * DIFF_SUB
        for part in range(2):
            for sub in range(2):
                start = base + sub * DIFF_SUB + part * half
                perm.extend(range(start, start + half))
    return np.asarray(perm)


def kernel(x, c, ctx, c_ctx, ada_w, ada_b, ln1_g, ln1_b, ln2_g, ln2_b, ffn_w_up, ffn_conv_w, ffn_conv_b, ffn_w_down, hyb_w_in, ssd_conv_w, ssd_conv_b, ssd_a_log, ssd_dt_bias, ssd_d, ssd_norm_w, na_rpb, hyb_w_out, diff_w_in, diff_lambda, diff_subln_w, diff_w_out):
    nb, n_lat, d = x.shape
    n_ctx = ctx.shape[1]
    assert d == D_MODEL and n_lat % TM == 0 and n_ctx % TM == 0 and n_lat % (NA_WIN_ROWS * GRID_W) == 0
    t = n_lat + n_ctx
    nt = t // TM
    nxt = n_lat // TM

    h = jnp.concatenate([x, ctx], axis=1)
    cond_rows = -(-(nb + FFN_BPS) // 8) * 8
    cond = jnp.concatenate([c, jnp.broadcast_to(c_ctx[None, :], (FFN_BPS, d)),
                            jnp.zeros((cond_rows - nb - FFN_BPS, d), F32)], axis=0)
    mod_all = _ada_mod(cond, ada_w, ada_b).reshape(DEPTH, cond_rows, 6, d)
    cos_t, sin_t = _rope_tables(n_lat, n_ctx)
    perm = _diff_head_perm()
    na_variants, na_vid = _na_variants(n_lat // GRID_W)
    expand = jnp.asarray(np.kron(np.eye(LANES, SSD_H, dtype=np.float32), np.ones((1, SSD_P), np.float32)))

    for i in range(DEPTH):
        last = i == DEPTH - 1
        j = i // 2
        mod = mod_all[i]
        n_tiles = nxt if last else nt
        row = lambda v: v.reshape(1, -1)
        if i % 2 == 0:
            w_in = hyb_w_in[j]
            o_xbc = SSD_INNER
            o_dt = o_xbc + SSD_CONV_DIM
            o_q = o_dt + 2 * SSD_H
            w_main = jnp.concatenate([w_in[:, o_xbc:o_dt], w_in[:, :SSD_INNER],
                                      w_in[:, o_q:o_q + NA_WIDTH] * NA_DH ** -0.5,
                                      w_in[:, o_q + NA_WIDTH:]], axis=1).astype(BF16)
            w_dt = jnp.zeros((d, 2 * LANES), F32)
            w_dt = w_dt.at[:, :SSD_H].set(w_in[:, o_dt:o_dt + SSD_H])
            w_dt = w_dt.at[:, LANES:LANES + SSD_H].set(w_in[:, o_dt + SSD_H:o_q]).astype(BF16)
            pm, dt, na_qt, na_k, na_vt = _hyb_proj(h, mod, w_main, w_dt, ssd_conv_w[j], row(ssd_conv_b[j]), nxt)

            pad = lambda v: jnp.zeros((2, 1, LANES), F32).at[:, 0, :SSD_H].set(v)
            arow = pad(-jnp.exp(ssd_a_log[j].astype(F32)))
            brow = pad(ssd_dt_bias[j].astype(F32))
            dsk = jnp.repeat(ssd_d[j].astype(F32), SSD_P, axis=-1).reshape(2, 1, SSD_INNER)
            yf, yb = _ssd(pm, dt, arow, brow, dsk, expand, n_lat)
            att = _na(na_qt, na_k, na_vt, _na_bias_table(na_rpb[j], na_variants), na_vid, n_lat)
            w_out = hyb_w_out[j].astype(BF16)
            h = _hyb_out(yf, yb, pm, att, h, mod, row(ssd_norm_w[j]), w_out[:SSD_INNER], w_out[SSD_INNER:],
                         row(ln1_g[i]), row(ln1_b[i]), nxt)
        else:
            lam_init = 0.8 - 0.6 * math.exp(-0.3 * i)
            w_in = diff_w_in[j]
            wq = w_in[:, :d][:, perm] * (DIFF_SUB ** -0.5 * LOG2E)
            wk = w_in[:, d:2 * d][:, perm]
            w_qkv = jnp.concatenate([wq, wk, w_in[:, 2 * d:]], axis=1).astype(BF16)
            d_qt, d_k, d_vt = _diff_proj(h, mod, w_qkv, cos_t, sin_t, nxt)
            subln_b = jnp.broadcast_to(diff_subln_w[j].astype(F32)[:, None], (2 * DIFF_SUB, TM))
            att = _diff_attn(d_qt, d_k, d_vt, diff_lambda[j].astype(F32), subln_b, n_lat, lam_init, n_tiles, TM)
            h = _diff_out(att, h, mod, diff_w_out[j].astype(BF16), row(ln1_g[i]), row(ln1_b[i]), nxt, n_tiles)
        h = _ffn(h, mod, ffn_w_up[i].astype(BF16), ffn_conv_w[i], row(ffn_conv_b[i]),
                 ffn_w_down[i].astype(BF16), row(ln2_g[i]), row(ln2_b[i]), nxt, n_tiles)
    return h
```

```python
import functools
import math

import numpy as np
import jax
import jax.numpy as jnp
from jax import lax
from jax.experimental import pallas as pl
from jax.experimental.pallas import tpu as pltpu

F32 = jnp.float32
BF16 = jnp.bfloat16

D_MODEL = 1024
DEPTH = 4
GRID_W = 64
SSD_P = 64
SSD_H = 16
SSD_G = 4
SSD_R = SSD_H // SSD_G
SSD_N = 128
SSD_INNER = SSD_H * SSD_P
SSD_GN = SSD_G * SSD_N
SSD_CONV_W = 5
SSD_CONV_DIM = SSD_INNER + 2 * SSD_GN
SSD_CHUNK = 128
SSD_BPS = 2
NA_HEADS = 8
NA_DH = 64
NA_WIDTH = NA_HEADS * NA_DH
NA_WIN_ROWS = 8
NA_WIN_COLS = 16
DIFF_HEADS = 8
DIFF_SUB = 64
ROPE_BASE = 10000.0
D_FF = 2816
FFN_CH = 256
FFN_BPS = 2
ALPHA = (2.0 * DEPTH) ** 0.25
LN_EPS = 1e-5
RMS_EPS = 1e-5

LANES = 128
TM = 256
HALO_F32 = 8
HALO_BF16 = 16
VMEM_LIMIT = 56 * 1024 * 1024
HYB_SSD = SSD_CONV_DIM + SSD_INNER
HYB_MAIN = HYB_SSD + 3 * NA_WIDTH
NA_VROWS = NA_DH + HALO_BF16
NA_BLK_ROWS = 4
NA_KEY_BLKS = 3
DIFF_VROWS = 2 * DIFF_SUB
DIFF_KCH = 1024
DIFF_HPS = 4
LOG2E = 1.4426950408889634
SUM_LO = 2.0 ** -90
SUM_HI = 2.0 ** 100
NORM_SLACK = 1.0 + 2.0 ** -5
HYB_NCH = 512


def _params(sem, vmem=VMEM_LIMIT):
    return pltpu.CompilerParams(dimension_semantics=sem, vmem_limit_bytes=vmem)


def _resident(shape):
    nd = len(shape)
    return pl.BlockSpec(shape, lambda *_: (0,) * nd, pipeline_mode=pl.Buffered(1))


def _dot(a, b):
    return jnp.dot(a, b, preferred_element_type=F32)


def _dot_nt(a, b):
    return lax.dot_general(a, b, (((1,), (1,)), ((), ())), preferred_element_type=F32)


def _silu(v):
    return v * jax.nn.sigmoid(v)


def _layer_norm(r, g, b):
    mu = jnp.mean(r, axis=-1, keepdims=True)
    xc = r - mu
    var = jnp.mean(xc * xc, axis=-1, keepdims=True)
    return xc * lax.rsqrt(var + LN_EPS) * g + b


def _mod_spec(nxt, nb):
    return pl.BlockSpec((1, 6, D_MODEL), lambda b, i: (jnp.where(i >= nxt, nb, b), 0, 0))


def _ada_kernel(c_ref, w_ref, b_ref, o_ref):
    s = _silu(c_ref[...]).astype(BF16)
    o_ref[0] = _dot(s, w_ref[0].astype(BF16)) + b_ref[0]


def _ada_mod(cond, ada_w, ada_b):
    rows = cond.shape[0]
    n = ada_w.shape[-1]
    tn = n // 4
    return pl.pallas_call(
        _ada_kernel,
        grid=(DEPTH, n // tn),
        in_specs=[pl.BlockSpec((rows, D_MODEL), lambda l, j: (0, 0)),
                  pl.BlockSpec((1, D_MODEL, tn), lambda l, j: (l, 0, j)),
                  pl.BlockSpec((1, 1, tn), lambda l, j: (l, 0, j))],
        out_specs=pl.BlockSpec((1, rows, tn), lambda l, j: (l, 0, j)),
        out_shape=jax.ShapeDtypeStruct((DEPTH, rows, n), F32),
        name="ada_mod",
        compiler_params=_params(("parallel", "parallel")),
    )(cond, ada_w, ada_b.reshape(DEPTH, 1, n))


def _hyb_proj_kernel(h_ref, hp_ref, hn_ref, mod_ref, w_ref, wdt_ref, cw_ref, cb_ref, sel_ref,
                     o_ref, dt_ref, qt_ref, k_ref, vt_ref, kn_ref, ext_ref, *, nxt, nt):
    i = pl.program_id(1)
    m = mod_ref[0]
    shift, scale = m[0:1, :], m[1:2, :]
    seg_start = jnp.logical_or(i == 0, i == nxt)
    seg_end = jnp.logical_or(i == nxt - 1, i == nt - 1)
    um = h_ref[0] * (1.0 + scale) + shift
    u = um.astype(BF16)
    up = jnp.where(seg_start, 0.0, hp_ref[0] * (1.0 + scale) + shift)
    un = jnp.where(seg_end, 0.0, hn_ref[0] * (1.0 + scale) + shift)
    uext = jnp.concatenate([up, um, un], axis=0).astype(BF16)

    def xbc_proj(c):
        ext_ref[c % 2] = _dot(uext, w_ref[:, c * HYB_NCH:(c + 1) * HYB_NCH])

    n_ch = SSD_CONV_DIM // HYB_NCH
    xbc_proj(0)
    for c in range(n_ch):
        n0 = c * HYB_NCH
        if c + 1 < n_ch:
            xbc_proj(c + 1)
        acc = cb_ref[:, n0:n0 + HYB_NCH]
        for k in range(SSD_CONV_W):
            acc = acc + cw_ref[k:k + 1, n0:n0 + HYB_NCH] * ext_ref[c % 2, pl.ds(HALO_F32 - SSD_CONV_W // 2 + k, TM), :]
        o_ref[0, :, n0:n0 + HYB_NCH] = _silu(acc).astype(BF16)
    for n0 in range(SSD_CONV_DIM, HYB_SSD, HYB_NCH):
        o_ref[0, :, n0:n0 + HYB_NCH] = _dot(u, w_ref[:, n0:n0 + HYB_NCH]).astype(BF16)
    dt_ref[0] = _dot(u, wdt_ref[...])
    qt_ref[0] = _dot(u, w_ref[:, HYB_SSD:HYB_SSD + NA_WIDTH]).T.astype(BF16)
    kk = _dot(u, w_ref[:, HYB_SSD + NA_WIDTH:HYB_SSD + 2 * NA_WIDTH])
    k_ref[0] = kk.astype(BF16)
    kn_ref[0] = _dot((kk * kk).astype(BF16), sel_ref[...])
    v_t = _dot(u, w_ref[:, HYB_SSD + 2 * NA_WIDTH:HYB_MAIN]).T
    for hh in range(NA_HEADS):
        vt_ref[0, hh, 0:NA_DH, :] = v_t[hh * NA_DH:(hh + 1) * NA_DH, :].astype(BF16)
        vt_ref[0, hh, NA_DH:NA_VROWS, :] = jnp.ones((NA_VROWS - NA_DH, TM), BF16)


def _hyb_proj(h, mod, w_main, w_dt, conv_w, conv_b, nxt):
    nb, t, _ = h.shape
    nt = t // TM
    pair_sel = jnp.asarray(np.kron(np.eye(NA_WIDTH // LANES, LANES, dtype=np.float32),
                                   np.ones((LANES, 1), np.float32)), BF16)
    hb = TM // HALO_F32
    last_hb = t // HALO_F32 - 1
    return pl.pallas_call(
        functools.partial(_hyb_proj_kernel, nxt=nxt, nt=nt),
        grid=(nb, nt),
        in_specs=[pl.BlockSpec((1, TM, D_MODEL), lambda b, i: (b, i, 0)),
                  pl.BlockSpec((1, HALO_F32, D_MODEL), lambda b, i: (b, jnp.maximum(i * hb - 1, 0), 0)),
                  pl.BlockSpec((1, HALO_F32, D_MODEL), lambda b, i: (b, jnp.minimum((i + 1) * hb, last_hb), 0)),
                  _mod_spec(nxt, nb),
                  _resident(w_main.shape),
                  _resident(w_dt.shape),
                  _resident(conv_w.shape),
                  _resident(conv_b.shape),
                  _resident(pair_sel.shape)],
        out_specs=[pl.BlockSpec((1, TM, HYB_SSD), lambda b, i: (b, i, 0)),
                   pl.BlockSpec((1, TM, 2 * LANES), lambda b, i: (b, i, 0)),
                   pl.BlockSpec((1, NA_WIDTH, TM), lambda b, i: (b, 0, i)),
                   pl.BlockSpec((1, TM, NA_WIDTH), lambda b, i: (b, i, 0)),
                   pl.BlockSpec((1, NA_HEADS, NA_VROWS, TM), lambda b, i: (b, 0, 0, i)),
                   pl.BlockSpec((1, TM, LANES), lambda b, i: (b, i, 0))],
        out_shape=[jax.ShapeDtypeStruct((nb, t, HYB_SSD), BF16),
                   jax.ShapeDtypeStruct((nb, t, 2 * LANES), F32),
                   jax.ShapeDtypeStruct((nb, NA_WIDTH, t), BF16),
                   jax.ShapeDtypeStruct((nb, t, NA_WIDTH), BF16),
                   jax.ShapeDtypeStruct((nb, NA_HEADS, NA_VROWS, t), BF16),
                   jax.ShapeDtypeStruct((nb, t, LANES), F32)],
        scratch_shapes=[pltpu.VMEM((2, TM + 2 * HALO_F32, HYB_NCH), F32)],
        name="hyb_proj",
        compiler_params=_params(("parallel", "parallel")),
    )(h, h, h, mod, w_main, w_dt, conv_w, conv_b, pair_sel)


def _ssd_kernel(uf_ref, ub_ref, dtf_ref, dtb_ref, arow_ref, brow_ref, dsk_ref, e_ref, yf_ref, yb_ref, st_ref):
    j = pl.program_id(1)

    @pl.when(j == 0)
    def _():
        st_ref[...] = jnp.zeros_like(st_ref)

    for bi in range(SSD_BPS):
        _ssd_chunk(0, bi, uf_ref, dtf_ref, arow_ref, brow_ref, dsk_ref, e_ref, yf_ref, st_ref)
        _ssd_chunk(1, bi, ub_ref, dtb_ref, arow_ref, brow_ref, dsk_ref, e_ref, yb_ref, st_ref)


def _ssd_chunk(d, bi, u_ref, dt_ref, arow_ref, brow_ref, dsk_ref, e_ref, y_ref, st_ref):
    q = SSD_CHUNK
    u = u_ref[bi].astype(F32)
    xs = u[:, :SSD_INNER]

    dtr = dt_ref[bi] + brow_ref[d]
    dtv = jnp.maximum(dtr, 0.0) + jnp.log1p(jnp.exp(-jnp.abs(dtr)))
    adt = dtv * arow_ref[d]
    ri = lax.broadcasted_iota(jnp.int32, (q, q), 0)
    ci = lax.broadcasted_iota(jnp.int32, (q, q), 1)
    tri = (ri >= ci) if d == 0 else (ri <= ci)
    cs = jnp.dot(tri.astype(F32), adt, precision=lax.Precision.HIGHEST, preferred_element_type=F32)
    cs_t = cs.T
    dt_t = dtv.T
    last = q - 1 if d == 0 else 0
    tot = cs[last:last + 1, :]
    tot_t = cs_t[:, last:last + 1]
    w_t = jnp.exp(tot_t - cs_t) * dt_t
    dec_row = jnp.dot(jnp.broadcast_to(jnp.exp(tot), (8, LANES)), e_ref[...],
                      precision=lax.Precision.HIGHEST, preferred_element_type=F32)[0:1]

    lane = lax.broadcasted_iota(jnp.int32, (q, LANES), 1)
    lo = lane < SSD_P
    dsk = dsk_ref[d]
    for g in range(SSD_G):
        bm = u[:, SSD_INNER + g * SSD_N:SSD_INNER + (g + 1) * SSD_N]
        cm = u[:, SSD_INNER + SSD_GN + g * SSD_N:SSD_INNER + SSD_GN + (g + 1) * SSD_N]
        cmb = cm.astype(BF16)
        cb = _dot_nt(cmb, bm.astype(BF16))
        bm_t = bm.T
        s_prev = st_ref[bi, d, g]
        y_off = _dot(cmb, s_prev.astype(BF16))
        s_parts = []
        for pr in range(SSD_R // 2):
            col0 = g * SSD_R * SSD_P + pr * LANES
            xs_pair = xs[:, col0:col0 + LANES]
            xsb = xs_pair.astype(BF16)
            zero = jnp.zeros_like(xsb)
            xs_bd = jnp.concatenate([jnp.where(lo, xsb, zero), jnp.where(lo, zero, xsb)], axis=0)
            gm, bw, colbs = [], [], []
            for sub in range(2):
                h = g * SSD_R + pr * 2 + sub
                colb = jnp.broadcast_to(cs[:, h:h + 1], (q, q))
                decay = jnp.exp(jnp.where(tri, colb - cs_t[h:h + 1, :], -jnp.inf))
                gm.append((cb * decay * dt_t[h:h + 1, :]).astype(BF16))
                bw.append((bm_t * w_t[h:h + 1, :]).astype(BF16))
                colbs.append(colb)
            y_diag = _dot(jnp.concatenate(gm, axis=1), xs_bd)
            e_col = jnp.exp(jnp.where(lo, colbs[0], colbs[1]))
            y_pair = y_diag + y_off[:, pr * LANES:(pr + 1) * LANES] * e_col + dsk[:, col0:col0 + LANES] * xs_pair
            y_ref[bi, :, col0:col0 + LANES] = y_pair.astype(BF16)
            s_parts.append(_dot(jnp.concatenate(bw, axis=1), xs_bd))
        g0 = g * SSD_R * SSD_P
        st_ref[bi, d, g] = s_prev * dec_row[:, g0:g0 + SSD_R * SSD_P] + jnp.concatenate(s_parts, axis=1)


def _ssd(pm, dt, arow, brow, dsk, expand, n_lat):
    nb, t, _ = pm.shape
    q = SSD_CHUNK
    nc = t // q
    nxc = n_lat // q

    def cf(j):
        return (j + nxc) % nc

    def cbk(j):
        return nc - 1 - j

    y_shape = jax.ShapeDtypeStruct((nb, t, SSD_INNER), BF16)
    bps = SSD_BPS
    assert nb % bps == 0
    return pl.pallas_call(
        _ssd_kernel,
        grid=(nb // bps, nc),
        in_specs=[pl.BlockSpec((bps, q, SSD_CONV_DIM), lambda b, j: (b, cf(j), 0)),
                  pl.BlockSpec((bps, q, SSD_CONV_DIM), lambda b, j: (b, cbk(j), 0)),
                  pl.BlockSpec((bps, q, LANES), lambda b, j: (b, cf(j), 0)),
                  pl.BlockSpec((bps, q, LANES), lambda b, j: (b, cbk(j), 1)),
                  _resident(arow.shape), _resident(brow.shape), _resident(dsk.shape), _resident(expand.shape)],
        out_specs=[pl.BlockSpec((bps, q, SSD_INNER), lambda b, j: (b, cf(j), 0)),
                   pl.BlockSpec((bps, q, SSD_INNER), lambda b, j: (b, cbk(j), 0))],
        out_shape=[y_shape, y_shape],
        scratch_shapes=[pltpu.VMEM((bps, 2, SSD_G, SSD_N, SSD_R * SSD_P), F32)],
        name="ssd_scan",
        compiler_params=_params(("parallel", "arbitrary")),
    )(pm, pm, dt, dt, arow, brow, dsk, expand)


def _na_kernel(qt_ref, k0_ref, k1_ref, k2_ref, kc_ref, v0_ref, v1_ref, v2_ref, vc_ref,
               n0_ref, n1_ref, n2_ref, nc_ref, bias_ref, bmax_ref, o_ref, *, nblk):
    i = pl.program_id(1)
    nq = NA_BLK_ROWS * GRID_W
    zeros = jnp.zeros((NA_DH, nq), BF16)

    def run(with_local):
        k_refs = ([k0_ref, k1_ref, k2_ref] if with_local else []) + [kc_ref]
        v_refs = ([v0_ref, v1_ref, v2_ref] if with_local else []) + [vc_ref]
        n_refs = ([n0_ref, n1_ref, n2_ref] if with_local else []) + [nc_ref]
        kmax = jnp.sqrt(NORM_SLACK * functools.reduce(
            jnp.maximum, [jnp.max(nr[0], axis=0, keepdims=True) for nr in n_refs]))

        def scores(hh):
            sl = slice((hh // 2) * LANES, (hh // 2 + 1) * LANES)
            qh = qt_ref[0, hh * NA_DH:(hh + 1) * NA_DH, :]
            qm = jnp.concatenate([qh, zeros] if hh % 2 == 0 else [zeros, qh], axis=0)
            sc = [_dot(kr[0, :, sl], qm) for kr in k_refs]
            if with_local:
                sc = [s + bias_ref[0, hh, jb * nq:(jb + 1) * nq, :] if jb < NA_KEY_BLKS else s
                      for jb, s in enumerate(sc)]
            return sc

        def attend(hh, s_cur, shift):
            o_t = functools.reduce(lambda a, b: a + b, [_dot(vr[0, hh], jnp.exp(s - shift).astype(BF16))
                                                          for s, vr in zip(s_cur, v_refs)])
            return o_t[:NA_DH, :], o_t[NA_DH:NA_DH + 1, :]

        outs, oks = [], []
        s_next = scores(0)
        for hh in range(NA_HEADS):
            s_cur = s_next
            if hh + 1 < NA_HEADS:
                s_next = scores(hh + 1)
            qh = qt_ref[0, hh * NA_DH:(hh + 1) * NA_DH, :].astype(F32)
            bound = jnp.sqrt(jnp.sum(qh * qh, axis=0, keepdims=True)) * kmax[:, hh // 2:hh // 2 + 1]
            if with_local:
                bound = bound + jnp.maximum(bmax_ref[hh][:, 0:1], 0.0)
            num, den = attend(hh, s_cur, bound)
            oks.append(jnp.logical_and(jnp.min(den) > SUM_LO, jnp.max(den) < SUM_HI))
            outs.append(num * (1.0 / den))
        o_ref[0] = jnp.concatenate(outs, axis=0).T.astype(BF16)

        for hh in range(NA_HEADS):
            @pl.when(jnp.logical_not(oks[hh]))
            def _(hh=hh):
                s_cur = scores(hh)
                mx = functools.reduce(jnp.maximum, [jnp.max(s, axis=0, keepdims=True) for s in s_cur])
                num, den = attend(hh, s_cur, mx)
                o_ref[0, :, hh * NA_DH:(hh + 1) * NA_DH] = (num * (1.0 / den)).T.astype(BF16)

    @pl.when(i < nblk)
    def _():
        run(True)

    @pl.when(i >= nblk)
    def _():
        run(False)


def _na_variants(rows):
    nblk = rows // NA_BLK_ROWS
    assert rows % NA_BLK_ROWS == 0 and nblk >= NA_KEY_BLKS
    keys, vid = [], []
    for b in range(nblk):
        ws = int(np.clip(b - 1, 0, nblk - NA_KEY_BLKS)) * NA_BLK_ROWS
        key = []
        for a in range(NA_BLK_ROWS):
            r = b * NA_BLK_ROWS + a
            r0 = int(np.clip(r - NA_WIN_ROWS // 2, 0, rows - NA_WIN_ROWS))
            assert 0 <= r0 - ws and r0 - ws + NA_WIN_ROWS <= NA_KEY_BLKS * NA_BLK_ROWS
            key.append((r0 - ws, r0 - r))
        key = tuple(key)
        if key not in keys:
            keys.append(key)
        vid.append(keys.index(key))
    return keys, vid


def _na_bias_table(rpb, variants):
    c = np.arange(GRID_W)[:, None]
    kc = np.arange(GRID_W)[None, :]
    c0 = np.clip(c - NA_WIN_COLS // 2, 0, GRID_W - NA_WIN_COLS)
    valid_c = (kc >= c0) & (kc < c0 + NA_WIN_COLS)
    dc = kc - c + NA_WIN_COLS - 1
    onehot = ((dc[None] == np.arange(2 * NA_WIN_COLS - 1)[:, None, None]) & valid_c[None]).astype(np.float32)
    tz = jnp.einsum('hrd,dck->hrck', rpb.astype(F32), jnp.asarray(onehot), precision=lax.Precision.HIGHEST)
    nkr = NA_KEY_BLKS * NA_BLK_ROWS
    ndr = 2 * NA_WIN_ROWS - 1
    tabs = []
    for key in variants:
        sel = np.zeros((NA_BLK_ROWS, nkr, ndr), np.float32)
        valid_r = np.zeros((NA_BLK_ROWS, nkr), bool)
        for a, (off, e) in enumerate(key):
            for w in range(NA_WIN_ROWS):
                sel[a, off + w, e + NA_WIN_ROWS - 1 + w] = 1.0
                valid_r[a, off + w] = True
        tab = jnp.einsum('ajr,hrck->hajck', jnp.asarray(sel), tz, precision=lax.Precision.HIGHEST)
        ok = jnp.asarray(valid_r[None, :, :, None, None] & valid_c[None, None, None, :, :])
        tab = jnp.where(ok, tab, -jnp.inf)
        tabs.append(tab.transpose(0, 2, 4, 1, 3).reshape(NA_HEADS, nkr * GRID_W, NA_BLK_ROWS * GRID_W))
    return jnp.stack(tabs)


def _na(qt, k, vt, kn, bias, bmax, vid, n_lat):
    nb, t, _ = k.shape
    n_ctx = t - n_lat
    nq = NA_BLK_ROWS * GRID_W
    nblk = n_lat // nq
    assert n_ctx % nq == 0 and n_lat % n_ctx == 0
    steps = t // nq
    cblk = n_lat // n_ctx

    def wb(i):
        return jnp.clip(i - 1, 0, nblk - NA_KEY_BLKS)

    def variant(i):
        v = jnp.int32(0)
        for blk in range(1, nblk):
            v = jnp.where(i == blk, vid[blk], v)
        return v

    kspec = [pl.BlockSpec((1, nq, NA_WIDTH), functools.partial(lambda b, i, j: (b, wb(i) + j, 0), j=j))
             for j in range(NA_KEY_BLKS)]
    vspec = [pl.BlockSpec((1, NA_HEADS, NA_VROWS, nq), functools.partial(lambda b, i, j: (b, 0, 0, wb(i) + j), j=j))
             for j in range(NA_KEY_BLKS)]
    nspec = [pl.BlockSpec((1, nq, LANES), functools.partial(lambda b, i, j: (b, wb(i) + j, 0), j=j))
             for j in range(NA_KEY_BLKS)]
    return pl.pallas_call(
        functools.partial(_na_kernel, nblk=nblk),
        grid=(nb, steps),
        in_specs=([pl.BlockSpec((1, NA_WIDTH, nq), lambda b, i: (b, 0, i))] + kspec
                  + [pl.BlockSpec((1, n_ctx, NA_WIDTH), lambda b, i: (b, cblk, 0))] + vspec
                  + [pl.BlockSpec((1, NA_HEADS, NA_VROWS, n_ctx), lambda b, i: (b, 0, 0, cblk))] + nspec
                  + [pl.BlockSpec((1, n_ctx, LANES), lambda b, i: (b, cblk, 0)),
                     pl.BlockSpec((1,) + bias.shape[1:], lambda b, i: (variant(i), 0, 0, 0)),
                     _resident(bmax.shape)]),
        out_specs=pl.BlockSpec((1, nq, NA_WIDTH), lambda b, i: (b, i, 0)),
        out_shape=jax.ShapeDtypeStruct((nb, t, NA_WIDTH), BF16),
        name="na_attn",
        compiler_params=_params(("parallel", "arbitrary")),
    )(qt, k, k, k, k, vt, vt, vt, vt, kn, kn, kn, kn, bias, bmax)


def _hyb_out_kernel(yf_ref, yb_ref, z_ref, a_ref, h_ref, mod_ref, nw_ref, wy_ref, wa_ref, g_ref, b_ref, o_ref):
    y = (yf_ref[0].astype(F32) + yb_ref[0].astype(F32)) * _silu(z_ref[0].astype(F32))
    y = y * lax.rsqrt(jnp.mean(y * y, axis=-1, keepdims=True) + RMS_EPS) * nw_ref[...]
    o = _dot(y.astype(BF16), wy_ref[...]) + _dot(a_ref[0], wa_ref[...])
    gate = mod_ref[0][2:3, :]
    o_ref[0] = _layer_norm(ALPHA * h_ref[0] + gate * o, g_ref[...], b_ref[...])


def _hyb_out(yf, yb, pm, att, h, mod, norm_w, wy, wa, ln_g, ln_b, nxt):
    nb, t, _ = h.shape
    nt = t // TM
    zcol = SSD_CONV_DIM // SSD_INNER
    return pl.pallas_call(
        _hyb_out_kernel,
        grid=(nb, nt),
        in_specs=[pl.BlockSpec((1, TM, SSD_INNER), lambda b, i: (b, i, 0)),
                  pl.BlockSpec((1, TM, SSD_INNER), lambda b, i: (b, i, 0)),
                  pl.BlockSpec((1, TM, SSD_INNER), lambda b, i: (b, i, zcol)),
                  pl.BlockSpec((1, TM, NA_WIDTH), lambda b, i: (b, i, 0)),
                  pl.BlockSpec((1, TM, D_MODEL), lambda b, i: (b, i, 0)),
                  _mod_spec(nxt, nb),
                  _resident(norm_w.shape), _resident(wy.shape), _resident(wa.shape),
                  _resident(ln_g.shape), _resident(ln_b.shape)],
        out_specs=pl.BlockSpec((1, TM, D_MODEL), lambda b, i: (b, i, 0)),
        out_shape=jax.ShapeDtypeStruct((nb, t, D_MODEL), F32),
        name="hyb_out",
        compiler_params=_params(("parallel", "parallel")),
    )(yf, yb, pm, att, h, mod, norm_w, wy, wa, ln_g, ln_b)


def _ffn_kernel(hm_ref, hp_ref, hn_ref, mod_ref, wup_ref, cw_ref, cb_ref, wdn_ref, g_ref, b_ref, p_ref, o_ref,
                *scratch, nxt, nt):
    per_tile = len(scratch) // FFN_BPS
    hv_ref = [scratch[bi * per_tile:bi * per_tile + 2] for bi in range(FFN_BPS)]
    hg_ref = [scratch[bi * per_tile + 2:bi * per_tile + 4] for bi in range(FFN_BPS)]
    acc_ref = [scratch[bi * per_tile + 4] for bi in range(FFN_BPS)]
    perm_ref = [scratch[bi * per_tile + 5] for bi in range(FFN_BPS)]
    i = pl.program_id(1)
    seg_start = jnp.logical_or(i == 0, i == nxt)
    seg_end = jnp.logical_or(i == nxt - 1, i == nt - 1)
    sl = HALO_F32
    ngrp = TM // sl
    ncb = D_MODEL // LANES
    sub_d = lax.broadcasted_iota(jnp.int32, (2 * sl, D_MODEL), 0)
    sub_c = lax.broadcasted_iota(jnp.int32, (sl, FFN_CH), 0)

    def load_tile(bi):
        m = mod_ref[bi]
        shift, scale = m[3:4, :], m[4:5, :]
        hm = hm_ref[bi]
        u_perm = _dot(p_ref[...], (hm * (1.0 + scale) + shift).astype(BF16)).astype(BF16)
        prev_row = jnp.where(seg_start, 0.0, hp_ref[bi, sl - 1:sl, :] * (1.0 + scale) + shift)
        next_row = jnp.where(seg_end, 0.0, hn_ref[bi, 0:1, :] * (1.0 + scale) + shift)
        halo = jnp.where(sub_d == 0, prev_row, jnp.where(sub_d == 1, next_row, 0.0))
        uext = jnp.concatenate([u_perm, halo.astype(BF16)], axis=0)
        return hm, uext

    def conv(ref, col0):
        w = cw_ref[:, col0:col0 + FFN_CH]
        before_first = jnp.where(sub_c == 0, ref[TM:TM + 1, :], pltpu.roll(ref[TM - sl:TM, :], 1, axis=0))
        after_last = jnp.where(sub_c == sl - 1, ref[TM + 1:TM + 2, :], pltpu.roll(ref[0:sl, :], sl - 1, axis=0))
        prev = jnp.concatenate([before_first, ref[0:TM - sl, :]], axis=0)
        nxt_ = jnp.concatenate([ref[sl:TM, :], after_last], axis=0)
        return w[0:1, :] * prev + w[1:2, :] * ref[0:TM, :] + w[2:3, :] * nxt_ + cb_ref[:, col0:col0 + FFN_CH]

    n_ch = D_FF // FFN_CH

    def up_proj(bi, uext, c):
        hv_ref[bi][c % 2][...] = _dot(uext, wup_ref[:, c * FFN_CH:(c + 1) * FFN_CH])
        hg_ref[bi][c % 2][...] = _dot(uext, wup_ref[:, D_FF + c * FFN_CH:D_FF + (c + 1) * FFN_CH])

    def chunk(bi, uext, c):
        v0 = c * FFN_CH
        if c + 1 < n_ch:
            up_proj(bi, uext, c + 1)
        act = (_silu(conv(hg_ref[bi][c % 2], D_FF + v0)) * conv(hv_ref[bi][c % 2], v0)).astype(BF16)
        contrib = _dot(act, wdn_ref[v0:v0 + FFN_CH, :])
        if c == 0:
            acc_ref[bi][...] = contrib
        else:
            acc_ref[bi][...] += contrib

    def finish(bi, hm):
        gate = mod_ref[bi][5:6, :]
        acc = acc_ref[bi][...]
        for cb in range(ncb):
            for a in range(ngrp):
                perm_ref[bi][cb, pl.ds(a, sl, stride=ngrp), :] = acc[a * sl:(a + 1) * sl, cb * LANES:(cb + 1) * LANES]
        f = jnp.concatenate([perm_ref[bi][cb] for cb in range(ncb)], axis=1)
        o_ref[bi] = _layer_norm(ALPHA * hm + gate * f, g_ref[...], b_ref[...])

    tiles = [load_tile(bi) for bi in range(FFN_BPS)]
    for bi, (hm, uext) in enumerate(tiles):
        up_proj(bi, uext, 0)
        for c in range(n_ch):
            chunk(bi, uext, c)
        finish(bi, hm)


def _ffn(h, mod, w_up, conv_w, conv_b, w_dn, ln_g, ln_b, nxt, n_out_tiles):
    nb, t, _ = h.shape
    nt = t // TM
    hb = TM // HALO_F32
    last_hb = t // HALO_F32 - 1
    bps = FFN_BPS
    assert nb % bps == 0
    ngrp = TM // HALO_F32
    src = (np.arange(TM) % HALO_F32) * ngrp + np.arange(TM) // HALO_F32
    perm = jnp.asarray(np.eye(TM, dtype=np.float32)[src], BF16)
    mod_spec = pl.BlockSpec((bps, 6, D_MODEL), lambda b, i: (jnp.where(i >= nxt, nb // bps, b), 0, 0))
    return pl.pallas_call(
        functools.partial(_ffn_kernel, nxt=nxt, nt=nt),
        grid=(nb // bps, n_out_tiles),
        in_specs=[pl.BlockSpec((bps, TM, D_MODEL), lambda b, i: (b, i, 0)),
                  pl.BlockSpec((bps, HALO_F32, D_MODEL), lambda b, i: (b, jnp.maximum(i * hb - 1, 0), 0)),
                  pl.BlockSpec((bps, HALO_F32, D_MODEL), lambda b, i: (b, jnp.minimum((i + 1) * hb, last_hb), 0)),
                  mod_spec,
                  _resident(w_up.shape), _resident(conv_w.shape), _resident(conv_b.shape),
                  _resident(w_dn.shape), _resident(ln_g.shape), _resident(ln_b.shape), _resident(perm.shape)],
        out_specs=pl.BlockSpec((bps, TM, D_MODEL), lambda b, i: (b, i, 0)),
        out_shape=jax.ShapeDtypeStruct((nb, n_out_tiles * TM, D_MODEL), F32),
        scratch_shapes=bps * ([pltpu.VMEM((TM + 2 * HALO_F32, FFN_CH), F32)] * 4
                              + [pltpu.VMEM((TM, D_MODEL), F32),
                                 pltpu.VMEM((D_MODEL // LANES, TM, LANES), F32)]),
        name="conv_ffn",
        compiler_params=_params(("parallel", "parallel")),
    )(h, h, h, mod, w_up, conv_w, conv_b, w_dn, ln_g, ln_b, perm)


def _diff_proj_kernel(h_ref, mod_ref, w_ref, cos_ref, sin_ref, cost_ref, sint_ref, qt_ref, k_ref, vt_ref):
    m = mod_ref[0]
    u = (h_ref[0] * (1.0 + m[1:2, :]) + m[0:1, :]).astype(BF16)
    half = LANES // 2
    for n0 in range(0, D_MODEL, HYB_NCH):
        y_t = _dot(u, w_ref[:, n0:n0 + HYB_NCH]).T
        for p0 in range(0, HYB_NCH, LANES):
            blk = y_t[p0:p0 + LANES, :]
            partner = jnp.concatenate([blk[half:, :], blk[:half, :]], axis=0)
            qt_ref[0, n0 + p0:n0 + p0 + LANES, :] = (blk * cost_ref[...] + partner * sint_ref[...]).astype(BF16)
    for n0 in range(0, D_MODEL, HYB_NCH):
        y = _dot(u, w_ref[:, D_MODEL + n0:D_MODEL + n0 + HYB_NCH])
        parts = []
        for p0 in range(0, HYB_NCH, LANES):
            yp = y[:, p0:p0 + LANES]
            parts.append(yp * cos_ref[...] + pltpu.roll(yp, half, axis=1) * sin_ref[...])
        k_ref[0, :, n0:n0 + HYB_NCH] = jnp.concatenate(parts, axis=1).astype(BF16)
    hd = 2 * DIFF_SUB
    for n0 in range(0, D_MODEL, HYB_NCH):
        y_t = _dot(u, w_ref[:, 2 * D_MODEL + n0:2 * D_MODEL + n0 + HYB_NCH]).T
        for p0 in range(0, HYB_NCH, hd):
            vt_ref[0, (n0 + p0) // hd, 0:hd, :] = y_t[p0:p0 + hd, :].astype(BF16)


def _diff_proj(h, mod, w, cos_t, sin_t, nxt):
    nb, t, _ = h.shape
    nt = t // TM
    return pl.pallas_call(
        _diff_proj_kernel,
        grid=(nb, nt),
        in_specs=[pl.BlockSpec((1, TM, D_MODEL), lambda b, i: (b, i, 0)),
                  _mod_spec(nxt, nb),
                  _resident(w.shape),
                  pl.BlockSpec((TM, LANES), lambda b, i: (i, 0)),
                  pl.BlockSpec((TM, LANES), lambda b, i: (i, 0)),
                  pl.BlockSpec((LANES, TM), lambda b, i: (0, i)),
                  pl.BlockSpec((LANES, TM), lambda b, i: (0, i))],
        out_specs=[pl.BlockSpec((1, D_MODEL, TM), lambda b, i: (b, 0, i)),
                   pl.BlockSpec((1, TM, D_MODEL), lambda b, i: (b, i, 0)),
                   pl.BlockSpec((1, DIFF_HEADS, DIFF_VROWS, TM), lambda b, i: (b, 0, 0, i))],
        out_shape=[jax.ShapeDtypeStruct((nb, D_MODEL, t), BF16),
                   jax.ShapeDtypeStruct((nb, t, D_MODEL), BF16),
                   jax.ShapeDtypeStruct((nb, DIFF_HEADS, DIFF_VROWS, t), BF16)],
        name="diff_proj",
        compiler_params=_params(("parallel", "parallel")),
    )(h, mod, w, cos_t, sin_t, cos_t.T, sin_t.T)


def _diff_attn_kernel(qt_ref, k_ref, vt_ref, lam_ref, sw_ref, o_ref, kmax_ref, *, n_lat, n_ctx, nxt, lam_init):
    i = pl.program_id(2)
    lp = lam_ref[...]
    lam = (jnp.exp(jnp.sum(lp[0:1, :] * lp[1:2, :], axis=1, keepdims=True))
           - jnp.exp(jnp.sum(lp[2:3, :] * lp[3:4, :], axis=1, keepdims=True)) + lam_init)
    hd = 2 * DIFF_SUB
    tq = qt_ref.shape[-1]
    row = lax.broadcasted_iota(jnp.int32, (hd, tq), 0)
    sub0 = (row % DIFF_SUB) < (DIFF_SUB // 2)
    zero = jnp.zeros((hd, tq), BF16)
    heads = range(DIFF_HPS)

    @pl.when(i == 0)
    def _():
        for hh in heads:
            kf = k_ref[0, :, hh * hd:(hh + 1) * hd].astype(F32)
            kmax_ref[hh] = jnp.sqrt(jnp.max(jnp.sum(kf * kf, axis=1, keepdims=True), axis=0, keepdims=True))

    qms, bounds = [], []
    for hh in heads:
        qt = qt_ref[0, hh * hd:(hh + 1) * hd, :]
        qms.append((jnp.where(sub0, qt, zero), jnp.where(sub0, zero, qt)))
        bounds.append([jnp.sqrt(jnp.sum(jnp.square(qm.astype(F32)), axis=0, keepdims=True)) * kmax_ref[hh]
                       for qm in qms[hh]])

    def finish(hh, acc, den):
        outs = [a * (1.0 / d) for a, d in zip(acc, den)]
        o = outs[0] - lam * outs[1]
        o = o * lax.rsqrt(jnp.mean(o * o, axis=0, keepdims=True) + RMS_EPS) * sw_ref[...] * (1.0 - lam_init)
        o_ref[0, :, hh * hd:(hh + 1) * hd] = o.T.astype(BF16)

    def run(k0, nk):
        chunks = [(k0 + c0, min(DIFF_KCH, nk - c0)) for c0 in range(0, nk, DIFF_KCH)]

        def attend(hh, shifted):
            def scores(ci):
                c0, n = chunks[ci]
                return [_dot(k_ref[0, c0:c0 + n, hh * hd:(hh + 1) * hd], qm) for qm in qms[hh]]

            s_next = scores(0)
            mx = [None, None]
            acc = [None, None]
            den = [None, None]
            for ci, (c0, n) in enumerate(chunks):
                s_cur = s_next
                if ci + 1 < len(chunks):
                    s_next = scores(ci + 1)
                vv = vt_ref[0, hh, :, c0:c0 + n]
                for sub in range(2):
                    if shifted:
                        p = jnp.exp2(s_cur[sub] - bounds[hh][sub])
                        pv = _dot(vv, p.astype(BF16))
                        ps = jnp.sum(p, axis=0, keepdims=True)
                        acc[sub] = pv if ci == 0 else acc[sub] + pv
                        den[sub] = ps if ci == 0 else den[sub] + ps
                    else:
                        cm = jnp.max(s_cur[sub], axis=0, keepdims=True)
                        m_new = cm if ci == 0 else jnp.maximum(mx[sub], cm)
                        p = jnp.exp2(s_cur[sub] - m_new)
                        pv = _dot(vv, p.astype(BF16))
                        ps = jnp.sum(p, axis=0, keepdims=True)
                        if ci > 0:
                            alpha = jnp.exp2(mx[sub] - m_new)
                            pv, ps = acc[sub] * alpha + pv, den[sub] * alpha + ps
                        acc[sub], den[sub], mx[sub] = pv, ps, m_new
            return acc, den

        oks = []
        for hh in heads:
            acc, den = attend(hh, True)
            sums = jnp.concatenate(den, axis=0)
            oks.append(jnp.logical_and(jnp.min(sums) > SUM_LO, jnp.max(sums) < SUM_HI))
            finish(hh, acc, den)

        for hh in heads:
            @pl.when(jnp.logical_not(oks[hh]))
            def _(hh=hh):
                finish(hh, *attend(hh, False))

    @pl.when(i < nxt)
    def _():
        run(0, n_lat + n_ctx)

    @pl.when(i >= nxt)
    def _():
        run(n_lat, n_ctx)


def _diff_attn(qt, k, vt, lam_p, subln_b, n_lat, lam_init, n_q_tiles, tq):
    nb, t, _ = k.shape
    n_ctx = t - n_lat
    nxt = n_lat // tq
    hd = 2 * DIFF_SUB
    return pl.pallas_call(
        functools.partial(_diff_attn_kernel, n_lat=n_lat, n_ctx=n_ctx, nxt=nxt, lam_init=lam_init),
        grid=(nb, DIFF_HEADS // DIFF_HPS, n_q_tiles),
        in_specs=[pl.BlockSpec((1, DIFF_HPS * hd, tq), lambda b, hh, i: (b, hh, i)),
                  pl.BlockSpec((1, t, DIFF_HPS * hd), lambda b, hh, i: (b, 0, hh)),
                  pl.BlockSpec((1, DIFF_HPS, DIFF_VROWS, t), lambda b, hh, i: (b, hh, 0, 0)),
                  _resident(lam_p.shape), _resident(subln_b.shape)],
        out_specs=pl.BlockSpec((1, tq, DIFF_HPS * hd), lambda b, hh, i: (b, i, hh)),
        out_shape=jax.ShapeDtypeStruct((nb, n_q_tiles * tq, D_MODEL), BF16),
        scratch_shapes=[pltpu.VMEM((DIFF_HPS, 1, 1), F32)],
        name="diff_attn",
        compiler_params=_params(("parallel", "parallel", "arbitrary")),
    )(qt, k, vt, lam_p, subln_b)


def _diff_out_kernel(a_ref, h_ref, mod_ref, w_ref, g_ref, b_ref, o_ref):
    o = _dot(a_ref[0], w_ref[...])
    gate = mod_ref[0][2:3, :]
    o_ref[0] = _layer_norm(ALPHA * h_ref[0] + gate * o, g_ref[...], b_ref[...])


def _diff_out(att, h, mod, w, ln_g, ln_b, nxt, n_tiles):
    nb, t, _ = h.shape
    return pl.pallas_call(
        _diff_out_kernel,
        grid=(nb, n_tiles),
        in_specs=[pl.BlockSpec((1, TM, D_MODEL), lambda b, i: (b, i, 0)),
                  pl.BlockSpec((1, TM, D_MODEL), lambda b, i: (b, i, 0)),
                  _mod_spec(nxt, nb),
                  _resident(w.shape), _resident(ln_g.shape), _resident(ln_b.shape)],
        out_specs=pl.BlockSpec((1, TM, D_MODEL), lambda b, i: (b, i, 0)),
        out_shape=jax.ShapeDtypeStruct((nb, n_tiles * TM, D_MODEL), F32),
        name="diff_out",
        compiler_params=_params(("parallel", "parallel")),
    )(att, h, mod, w, ln_g, ln_b)


def _rope_tables(n_lat, n_ctx):
    tok = np.arange(n_lat)
    row = (tok // GRID_W).astype(np.float32)
    col = (tok % GRID_W).astype(np.float32)
    n_freq = DIFF_SUB // 4
    inv = jnp.asarray(ROPE_BASE, F32) ** (-jnp.arange(n_freq, dtype=F32) / n_freq)
    ang = jnp.concatenate([jnp.asarray(row)[:, None] * inv, jnp.asarray(col)[:, None] * inv], axis=-1)
    cos = jnp.cos(ang)
    sin = jnp.sin(ang)
    cos_t = jnp.concatenate([cos, cos, cos, cos], axis=-1)
    sin_t = jnp.concatenate([-sin, -sin, sin, sin], axis=-1)
    cos_t = jnp.concatenate([cos_t, jnp.ones((n_ctx, LANES), F32)], axis=0)
    sin_t = jnp.concatenate([sin_t, jnp.zeros((n_ctx, LANES), F32)], axis=0)
    return cos_t, sin_t


def _diff_head_perm():
    half = DIFF_SUB // 2
    perm = []
    for hh in range(DIFF_HEADS):
        base = hh * 2 * DIFF_SUB
        for part in range(2):
            for sub in range(2):
                start = base + sub * DIFF_SUB + part * half
                perm.extend(range(start, start + half))
    return np.asarray(perm)


def kernel(x, c, ctx, c_ctx, ada_w, ada_b, ln1_g, ln1_b, ln2_g, ln2_b, ffn_w_up, ffn_conv_w, ffn_conv_b, ffn_w_down, hyb_w_in, ssd_conv_w, ssd_conv_b, ssd_a_log, ssd_dt_bias, ssd_d, ssd_norm_w, na_rpb, hyb_w_out, diff_w_in, diff_lambda, diff_subln_w, diff_w_out):
    nb, n_lat, d = x.shape
    n_ctx = ctx.shape[1]
    assert d == D_MODEL and n_lat % TM == 0 and n_ctx % TM == 0 and n_lat % (NA_WIN_ROWS * GRID_W) == 0
    t = n_lat + n_ctx
    nt = t // TM
    nxt = n_lat // TM

    h = jnp.concatenate([x, ctx], axis=1)
    cond_rows = -(-(nb + FFN_BPS) // 8) * 8
    cond = jnp.concatenate([c, jnp.broadcast_to(c_ctx[None, :], (FFN_BPS, d)),
                            jnp.zeros((cond_rows - nb - FFN_BPS, d), F32)], axis=0)
    mod_all = _ada_mod(cond, ada_w, ada_b).reshape(DEPTH, cond_rows, 6, d)
    cos_t, sin_t = _rope_tables(n_lat, n_ctx)
    perm = _diff_head_perm()
    na_variants, na_vid = _na_variants(n_lat // GRID_W)
    expand = jnp.asarray(np.kron(np.eye(LANES, SSD_H, dtype=np.float32), np.ones((1, SSD_P), np.float32)))

    for i in range(DEPTH):
        last = i == DEPTH - 1
        j = i // 2
        mod = mod_all[i]
        n_tiles = nxt if last else nt
        row = lambda v: v.reshape(1, -1)
        if i % 2 == 0:
            w_in = hyb_w_in[j]
            o_xbc = SSD_INNER
            o_dt = o_xbc + SSD_CONV_DIM
            o_q = o_dt + 2 * SSD_H
            w_main = jnp.concatenate([w_in[:, o_xbc:o_dt], w_in[:, :SSD_INNER],
                                      w_in[:, o_q:o_q + NA_WIDTH] * NA_DH ** -0.5,
                                      w_in[:, o_q + NA_WIDTH:]], axis=1).astype(BF16)
            w_dt = jnp.zeros((d, 2 * LANES), F32)
            w_dt = w_dt.at[:, :SSD_H].set(w_in[:, o_dt:o_dt + SSD_H])
            w_dt = w_dt.at[:, LANES:LANES + SSD_H].set(w_in[:, o_dt + SSD_H:o_q]).astype(BF16)
            pm, dt, na_qt, na_k, na_vt, na_kn = _hyb_proj(h, mod, w_main, w_dt, ssd_conv_w[j], row(ssd_conv_b[j]), nxt)

            pad = lambda v: jnp.zeros((2, 1, LANES), F32).at[:, 0, :SSD_H].set(v)
            arow = pad(-jnp.exp(ssd_a_log[j].astype(F32)))
            brow = pad(ssd_dt_bias[j].astype(F32))
            dsk = jnp.repeat(ssd_d[j].astype(F32), SSD_P, axis=-1).reshape(2, 1, SSD_INNER)
            yf, yb = _ssd(pm, dt, arow, brow, dsk, expand, n_lat)
            rpb_max = jnp.max(na_rpb[j].astype(F32).reshape(NA_HEADS, -1), axis=1)
            bmax = jnp.broadcast_to(rpb_max[:, None, None], (NA_HEADS, 1, LANES))
            att = _na(na_qt, na_k, na_vt, na_kn, _na_bias_table(na_rpb[j], na_variants), bmax, na_vid, n_lat)
            w_out = hyb_w_out[j].astype(BF16)
            h = _hyb_out(yf, yb, pm, att, h, mod, row(ssd_norm_w[j]), w_out[:SSD_INNER], w_out[SSD_INNER:],
                         row(ln1_g[i]), row(ln1_b[i]), nxt)
        else:
            lam_init = 0.8 - 0.6 * math.exp(-0.3 * i)
            w_in = diff_w_in[j]
            wq = w_in[:, :d][:, perm] * (DIFF_SUB ** -0.5 * LOG2E)
            wk = w_in[:, d:2 * d][:, perm]
            w_qkv = jnp.concatenate([wq, wk, w_in[:, 2 * d:]], axis=1).astype(BF16)
            d_qt, d_k, d_vt = _diff_proj(h, mod, w_qkv, cos_t, sin_t, nxt)
            subln_b = jnp.broadcast_to(diff_subln_w[j].astype(F32)[:, None], (2 * DIFF_SUB, TM))
            att = _diff_attn(d_qt, d_k, d_vt, diff_lambda[j].astype(F32), subln_b, n_lat, lam_init, n_tiles, TM)
            h = _diff_out(att, h, mod, diff_w_out[j].astype(BF16), row(ln1_g[i]), row(ln1_b[i]), nxt, n_tiles)
        h = _ffn(h, mod, ffn_w_up[i].astype(BF16), ffn_conv_w[i], row(ffn_conv_b[i]),
                 ffn_w_down[i].astype(BF16), row(ln2_g[i]), row(ln2_b[i]), nxt, n_tiles)
    return h
```

```python
import functools
import math

import numpy as np
import jax
import jax.numpy as jnp
from jax import lax
from jax.experimental import pallas as pl
from jax.experimental.pallas import tpu as pltpu

F32 = jnp.float32
BF16 = jnp.bfloat16

D_MODEL = 1024
DEPTH = 4
GRID_W = 64
SSD_P = 64
SSD_H = 16
SSD_G = 4
SSD_R = SSD_H // SSD_G
SSD_N = 128
SSD_INNER = SSD_H * SSD_P
SSD_GN = SSD_G * SSD_N
SSD_CONV_W = 5
SSD_CONV_DIM = SSD_INNER + 2 * SSD_GN
SSD_CHUNK = 128
SSD_BPS = 4
NA_HEADS = 8
NA_DH = 64
NA_WIDTH = NA_HEADS * NA_DH
NA_WIN_ROWS = 8
NA_WIN_COLS = 16
DIFF_HEADS = 8
DIFF_SUB = 64
ROPE_BASE = 10000.0
D_FF = 2816
FFN_CH = 256
FFN_BPS = 4
ALPHA = (2.0 * DEPTH) ** 0.25
LN_EPS = 1e-5
RMS_EPS = 1e-5

LANES = 128
TM = 256
HALO_F32 = 8
HALO_BF16 = 16
VMEM_LIMIT = 56 * 1024 * 1024
HYB_SSD = SSD_CONV_DIM + SSD_INNER
HYB_MAIN = HYB_SSD + 3 * NA_WIDTH
NA_VROWS = NA_DH + HALO_BF16
NA_BLK_ROWS = 4
NA_KEY_BLKS = 3
DIFF_VROWS = 2 * DIFF_SUB
DIFF_KCH = 1024
DIFF_HPS = 4
LOG2E = 1.4426950408889634
SUM_LO = 2.0 ** -90
SUM_HI = 2.0 ** 100
NORM_SLACK = 1.0 + 2.0 ** -5
HYB_NCH = 512


def _params(sem, vmem=VMEM_LIMIT):
    return pltpu.CompilerParams(dimension_semantics=sem, vmem_limit_bytes=vmem)


def _resident(shape):
    nd = len(shape)
    return pl.BlockSpec(shape, lambda *_: (0,) * nd, pipeline_mode=pl.Buffered(1))


def _dot(a, b):
    return jnp.dot(a, b, preferred_element_type=F32)


def _dot_nt(a, b):
    return lax.dot_general(a, b, (((1,), (1,)), ((), ())), preferred_element_type=F32)


def _silu(v):
    return v * jax.nn.sigmoid(v)


def _layer_norm(r, g, b):
    mu = jnp.mean(r, axis=-1, keepdims=True)
    xc = r - mu
    var = jnp.mean(xc * xc, axis=-1, keepdims=True)
    return xc * lax.rsqrt(var + LN_EPS) * g + b


def _mod_spec(nxt, nb):
    return pl.BlockSpec((1, 6, D_MODEL), lambda b, i: (jnp.where(i >= nxt, nb, b), 0, 0))


def _ada_kernel(c_ref, w_ref, b_ref, o_ref):
    s = _silu(c_ref[...]).astype(BF16)
    o_ref[0] = _dot(s, w_ref[0].astype(BF16)) + b_ref[0]


def _ada_mod(cond, ada_w, ada_b):
    rows = cond.shape[0]
    n = ada_w.shape[-1]
    tn = n // 4
    return pl.pallas_call(
        _ada_kernel,
        grid=(DEPTH, n // tn),
        in_specs=[pl.BlockSpec((rows, D_MODEL), lambda l, j: (0, 0)),
                  pl.BlockSpec((1, D_MODEL, tn), lambda l, j: (l, 0, j)),
                  pl.BlockSpec((1, 1, tn), lambda l, j: (l, 0, j))],
        out_specs=pl.BlockSpec((1, rows, tn), lambda l, j: (l, 0, j)),
        out_shape=jax.ShapeDtypeStruct((DEPTH, rows, n), F32),
        name="ada_mod",
        compiler_params=_params(("parallel", "parallel")),
    )(cond, ada_w, ada_b.reshape(DEPTH, 1, n))


def _hyb_proj_kernel(h_ref, hp_ref, hn_ref, mod_ref, w_ref, wdt_ref, cw_ref, cb_ref, sel_ref,
                     o_ref, dt_ref, qt_ref, k_ref, vt_ref, kn_ref, ext_ref, *, nxt, nt):
    i = pl.program_id(1)
    m = mod_ref[0]
    shift, scale = m[0:1, :], m[1:2, :]
    seg_start = jnp.logical_or(i == 0, i == nxt)
    seg_end = jnp.logical_or(i == nxt - 1, i == nt - 1)
    um = h_ref[0] * (1.0 + scale) + shift
    u = um.astype(BF16)
    up = jnp.where(seg_start, 0.0, hp_ref[0] * (1.0 + scale) + shift)
    un = jnp.where(seg_end, 0.0, hn_ref[0] * (1.0 + scale) + shift)
    uext = jnp.concatenate([up, um, un], axis=0).astype(BF16)

    def xbc_proj(c):
        ext_ref[c % 2] = _dot(uext, w_ref[:, c * HYB_NCH:(c + 1) * HYB_NCH])

    n_ch = SSD_CONV_DIM // HYB_NCH
    xbc_proj(0)
    for c in range(n_ch):
        n0 = c * HYB_NCH
        if c + 1 < n_ch:
            xbc_proj(c + 1)
        acc = cb_ref[:, n0:n0 + HYB_NCH]
        for k in range(SSD_CONV_W):
            acc = acc + cw_ref[k:k + 1, n0:n0 + HYB_NCH] * ext_ref[c % 2, pl.ds(HALO_F32 - SSD_CONV_W // 2 + k, TM), :]
        o_ref[0, :, n0:n0 + HYB_NCH] = _silu(acc).astype(BF16)
    for n0 in range(SSD_CONV_DIM, HYB_SSD, HYB_NCH):
        o_ref[0, :, n0:n0 + HYB_NCH] = _dot(u, w_ref[:, n0:n0 + HYB_NCH]).astype(BF16)
    dt_ref[0] = _dot(u, wdt_ref[...])
    qt_ref[0] = _dot(u, w_ref[:, HYB_SSD:HYB_SSD + NA_WIDTH]).T.astype(BF16)
    kb = _dot(u, w_ref[:, HYB_SSD + NA_WIDTH:HYB_SSD + 2 * NA_WIDTH]).astype(BF16)
    k_ref[0] = kb
    kn_ref[0] = _dot(kb * kb, sel_ref[...])
    v_t = _dot(u, w_ref[:, HYB_SSD + 2 * NA_WIDTH:HYB_MAIN]).T
    for hh in range(NA_HEADS):
        vt_ref[0, hh, 0:NA_DH, :] = v_t[hh * NA_DH:(hh + 1) * NA_DH, :].astype(BF16)
        vt_ref[0, hh, NA_DH:NA_VROWS, :] = jnp.ones((NA_VROWS - NA_DH, TM), BF16)


def _hyb_proj(h, mod, w_main, w_dt, conv_w, conv_b, nxt):
    nb, t, _ = h.shape
    nt = t // TM
    pair_sel = jnp.asarray(np.kron(np.eye(NA_WIDTH // LANES, LANES, dtype=np.float32),
                                   np.ones((LANES, 1), np.float32)), BF16)
    hb = TM // HALO_F32
    last_hb = t // HALO_F32 - 1
    return pl.pallas_call(
        functools.partial(_hyb_proj_kernel, nxt=nxt, nt=nt),
        grid=(nb, nt),
        in_specs=[pl.BlockSpec((1, TM, D_MODEL), lambda b, i: (b, i, 0)),
                  pl.BlockSpec((1, HALO_F32, D_MODEL), lambda b, i: (b, jnp.maximum(i * hb - 1, 0), 0)),
                  pl.BlockSpec((1, HALO_F32, D_MODEL), lambda b, i: (b, jnp.minimum((i + 1) * hb, last_hb), 0)),
                  _mod_spec(nxt, nb),
                  _resident(w_main.shape),
                  _resident(w_dt.shape),
                  _resident(conv_w.shape),
                  _resident(conv_b.shape),
                  _resident(pair_sel.shape)],
        out_specs=[pl.BlockSpec((1, TM, HYB_SSD), lambda b, i: (b, i, 0)),
                   pl.BlockSpec((1, TM, 2 * LANES), lambda b, i: (b, i, 0)),
                   pl.BlockSpec((1, NA_WIDTH, TM), lambda b, i: (b, 0, i)),
                   pl.BlockSpec((1, TM, NA_WIDTH), lambda b, i: (b, i, 0)),
                   pl.BlockSpec((1, NA_HEADS, NA_VROWS, TM), lambda b, i: (b, 0, 0, i)),
                   pl.BlockSpec((1, TM, LANES), lambda b, i: (b, i, 0))],
        out_shape=[jax.ShapeDtypeStruct((nb, t, HYB_SSD), BF16),
                   jax.ShapeDtypeStruct((nb, t, 2 * LANES), F32),
                   jax.ShapeDtypeStruct((nb, NA_WIDTH, t), BF16),
                   jax.ShapeDtypeStruct((nb, t, NA_WIDTH), BF16),
                   jax.ShapeDtypeStruct((nb, NA_HEADS, NA_VROWS, t), BF16),
                   jax.ShapeDtypeStruct((nb, t, LANES), F32)],
        scratch_shapes=[pltpu.VMEM((2, TM + 2 * HALO_F32, HYB_NCH), F32)],
        name="hyb_proj",
        compiler_params=_params(("parallel", "parallel")),
    )(h, h, h, mod, w_main, w_dt, conv_w, conv_b, pair_sel)


def _ssd_kernel(uf_ref, ub_ref, dtf_ref, dtb_ref, arow_ref, brow_ref, dsk_ref, e_ref, yf_ref, yb_ref, st_ref):
    j = pl.program_id(1)

    @pl.when(j == 0)
    def _():
        st_ref[...] = jnp.zeros_like(st_ref)

    for bi in range(SSD_BPS):
        _ssd_chunk(0, bi, uf_ref, dtf_ref, arow_ref, brow_ref, dsk_ref, e_ref, yf_ref, st_ref)
        _ssd_chunk(1, bi, ub_ref, dtb_ref, arow_ref, brow_ref, dsk_ref, e_ref, yb_ref, st_ref)


def _ssd_chunk(d, bi, u_ref, dt_ref, arow_ref, brow_ref, dsk_ref, e_ref, y_ref, st_ref):
    q = SSD_CHUNK
    u = u_ref[bi].astype(F32)
    xs = u[:, :SSD_INNER]

    dtr = dt_ref[bi] + brow_ref[d]
    dtv = jnp.maximum(dtr, 0.0) + jnp.log1p(jnp.exp(-jnp.abs(dtr)))
    adt = dtv * arow_ref[d]
    ri = lax.broadcasted_iota(jnp.int32, (q, q), 0)
    ci = lax.broadcasted_iota(jnp.int32, (q, q), 1)
    tri = (ri >= ci) if d == 0 else (ri <= ci)
    cs = jnp.dot(tri.astype(F32), adt, precision=lax.Precision.HIGHEST, preferred_element_type=F32)
    cs_t = cs.T
    dt_t = dtv.T
    last = q - 1 if d == 0 else 0
    tot = cs[last:last + 1, :]
    tot_t = cs_t[:, last:last + 1]
    w_t = jnp.exp(tot_t - cs_t) * dt_t
    dec_row = jnp.dot(jnp.broadcast_to(jnp.exp(tot), (8, LANES)), e_ref[...],
                      precision=lax.Precision.HIGHEST, preferred_element_type=F32)[0:1]

    lane = lax.broadcasted_iota(jnp.int32, (q, LANES), 1)
    lo = lane < SSD_P
    dsk = dsk_ref[d]
    for g in range(SSD_G):
        bm = u[:, SSD_INNER + g * SSD_N:SSD_INNER + (g + 1) * SSD_N]
        cm = u[:, SSD_INNER + SSD_GN + g * SSD_N:SSD_INNER + SSD_GN + (g + 1) * SSD_N]
        cmb = cm.astype(BF16)
        cb = _dot_nt(cmb, bm.astype(BF16))
        bm_t = bm.T
        s_prev = st_ref[bi, d, g]
        y_off = _dot(cmb, s_prev.astype(BF16))
        s_parts = []
        for pr in range(SSD_R // 2):
            col0 = g * SSD_R * SSD_P + pr * LANES
            xs_pair = xs[:, col0:col0 + LANES]
            xsb = xs_pair.astype(BF16)
            zero = jnp.zeros_like(xsb)
            xs_bd = jnp.concatenate([jnp.where(lo, xsb, zero), jnp.where(lo, zero, xsb)], axis=0)
            gm, bw, colbs = [], [], []
            for sub in range(2):
                h = g * SSD_R + pr * 2 + sub
                colb = jnp.broadcast_to(cs[:, h:h + 1], (q, q))
                decay = jnp.exp(jnp.where(tri, colb - cs_t[h:h + 1, :], -jnp.inf))
                gm.append((cb * decay * dt_t[h:h + 1, :]).astype(BF16))
                bw.append((bm_t * w_t[h:h + 1, :]).astype(BF16))
                colbs.append(colb)
            y_diag = _dot(jnp.concatenate(gm, axis=1), xs_bd)
            e_col = jnp.exp(jnp.where(lo, colbs[0], colbs[1]))
            y_pair = y_diag + y_off[:, pr * LANES:(pr + 1) * LANES] * e_col + dsk[:, col0:col0 + LANES] * xs_pair
            y_ref[bi, :, col0:col0 + LANES] = y_pair.astype(BF16)
            s_parts.append(_dot(jnp.concatenate(bw, axis=1), xs_bd))
        g0 = g * SSD_R * SSD_P
        st_ref[bi, d, g] = s_prev * dec_row[:, g0:g0 + SSD_R * SSD_P] + jnp.concatenate(s_parts, axis=1)


def _ssd(pm, dt, arow, brow, dsk, expand, n_lat):
    nb, t, _ = pm.shape
    q = SSD_CHUNK
    nc = t // q
    nxc = n_lat // q

    def cf(j):
        return (j + nxc) % nc

    def cbk(j):
        return nc - 1 - j

    y_shape = jax.ShapeDtypeStruct((nb, t, SSD_INNER), BF16)
    bps = SSD_BPS
    assert nb % bps == 0
    return pl.pallas_call(
        _ssd_kernel,
        grid=(nb // bps, nc),
        in_specs=[pl.BlockSpec((bps, q, SSD_CONV_DIM), lambda b, j: (b, cf(j), 0)),
                  pl.BlockSpec((bps, q, SSD_CONV_DIM), lambda b, j: (b, cbk(j), 0)),
                  pl.BlockSpec((bps, q, LANES), lambda b, j: (b, cf(j), 0)),
                  pl.BlockSpec((bps, q, LANES), lambda b, j: (b, cbk(j), 1)),
                  _resident(arow.shape), _resident(brow.shape), _resident(dsk.shape), _resident(expand.shape)],
        out_specs=[pl.BlockSpec((bps, q, SSD_INNER), lambda b, j: (b, cf(j), 0)),
                   pl.BlockSpec((bps, q, SSD_INNER), lambda b, j: (b, cbk(j), 0))],
        out_shape=[y_shape, y_shape],
        scratch_shapes=[pltpu.VMEM((bps, 2, SSD_G, SSD_N, SSD_R * SSD_P), F32)],
        name="ssd_scan",
        compiler_params=_params(("parallel", "arbitrary")),
    )(pm, pm, dt, dt, arow, brow, dsk, expand)


def _na_kernel(qt_ref, k0_ref, k1_ref, k2_ref, kc_ref, v0_ref, v1_ref, v2_ref, vc_ref,
               n0_ref, n1_ref, n2_ref, nc_ref, bias_ref, bmax_ref, o_ref, *, nblk):
    i = pl.program_id(1)
    nq = NA_BLK_ROWS * GRID_W
    zeros = jnp.zeros((NA_DH, nq), BF16)

    def run(with_local):
        k_refs = ([k0_ref, k1_ref, k2_ref] if with_local else []) + [kc_ref]
        v_refs = ([v0_ref, v1_ref, v2_ref] if with_local else []) + [vc_ref]
        n_refs = ([n0_ref, n1_ref, n2_ref] if with_local else []) + [nc_ref]
        kmax = jnp.sqrt(NORM_SLACK * functools.reduce(
            jnp.maximum, [jnp.max(nr[0], axis=0, keepdims=True) for nr in n_refs]))

        def scores(hh):
            sl = slice((hh // 2) * LANES, (hh // 2 + 1) * LANES)
            qh = qt_ref[0, hh * NA_DH:(hh + 1) * NA_DH, :]
            qm = jnp.concatenate([qh, zeros] if hh % 2 == 0 else [zeros, qh], axis=0)
            sc = [_dot(kr[0, :, sl], qm) for kr in k_refs]
            if with_local:
                sc = [s + bias_ref[0, hh, jb * nq:(jb + 1) * nq, :] if jb < NA_KEY_BLKS else s
                      for jb, s in enumerate(sc)]
            return sc

        def attend(hh, s_cur, shift):
            o_t = functools.reduce(lambda a, b: a + b, [_dot(vr[0, hh], jnp.exp(s - shift).astype(BF16))
                                                          for s, vr in zip(s_cur, v_refs)])
            return o_t[:NA_DH, :], o_t[NA_DH:NA_DH + 1, :]

        outs, oks = [], []
        s_next = scores(0)
        for hh in range(NA_HEADS):
            s_cur = s_next
            if hh + 1 < NA_HEADS:
                s_next = scores(hh + 1)
            qh = qt_ref[0, hh * NA_DH:(hh + 1) * NA_DH, :].astype(F32)
            bound = jnp.sqrt(jnp.sum(qh * qh, axis=0, keepdims=True)) * kmax[:, hh // 2:hh // 2 + 1]
            if with_local:
                bound = bound + jnp.maximum(bmax_ref[hh][:, 0:1], 0.0)
            num, den = attend(hh, s_cur, bound)
            oks.append(jnp.logical_and(jnp.min(den) > SUM_LO, jnp.max(den) < SUM_HI))
            outs.append(num * (1.0 / den))
        o_ref[0] = jnp.concatenate(outs, axis=0).T.astype(BF16)

        for hh in range(NA_HEADS):
            @pl.when(jnp.logical_not(oks[hh]))
            def _(hh=hh):
                s_cur = scores(hh)
                mx = functools.reduce(jnp.maximum, [jnp.max(s, axis=0, keepdims=True) for s in s_cur])
                num, den = attend(hh, s_cur, mx)
                o_ref[0, :, hh * NA_DH:(hh + 1) * NA_DH] = (num * (1.0 / den)).T.astype(BF16)

    @pl.when(i < nblk)
    def _():
        run(True)

    @pl.when(i >= nblk)
    def _():
        run(False)


def _na_variants(rows):
    nblk = rows // NA_BLK_ROWS
    assert rows % NA_BLK_ROWS == 0 and nblk >= NA_KEY_BLKS
    keys, vid = [], []
    for b in range(nblk):
        ws = int(np.clip(b - 1, 0, nblk - NA_KEY_BLKS)) * NA_BLK_ROWS
        key = []
        for a in range(NA_BLK_ROWS):
            r = b * NA_BLK_ROWS + a
            r0 = int(np.clip(r - NA_WIN_ROWS // 2, 0, rows - NA_WIN_ROWS))
            assert 0 <= r0 - ws and r0 - ws + NA_WIN_ROWS <= NA_KEY_BLKS * NA_BLK_ROWS
            key.append((r0 - ws, r0 - r))
        key = tuple(key)
        if key not in keys:
            keys.append(key)
        vid.append(keys.index(key))
    return keys, vid


def _na_bias_table(rpb, variants):
    c = np.arange(GRID_W)[:, None]
    kc = np.arange(GRID_W)[None, :]
    c0 = np.clip(c - NA_WIN_COLS // 2, 0, GRID_W - NA_WIN_COLS)
    valid_c = (kc >= c0) & (kc < c0 + NA_WIN_COLS)
    dc = kc - c + NA_WIN_COLS - 1
    onehot = ((dc[None] == np.arange(2 * NA_WIN_COLS - 1)[:, None, None]) & valid_c[None]).astype(np.float32)
    tz = jnp.einsum('hrd,dck->hrck', rpb.astype(F32), jnp.asarray(onehot), precision=lax.Precision.HIGHEST)
    nkr = NA_KEY_BLKS * NA_BLK_ROWS
    ndr = 2 * NA_WIN_ROWS - 1
    tabs = []
    for key in variants:
        sel = np.zeros((NA_BLK_ROWS, nkr, ndr), np.float32)
        valid_r = np.zeros((NA_BLK_ROWS, nkr), bool)
        for a, (off, e) in enumerate(key):
            for w in range(NA_WIN_ROWS):
                sel[a, off + w, e + NA_WIN_ROWS - 1 + w] = 1.0
                valid_r[a, off + w] = True
        tab = jnp.einsum('ajr,hrck->hajck', jnp.asarray(sel), tz, precision=lax.Precision.HIGHEST)
        ok = jnp.asarray(valid_r[None, :, :, None, None] & valid_c[None, None, None, :, :])
        tab = jnp.where(ok, tab, -jnp.inf)
        tabs.append(tab.transpose(0, 2, 4, 1, 3).reshape(NA_HEADS, nkr * GRID_W, NA_BLK_ROWS * GRID_W))
    return jnp.stack(tabs)


def _na(qt, k, vt, kn, bias, bmax, vid, n_lat):
    nb, t, _ = k.shape
    n_ctx = t - n_lat
    nq = NA_BLK_ROWS * GRID_W
    nblk = n_lat // nq
    assert n_ctx % nq == 0 and n_lat % n_ctx == 0
    steps = t // nq
    cblk = n_lat // n_ctx

    def wb(i):
        return jnp.clip(i - 1, 0, nblk - NA_KEY_BLKS)

    def variant(i):
        v = jnp.int32(0)
        for blk in range(1, nblk):
            v = jnp.where(i == blk, vid[blk], v)
        return v

    kspec = [pl.BlockSpec((1, nq, NA_WIDTH), functools.partial(lambda b, i, j: (b, wb(i) + j, 0), j=j))
             for j in range(NA_KEY_BLKS)]
    vspec = [pl.BlockSpec((1, NA_HEADS, NA_VROWS, nq), functools.partial(lambda b, i, j: (b, 0, 0, wb(i) + j), j=j))
             for j in range(NA_KEY_BLKS)]
    nspec = [pl.BlockSpec((1, nq, LANES), functools.partial(lambda b, i, j: (b, wb(i) + j, 0), j=j))
             for j in range(NA_KEY_BLKS)]
    return pl.pallas_call(
        functools.partial(_na_kernel, nblk=nblk),
        grid=(nb, steps),
        in_specs=([pl.BlockSpec((1, NA_WIDTH, nq), lambda b, i: (b, 0, i))] + kspec
                  + [pl.BlockSpec((1, n_ctx, NA_WIDTH), lambda b, i: (b, cblk, 0))] + vspec
                  + [pl.BlockSpec((1, NA_HEADS, NA_VROWS, n_ctx), lambda b, i: (b, 0, 0, cblk))] + nspec
                  + [pl.BlockSpec((1, n_ctx, LANES), lambda b, i: (b, cblk, 0)),
                     pl.BlockSpec((1,) + bias.shape[1:], lambda b, i: (variant(i), 0, 0, 0)),
                     _resident(bmax.shape)]),
        out_specs=pl.BlockSpec((1, nq, NA_WIDTH), lambda b, i: (b, i, 0)),
        out_shape=jax.ShapeDtypeStruct((nb, t, NA_WIDTH), BF16),
        name="na_attn",
        compiler_params=_params(("parallel", "arbitrary")),
    )(qt, k, k, k, k, vt, vt, vt, vt, kn, kn, kn, kn, bias, bmax)


def _hyb_out_kernel(yf_ref, yb_ref, z_ref, a_ref, h_ref, mod_ref, nw_ref, wy_ref, wa_ref, g_ref, b_ref, o_ref):
    y = (yf_ref[0].astype(F32) + yb_ref[0].astype(F32)) * _silu(z_ref[0].astype(F32))
    y = y * lax.rsqrt(jnp.mean(y * y, axis=-1, keepdims=True) + RMS_EPS) * nw_ref[...]
    o = _dot(y.astype(BF16), wy_ref[...]) + _dot(a_ref[0], wa_ref[...])
    gate = mod_ref[0][2:3, :]
    o_ref[0] = _layer_norm(ALPHA * h_ref[0] + gate * o, g_ref[...], b_ref[...])


def _hyb_out(yf, yb, pm, att, h, mod, norm_w, wy, wa, ln_g, ln_b, nxt):
    nb, t, _ = h.shape
    nt = t // TM
    zcol = SSD_CONV_DIM // SSD_INNER
    return pl.pallas_call(
        _hyb_out_kernel,
        grid=(nb, nt),
        in_specs=[pl.BlockSpec((1, TM, SSD_INNER), lambda b, i: (b, i, 0)),
                  pl.BlockSpec((1, TM, SSD_INNER), lambda b, i: (b, i, 0)),
                  pl.BlockSpec((1, TM, SSD_INNER), lambda b, i: (b, i, zcol)),
                  pl.BlockSpec((1, TM, NA_WIDTH), lambda b, i: (b, i, 0)),
                  pl.BlockSpec((1, TM, D_MODEL), lambda b, i: (b, i, 0)),
                  _mod_spec(nxt, nb),
                  _resident(norm_w.shape), _resident(wy.shape), _resident(wa.shape),
                  _resident(ln_g.shape), _resident(ln_b.shape)],
        out_specs=pl.BlockSpec((1, TM, D_MODEL), lambda b, i: (b, i, 0)),
        out_shape=jax.ShapeDtypeStruct((nb, t, D_MODEL), F32),
        name="hyb_out",
        compiler_params=_params(("parallel", "parallel")),
    )(yf, yb, pm, att, h, mod, norm_w, wy, wa, ln_g, ln_b)


def _ffn_kernel(hm_ref, hp_ref, hn_ref, mod_ref, wup_ref, cw_ref, cb_ref, wdn_ref, g_ref, b_ref, p_ref, o_ref,
                *scratch, nxt, nt):
    per_tile = len(scratch) // FFN_BPS
    hv_ref = [scratch[bi * per_tile:bi * per_tile + 2] for bi in range(FFN_BPS)]
    hg_ref = [scratch[bi * per_tile + 2:bi * per_tile + 4] for bi in range(FFN_BPS)]
    acc_ref = [scratch[bi * per_tile + 4] for bi in range(FFN_BPS)]
    perm_ref = [scratch[bi * per_tile + 5] for bi in range(FFN_BPS)]
    i = pl.program_id(1)
    seg_start = jnp.logical_or(i == 0, i == nxt)
    seg_end = jnp.logical_or(i == nxt - 1, i == nt - 1)
    sl = HALO_F32
    ngrp = TM // sl
    ncb = D_MODEL // LANES
    sub_d = lax.broadcasted_iota(jnp.int32, (2 * sl, D_MODEL), 0)
    sub_c = lax.broadcasted_iota(jnp.int32, (sl, FFN_CH), 0)

    def load_tile(bi):
        m = mod_ref[bi]
        shift, scale = m[3:4, :], m[4:5, :]
        hm = hm_ref[bi]
        u_perm = _dot(p_ref[...], (hm * (1.0 + scale) + shift).astype(BF16)).astype(BF16)
        prev_row = jnp.where(seg_start, 0.0, hp_ref[bi, sl - 1:sl, :] * (1.0 + scale) + shift)
        next_row = jnp.where(seg_end, 0.0, hn_ref[bi, 0:1, :] * (1.0 + scale) + shift)
        halo = jnp.where(sub_d == 0, prev_row, jnp.where(sub_d == 1, next_row, 0.0))
        uext = jnp.concatenate([u_perm, halo.astype(BF16)], axis=0)
        return hm, uext

    def conv(ref, col0):
        w = cw_ref[:, col0:col0 + FFN_CH]
        before_first = jnp.where(sub_c == 0, ref[TM:TM + 1, :], pltpu.roll(ref[TM - sl:TM, :], 1, axis=0))
        after_last = jnp.where(sub_c == sl - 1, ref[TM + 1:TM + 2, :], pltpu.roll(ref[0:sl, :], sl - 1, axis=0))
        prev = jnp.concatenate([before_first, ref[0:TM - sl, :]], axis=0)
        nxt_ = jnp.concatenate([ref[sl:TM, :], after_last], axis=0)
        return w[0:1, :] * prev + w[1:2, :] * ref[0:TM, :] + w[2:3, :] * nxt_ + cb_ref[:, col0:col0 + FFN_CH]

    n_ch = D_FF // FFN_CH

    def up_proj(bi, uext, c):
        hv_ref[bi][c % 2][...] = _dot(uext, wup_ref[:, c * FFN_CH:(c + 1) * FFN_CH])
        hg_ref[bi][c % 2][...] = _dot(uext, wup_ref[:, D_FF + c * FFN_CH:D_FF + (c + 1) * FFN_CH])

    def chunk(bi, uext, c):
        v0 = c * FFN_CH
        if c + 1 < n_ch:
            up_proj(bi, uext, c + 1)
        act = (_silu(conv(hg_ref[bi][c % 2], D_FF + v0)) * conv(hv_ref[bi][c % 2], v0)).astype(BF16)
        contrib = _dot(act, wdn_ref[v0:v0 + FFN_CH, :])
        if c == 0:
            acc_ref[bi][...] = contrib
        else:
            acc_ref[bi][...] += contrib

    def finish(bi, hm):
        gate = mod_ref[bi][5:6, :]
        acc = acc_ref[bi][...]
        for cb in range(ncb):
            for a in range(ngrp):
                perm_ref[bi][cb, pl.ds(a, sl, stride=ngrp), :] = acc[a * sl:(a + 1) * sl, cb * LANES:(cb + 1) * LANES]
        f = jnp.concatenate([perm_ref[bi][cb] for cb in range(ncb)], axis=1)
        o_ref[bi] = _layer_norm(ALPHA * hm + gate * f, g_ref[...], b_ref[...])

    tiles = [load_tile(bi) for bi in range(FFN_BPS)]
    for bi, (hm, uext) in enumerate(tiles):
        up_proj(bi, uext, 0)
        for c in range(n_ch):
            chunk(bi, uext, c)
        finish(bi, hm)


def _ffn(h, mod, w_up, conv_w, conv_b, w_dn, ln_g, ln_b, nxt, n_out_tiles):
    nb, t, _ = h.shape
    nt = t // TM
    hb = TM // HALO_F32
    last_hb = t // HALO_F32 - 1
    bps = FFN_BPS
    assert nb % bps == 0
    ngrp = TM // HALO_F32
    src = (np.arange(TM) % HALO_F32) * ngrp + np.arange(TM) // HALO_F32
    perm = jnp.asarray(np.eye(TM, dtype=np.float32)[src], BF16)
    mod_spec = pl.BlockSpec((bps, 6, D_MODEL), lambda b, i: (jnp.where(i >= nxt, nb // bps, b), 0, 0))
    return pl.pallas_call(
        functools.partial(_ffn_kernel, nxt=nxt, nt=nt),
        grid=(nb // bps, n_out_tiles),
        in_specs=[pl.BlockSpec((bps, TM, D_MODEL), lambda b, i: (b, i, 0)),
                  pl.BlockSpec((bps, HALO_F32, D_MODEL), lambda b, i: (b, jnp.maximum(i * hb - 1, 0), 0)),
                  pl.BlockSpec((bps, HALO_F32, D_MODEL), lambda b, i: (b, jnp.minimum((i + 1) * hb, last_hb), 0)),
                  mod_spec,
                  _resident(w_up.shape), _resident(conv_w.shape), _resident(conv_b.shape),
                  _resident(w_dn.shape), _resident(ln_g.shape), _resident(ln_b.shape), _resident(perm.shape)],
        out_specs=pl.BlockSpec((bps, TM, D_MODEL), lambda b, i: (b, i, 0)),
        out_shape=jax.ShapeDtypeStruct((nb, n_out_tiles * TM, D_MODEL), F32),
        scratch_shapes=bps * ([pltpu.VMEM((TM + 2 * HALO_F32, FFN_CH), F32)] * 4
                              + [pltpu.VMEM((TM, D_MODEL), F32),
                                 pltpu.VMEM((D_MODEL // LANES, TM, LANES), F32)]),
        name="conv_ffn",
        compiler_params=_params(("parallel", "parallel")),
    )(h, h, h, mod, w_up, conv_w, conv_b, w_dn, ln_g, ln_b, perm)


def _diff_proj_kernel(h_ref, mod_ref, w_ref, cos_ref, sin_ref, cost_ref, sint_ref, qt_ref, k_ref, vt_ref):
    m = mod_ref[0]
    u = (h_ref[0] * (1.0 + m[1:2, :]) + m[0:1, :]).astype(BF16)
    half = LANES // 2
    for n0 in range(0, D_MODEL, HYB_NCH):
        y_t = _dot(u, w_ref[:, n0:n0 + HYB_NCH]).T
        for p0 in range(0, HYB_NCH, LANES):
            blk = y_t[p0:p0 + LANES, :]
            partner = jnp.concatenate([blk[half:, :], blk[:half, :]], axis=0)
            qt_ref[0, n0 + p0:n0 + p0 + LANES, :] = (blk * cost_ref[...] + partner * sint_ref[...]).astype(BF16)
    for n0 in range(0, D_MODEL, HYB_NCH):
        y = _dot(u, w_ref[:, D_MODEL + n0:D_MODEL + n0 + HYB_NCH])
        parts = []
        for p0 in range(0, HYB_NCH, LANES):
            yp = y[:, p0:p0 + LANES]
            parts.append(yp * cos_ref[...] + pltpu.roll(yp, half, axis=1) * sin_ref[...])
        k_ref[0, :, n0:n0 + HYB_NCH] = jnp.concatenate(parts, axis=1).astype(BF16)
    hd = 2 * DIFF_SUB
    for n0 in range(0, D_MODEL, HYB_NCH):
        y_t = _dot(u, w_ref[:, 2 * D_MODEL + n0:2 * D_MODEL + n0 + HYB_NCH]).T
        for p0 in range(0, HYB_NCH, hd):
            vt_ref[0, (n0 + p0) // hd, 0:hd, :] = y_t[p0:p0 + hd, :].astype(BF16)


def _diff_proj(h, mod, w, cos_t, sin_t, nxt):
    nb, t, _ = h.shape
    nt = t // TM
    return pl.pallas_call(
        _diff_proj_kernel,
        grid=(nb, nt),
        in_specs=[pl.BlockSpec((1, TM, D_MODEL), lambda b, i: (b, i, 0)),
                  _mod_spec(nxt, nb),
                  _resident(w.shape),
                  pl.BlockSpec((TM, LANES), lambda b, i: (i, 0)),
                  pl.BlockSpec((TM, LANES), lambda b, i: (i, 0)),
                  pl.BlockSpec((LANES, TM), lambda b, i: (0, i)),
                  pl.BlockSpec((LANES, TM), lambda b, i: (0, i))],
        out_specs=[pl.BlockSpec((1, D_MODEL, TM), lambda b, i: (b, 0, i)),
                   pl.BlockSpec((1, TM, D_MODEL), lambda b, i: (b, i, 0)),
                   pl.BlockSpec((1, DIFF_HEADS, DIFF_VROWS, TM), lambda b, i: (b, 0, 0, i))],
        out_shape=[jax.ShapeDtypeStruct((nb, D_MODEL, t), BF16),
                   jax.ShapeDtypeStruct((nb, t, D_MODEL), BF16),
                   jax.ShapeDtypeStruct((nb, DIFF_HEADS, DIFF_VROWS, t), BF16)],
        name="diff_proj",
        compiler_params=_params(("parallel", "parallel")),
    )(h, mod, w, cos_t, sin_t, cos_t.T, sin_t.T)


def _diff_attn_kernel(qt_ref, k_ref, vt_ref, lam_ref, sw_ref, o_ref, kmax_ref, *, n_lat, n_ctx, nxt, lam_init):
    i = pl.program_id(2)
    lp = lam_ref[...]
    lam = (jnp.exp(jnp.sum(lp[0:1, :] * lp[1:2, :], axis=1, keepdims=True))
           - jnp.exp(jnp.sum(lp[2:3, :] * lp[3:4, :], axis=1, keepdims=True)) + lam_init)
    hd = 2 * DIFF_SUB
    tq = qt_ref.shape[-1]
    row = lax.broadcasted_iota(jnp.int32, (hd, tq), 0)
    sub0 = (row % DIFF_SUB) < (DIFF_SUB // 2)
    zero = jnp.zeros((hd, tq), BF16)
    heads = range(DIFF_HPS)

    @pl.when(i == 0)
    def _():
        for hh in heads:
            kf = k_ref[0, :, hh * hd:(hh + 1) * hd].astype(F32)
            kmax_ref[hh] = jnp.sqrt(jnp.max(jnp.sum(kf * kf, axis=1, keepdims=True), axis=0, keepdims=True))

    qms, bounds = [], []
    for hh in heads:
        qt = qt_ref[0, hh * hd:(hh + 1) * hd, :]
        qms.append((jnp.where(sub0, qt, zero), jnp.where(sub0, zero, qt)))
        bounds.append([jnp.sqrt(jnp.sum(jnp.square(qm.astype(F32)), axis=0, keepdims=True)) * kmax_ref[hh]
                       for qm in qms[hh]])

    def finish(hh, acc, den):
        outs = [a * (1.0 / d) for a, d in zip(acc, den)]
        o = outs[0] - lam * outs[1]
        o = o * lax.rsqrt(jnp.mean(o * o, axis=0, keepdims=True) + RMS_EPS) * sw_ref[...] * (1.0 - lam_init)
        o_ref[0, :, hh * hd:(hh + 1) * hd] = o.T.astype(BF16)

    def run(k0, nk):
        chunks = [(k0 + c0, min(DIFF_KCH, nk - c0)) for c0 in range(0, nk, DIFF_KCH)]

        def attend(hh, shifted):
            def scores(ci):
                c0, n = chunks[ci]
                return [_dot(k_ref[0, c0:c0 + n, hh * hd:(hh + 1) * hd], qm) for qm in qms[hh]]

            s_next = scores(0)
            mx = [None, None]
            acc = [None, None]
            den = [None, None]
            for ci, (c0, n) in enumerate(chunks):
                s_cur = s_next
                if ci + 1 < len(chunks):
                    s_next = scores(ci + 1)
                vv = vt_ref[0, hh, :, c0:c0 + n]
                for sub in range(2):
                    if shifted:
                        p = jnp.exp2(s_cur[sub] - bounds[hh][sub])
                        pv = _dot(vv, p.astype(BF16))
                        ps = jnp.sum(p, axis=0, keepdims=True)
                        acc[sub] = pv if ci == 0 else acc[sub] + pv
                        den[sub] = ps if ci == 0 else den[sub] + ps
                    else:
                        cm = jnp.max(s_cur[sub], axis=0, keepdims=True)
                        m_new = cm if ci == 0 else jnp.maximum(mx[sub], cm)
                        p = jnp.exp2(s_cur[sub] - m_new)
                        pv = _dot(vv, p.astype(BF16))
                        ps = jnp.sum(p, axis=0, keepdims=True)
                        if ci > 0:
                            alpha = jnp.exp2(mx[sub] - m_new)
                            pv, ps = acc[sub] * alpha + pv, den[sub] * alpha + ps
                        acc[sub], den[sub], mx[sub] = pv, ps, m_new
            return acc, den

        oks = []
        for hh in heads:
            acc, den = attend(hh, True)
            sums = jnp.concatenate(den, axis=0)
            oks.append(jnp.logical_and(jnp.min(sums) > SUM_LO, jnp.max(sums) < SUM_HI))
            finish(hh, acc, den)

        for hh in heads:
            @pl.when(jnp.logical_not(oks[hh]))
            def _(hh=hh):
                finish(hh, *attend(hh, False))

    @pl.when(i < nxt)
    def _():
        run(0, n_lat + n_ctx)

    @pl.when(i >= nxt)
    def _():
        run(n_lat, n_ctx)


def _diff_attn(qt, k, vt, lam_p, subln_b, n_lat, lam_init, n_q_tiles, tq):
    nb, t, _ = k.shape
    n_ctx = t - n_lat
    nxt = n_lat // tq
    hd = 2 * DIFF_SUB
    return pl.pallas_call(
        functools.partial(_diff_attn_kernel, n_lat=n_lat, n_ctx=n_ctx, nxt=nxt, lam_init=lam_init),
        grid=(nb, DIFF_HEADS // DIFF_HPS, n_q_tiles),
        in_specs=[pl.BlockSpec((1, DIFF_HPS * hd, tq), lambda b, hh, i: (b, hh, i)),
                  pl.BlockSpec((1, t, DIFF_HPS * hd), lambda b, hh, i: (b, 0, hh)),
                  pl.BlockSpec((1, DIFF_HPS, DIFF_VROWS, t), lambda b, hh, i: (b, hh, 0, 0)),
                  _resident(lam_p.shape), _resident(subln_b.shape)],
        out_specs=pl.BlockSpec((1, tq, DIFF_HPS * hd), lambda b, hh, i: (b, i, hh)),
        out_shape=jax.ShapeDtypeStruct((nb, n_q_tiles * tq, D_MODEL), BF16),
        scratch_shapes=[pltpu.VMEM((DIFF_HPS, 1, 1), F32)],
        name="diff_attn",
        compiler_params=_params(("parallel", "parallel", "arbitrary")),
    )(qt, k, vt, lam_p, subln_b)


def _diff_out_kernel(a_ref, h_ref, mod_ref, w_ref, g_ref, b_ref, o_ref):
    o = _dot(a_ref[0], w_ref[...])
    gate = mod_ref[0][2:3, :]
    o_ref[0] = _layer_norm(ALPHA * h_ref[0] + gate * o, g_ref[...], b_ref[...])


def _diff_out(att, h, mod, w, ln_g, ln_b, nxt, n_tiles):
    nb, t, _ = h.shape
    return pl.pallas_call(
        _diff_out_kernel,
        grid=(nb, n_tiles),
        in_specs=[pl.BlockSpec((1, TM, D_MODEL), lambda b, i: (b, i, 0)),
                  pl.BlockSpec((1, TM, D_MODEL), lambda b, i: (b, i, 0)),
                  _mod_spec(nxt, nb),
                  _resident(w.shape), _resident(ln_g.shape), _resident(ln_b.shape)],
        out_specs=pl.BlockSpec((1, TM, D_MODEL), lambda b, i: (b, i, 0)),
        out_shape=jax.ShapeDtypeStruct((nb, n_tiles * TM, D_MODEL), F32),
        name="diff_out",
        compiler_params=_params(("parallel", "parallel")),
    )(att, h, mod, w, ln_g, ln_b)


def _rope_tables(n_lat, n_ctx):
    tok = np.arange(n_lat)
    row = (tok // GRID_W).astype(np.float32)
    col = (tok % GRID_W).astype(np.float32)
    n_freq = DIFF_SUB // 4
    inv = jnp.asarray(ROPE_BASE, F32) ** (-jnp.arange(n_freq, dtype=F32) / n_freq)
    ang = jnp.concatenate([jnp.asarray(row)[:, None] * inv, jnp.asarray(col)[:, None] * inv], axis=-1)
    cos = jnp.cos(ang)
    sin = jnp.sin(ang)
    cos_t = jnp.concatenate([cos, cos, cos, cos], axis=-1)
    sin_t = jnp.concatenate([-sin, -sin, sin, sin], axis=-1)
    cos_t = jnp.concatenate([cos_t, jnp.ones((n_ctx, LANES), F32)], axis=0)
    sin_t = jnp.concatenate([sin_t, jnp.zeros((n_ctx, LANES), F32)], axis=0)
    return cos_t, sin_t


def _diff_head_perm():
    half = DIFF_SUB // 2
    perm = []
    for hh in range(DIFF_HEADS):
        base = hh * 2 * DIFF_SUB
        for part in range(2):
            for sub in range(2):
                start = base + sub * DIFF_SUB + part * half
                perm.extend(range(start, start + half))
    return np.asarray(perm)


def kernel(x, c, ctx, c_ctx, ada_w, ada_b, ln1_g, ln1_b, ln2_g, ln2_b, ffn_w_up, ffn_conv_w, ffn_conv_b, ffn_w_down, hyb_w_in, ssd_conv_w, ssd_conv_b, ssd_a_log, ssd_dt_bias, ssd_d, ssd_norm_w, na_rpb, hyb_w_out, diff_w_in, diff_lambda, diff_subln_w, diff_w_out):
    nb, n_lat, d = x.shape
    n_ctx = ctx.shape[1]
    assert d == D_MODEL and n_lat % TM == 0 and n_ctx % TM == 0 and n_lat % (NA_WIN_ROWS * GRID_W) == 0
    t = n_lat + n_ctx
    nt = t // TM
    nxt = n_lat // TM

    h = jnp.concatenate([x, ctx], axis=1)
    cond_rows = -(-(nb + FFN_BPS) // 8) * 8
    cond = jnp.concatenate([c, jnp.broadcast_to(c_ctx[None, :], (FFN_BPS, d)),
                            jnp.zeros((cond_rows - nb - FFN_BPS, d), F32)], axis=0)
    mod_all = _ada_mod(cond, ada_w, ada_b).reshape(DEPTH, cond_rows, 6, d)
    cos_t, sin_t = _rope_tables(n_lat, n_ctx)
    perm = _diff_head_perm()
    na_variants, na_vid = _na_variants(n_lat // GRID_W)
    expand = jnp.asarray(np.kron(np.eye(LANES, SSD_H, dtype=np.float32), np.ones((1, SSD_P), np.float32)))

    for i in range(DEPTH):
        last = i == DEPTH - 1
        j = i // 2
        mod = mod_all[i]
        n_tiles = nxt if last else nt
        row = lambda v: v.reshape(1, -1)
        if i % 2 == 0:
            w_in = hyb_w_in[j]
            o_xbc = SSD_INNER
            o_dt = o_xbc + SSD_CONV_DIM
            o_q = o_dt + 2 * SSD_H
            w_main = jnp.concatenate([w_in[:, o_xbc:o_dt], w_in[:, :SSD_INNER],
                                      w_in[:, o_q:o_q + NA_WIDTH] * NA_DH ** -0.5,
                                      w_in[:, o_q + NA_WIDTH:]], axis=1).astype(BF16)
            w_dt = jnp.zeros((d, 2 * LANES), F32)
            w_dt = w_dt.at[:, :SSD_H].set(w_in[:, o_dt:o_dt + SSD_H])
            w_dt = w_dt.at[:, LANES:LANES + SSD_H].set(w_in[:, o_dt + SSD_H:o_q]).astype(BF16)
            pm, dt, na_qt, na_k, na_vt, na_kn = _hyb_proj(h, mod, w_main, w_dt, ssd_conv_w[j], row(ssd_conv_b[j]), nxt)

            pad = lambda v: jnp.zeros((2, 1, LANES), F32).at[:, 0, :SSD_H].set(v)
            arow = pad(-jnp.exp(ssd_a_log[j].astype(F32)))
            brow = pad(ssd_dt_bias[j].astype(F32))
            dsk = jnp.repeat(ssd_d[j].astype(F32), SSD_P, axis=-1).reshape(2, 1, SSD_INNER)
            yf, yb = _ssd(pm, dt, arow, brow, dsk, expand, n_lat)
            rpb_max = jnp.max(na_rpb[j].astype(F32).reshape(NA_HEADS, -1), axis=1)
            bmax = jnp.broadcast_to(rpb_max[:, None, None], (NA_HEADS, 1, LANES))
            att = _na(na_qt, na_k, na_vt, na_kn, _na_bias_table(na_rpb[j], na_variants), bmax, na_vid, n_lat)
            w_out = hyb_w_out[j].astype(BF16)
            h = _hyb_out(yf, yb, pm, att, h, mod, row(ssd_norm_w[j]), w_out[:SSD_INNER], w_out[SSD_INNER:],
                         row(ln1_g[i]), row(ln1_b[i]), nxt)
        else:
            lam_init = 0.8 - 0.6 * math.exp(-0.3 * i)
            w_in = diff_w_in[j]
            wq = w_in[:, :d][:, perm] * (DIFF_SUB ** -0.5 * LOG2E)
            wk = w_in[:, d:2 * d][:, perm]
            w_qkv = jnp.concatenate([wq, wk, w_in[:, 2 * d:]], axis=1).astype(BF16)
            d_qt, d_k, d_vt = _diff_proj(h, mod, w_qkv, cos_t, sin_t, nxt)
            subln_b = jnp.broadcast_to(diff_subln_w[j].astype(F32)[:, None], (2 * DIFF_SUB, TM))
            att = _diff_attn(d_qt, d_k, d_vt, diff_lambda[j].astype(F32), subln_b, n_lat, lam_init, n_tiles, TM)
            h = _diff_out(att, h, mod, diff_w_out[j].astype(BF16), row(ln1_g[i]), row(ln1_b[i]), nxt, n_tiles)
        h = _ffn(h, mod, ffn_w_up[i].astype(BF16), ffn_conv_w[i], row(ffn_conv_b[i]),
                 ffn_w_down[i].astype(BF16), row(ln2_g[i]), row(ln2_b[i]), nxt, n_tiles)
    return h
```

```python
import functools
import math

import numpy as np
import jax
import jax.numpy as jnp
from jax import lax
from jax.experimental import pallas as pl
from jax.experimental.pallas import tpu as pltpu

F32 = jnp.float32
BF16 = jnp.bfloat16

D_MODEL = 1024
DEPTH = 4
GRID_W = 64
SSD_P = 64
SSD_H = 16
SSD_G = 4
SSD_R = SSD_H // SSD_G
SSD_N = 128
SSD_INNER = SSD_H * SSD_P
SSD_GN = SSD_G * SSD_N
SSD_CONV_W = 5
SSD_CONV_DIM = SSD_INNER + 2 * SSD_GN
SSD_CHUNK = 128
SSD_BPS = 2
NA_HEADS = 8
NA_DH = 64
NA_WIDTH = NA_HEADS * NA_DH
NA_WIN_ROWS = 8
NA_WIN_COLS = 16
DIFF_HEADS = 8
DIFF_SUB = 64
ROPE_BASE = 10000.0
D_FF = 2816
FFN_CH = 256
FFN_BPS = 2
ALPHA = (2.0 * DEPTH) ** 0.25
LN_EPS = 1e-5
RMS_EPS = 1e-5

LANES = 128
TM = 256
HALO_F32 = 8
HALO_BF16 = 16
VMEM_LIMIT = 56 * 1024 * 1024
HYB_SSD = SSD_CONV_DIM + SSD_INNER
HYB_MAIN = HYB_SSD + 3 * NA_WIDTH
NA_VROWS = NA_DH + HALO_BF16
NA_BLK_ROWS = 4
NA_KEY_BLKS = 3
DIFF_VROWS = 2 * DIFF_SUB
DIFF_KCH = 1024
DIFF_HPS = 4
LOG2E = 1.4426950408889634
SUM_LO = 2.0 ** -90
SUM_HI = 2.0 ** 100
NORM_SLACK = 1.0 + 2.0 ** -5
HYB_NCH = 512


def _params(sem, vmem=VMEM_LIMIT):
    return pltpu.CompilerParams(dimension_semantics=sem, vmem_limit_bytes=vmem)


def _resident(shape):
    nd = len(shape)
    return pl.BlockSpec(shape, lambda *_: (0,) * nd, pipeline_mode=pl.Buffered(1))


def _dot(a, b):
    return jnp.dot(a, b, preferred_element_type=F32)


def _dot_nt(a, b):
    return lax.dot_general(a, b, (((1,), (1,)), ((), ())), preferred_element_type=F32)


def _silu(v):
    return v * jax.nn.sigmoid(v)


def _layer_norm(r, g, b):
    mu = jnp.mean(r, axis=-1, keepdims=True)
    xc = r - mu
    var = jnp.mean(xc * xc, axis=-1, keepdims=True)
    return xc * lax.rsqrt(var + LN_EPS) * g + b


def _mod_spec(nxt, nb):
    return pl.BlockSpec((1, 6, D_MODEL), lambda b, i: (jnp.where(i >= nxt, nb, b), 0, 0))


def _ada_kernel(c_ref, w_ref, b_ref, o_ref):
    s = _silu(c_ref[...]).astype(BF16)
    o_ref[0] = _dot(s, w_ref[0].astype(BF16)) + b_ref[0]


def _ada_mod(cond, ada_w, ada_b):
    rows = cond.shape[0]
    n = ada_w.shape[-1]
    tn = n // 4
    return pl.pallas_call(
        _ada_kernel,
        grid=(DEPTH, n // tn),
        in_specs=[pl.BlockSpec((rows, D_MODEL), lambda l, j: (0, 0)),
                  pl.BlockSpec((1, D_MODEL, tn), lambda l, j: (l, 0, j)),
                  pl.BlockSpec((1, 1, tn), lambda l, j: (l, 0, j))],
        out_specs=pl.BlockSpec((1, rows, tn), lambda l, j: (l, 0, j)),
        out_shape=jax.ShapeDtypeStruct((DEPTH, rows, n), F32),
        name="ada_mod",
        compiler_params=_params(("parallel", "parallel")),
    )(cond, ada_w, ada_b.reshape(DEPTH, 1, n))


def _hyb_proj_kernel(h_ref, hp_ref, hn_ref, mod_ref, w_ref, wdt_ref, cw_ref, cb_ref, sel_ref,
                     o_ref, dt_ref, qt_ref, k_ref, vt_ref, kn_ref, ext_ref, *, nxt, nt):
    i = pl.program_id(1)
    m = mod_ref[0]
    shift, scale = m[0:1, :], m[1:2, :]
    seg_start = jnp.logical_or(i == 0, i == nxt)
    seg_end = jnp.logical_or(i == nxt - 1, i == nt - 1)
    um = h_ref[0] * (1.0 + scale) + shift
    u = um.astype(BF16)
    up = jnp.where(seg_start, 0.0, hp_ref[0] * (1.0 + scale) + shift)
    un = jnp.where(seg_end, 0.0, hn_ref[0] * (1.0 + scale) + shift)
    uext = jnp.concatenate([up, um, un], axis=0).astype(BF16)

    def xbc_proj(c):
        ext_ref[c % 2] = _dot(uext, w_ref[:, c * HYB_NCH:(c + 1) * HYB_NCH])

    n_ch = SSD_CONV_DIM // HYB_NCH
    xbc_proj(0)
    for c in range(n_ch):
        n0 = c * HYB_NCH
        if c + 1 < n_ch:
            xbc_proj(c + 1)
        acc = cb_ref[:, n0:n0 + HYB_NCH]
        for k in range(SSD_CONV_W):
            acc = acc + cw_ref[k:k + 1, n0:n0 + HYB_NCH] * ext_ref[c % 2, pl.ds(HALO_F32 - SSD_CONV_W // 2 + k, TM), :]
        o_ref[0, :, n0:n0 + HYB_NCH] = _silu(acc).astype(BF16)
    for n0 in range(SSD_CONV_DIM, HYB_SSD, HYB_NCH):
        o_ref[0, :, n0:n0 + HYB_NCH] = _dot(u, w_ref[:, n0:n0 + HYB_NCH]).astype(BF16)
    dt_ref[0] = _dot(u, wdt_ref[...])
    qt_ref[0] = _dot(u, w_ref[:, HYB_SSD:HYB_SSD + NA_WIDTH]).T.astype(BF16)
    kk = _dot(u, w_ref[:, HYB_SSD + NA_WIDTH:HYB_SSD + 2 * NA_WIDTH])
    k_ref[0] = kk.astype(BF16)
    kn_ref[0] = _dot((kk * kk).astype(BF16), sel_ref[...])
    v_t = _dot(u, w_ref[:, HYB_SSD + 2 * NA_WIDTH:HYB_MAIN]).T
    for hh in range(NA_HEADS):
        vt_ref[0, hh, 0:NA_DH, :] = v_t[hh * NA_DH:(hh + 1) * NA_DH, :].astype(BF16)
        vt_ref[0, hh, NA_DH:NA_VROWS, :] = jnp.ones((NA_VROWS - NA_DH, TM), BF16)


def _hyb_proj(h, mod, w_main, w_dt, conv_w, conv_b, nxt):
    nb, t, _ = h.shape
    nt = t // TM
    pair_sel = jnp.asarray(np.kron(np.eye(NA_WIDTH // LANES, LANES, dtype=np.float32),
                                   np.ones((LANES, 1), np.float32)), BF16)
    hb = TM // HALO_F32
    last_hb = t // HALO_F32 - 1
    return pl.pallas_call(
        functools.partial(_hyb_proj_kernel, nxt=nxt, nt=nt),
        grid=(nb, nt),
        in_specs=[pl.BlockSpec((1, TM, D_MODEL), lambda b, i: (b, i, 0)),
                  pl.BlockSpec((1, HALO_F32, D_MODEL), lambda b, i: (b, jnp.maximum(i * hb - 1, 0), 0)),
                  pl.BlockSpec((1, HALO_F32, D_MODEL), lambda b, i: (b, jnp.minimum((i + 1) * hb, last_hb), 0)),
                  _mod_spec(nxt, nb),
                  _resident(w_main.shape),
                  _resident(w_dt.shape),
                  _resident(conv_w.shape),
                  _resident(conv_b.shape),
                  _resident(pair_sel.shape)],
        out_specs=[pl.BlockSpec((1, TM, HYB_SSD), lambda b, i: (b, i, 0)),
                   pl.BlockSpec((1, TM, 2 * LANES), lambda b, i: (b, i, 0)),
                   pl.BlockSpec((1, NA_WIDTH, TM), lambda b, i: (b, 0, i)),
                   pl.BlockSpec((1, TM, NA_WIDTH), lambda b, i: (b, i, 0)),
                   pl.BlockSpec((1, NA_HEADS, NA_VROWS, TM), lambda b, i: (b, 0, 0, i)),
                   pl.BlockSpec((1, TM, LANES), lambda b, i: (b, i, 0))],
        out_shape=[jax.ShapeDtypeStruct((nb, t, HYB_SSD), BF16),
                   jax.ShapeDtypeStruct((nb, t, 2 * LANES), F32),
                   jax.ShapeDtypeStruct((nb, NA_WIDTH, t), BF16),
                   jax.ShapeDtypeStruct((nb, t, NA_WIDTH), BF16),
                   jax.ShapeDtypeStruct((nb, NA_HEADS, NA_VROWS, t), BF16),
                   jax.ShapeDtypeStruct((nb, t, LANES), F32)],
        scratch_shapes=[pltpu.VMEM((2, TM + 2 * HALO_F32, HYB_NCH), F32)],
        name="hyb_proj",
        compiler_params=_params(("parallel", "parallel")),
    )(h, h, h, mod, w_main, w_dt, conv_w, conv_b, pair_sel)


def _ssd_kernel(uf_ref, ub_ref, dtf_ref, dtb_ref, arow_ref, brow_ref, dsk_ref, e_ref, yf_ref, yb_ref, st_ref):
    j = pl.program_id(1)

    @pl.when(j == 0)
    def _():
        st_ref[...] = jnp.zeros_like(st_ref)

    for bi in range(SSD_BPS):
        _ssd_chunk(0, bi, uf_ref, dtf_ref, arow_ref, brow_ref, dsk_ref, e_ref, yf_ref, st_ref)
        _ssd_chunk(1, bi, ub_ref, dtb_ref, arow_ref, brow_ref, dsk_ref, e_ref, yb_ref, st_ref)


def _ssd_chunk(d, bi, u_ref, dt_ref, arow_ref, brow_ref, dsk_ref, e_ref, y_ref, st_ref):
    q = SSD_CHUNK
    u = u_ref[bi].astype(F32)
    xs = u[:, :SSD_INNER]

    dtr = dt_ref[bi] + brow_ref[d]
    dtv = jnp.maximum(dtr, 0.0) + jnp.log1p(jnp.exp(-jnp.abs(dtr)))
    adt = dtv * arow_ref[d]
    ri = lax.broadcasted_iota(jnp.int32, (q, q), 0)
    ci = lax.broadcasted_iota(jnp.int32, (q, q), 1)
    tri = (ri >= ci) if d == 0 else (ri <= ci)
    cs = jnp.dot(tri.astype(F32), adt, precision=lax.Precision.HIGHEST, preferred_element_type=F32)
    cs_t = cs.T
    dt_t = dtv.T
    last = q - 1 if d == 0 else 0
    tot = cs[last:last + 1, :]
    tot_t = cs_t[:, last:last + 1]
    w_t = jnp.exp(tot_t - cs_t) * dt_t
    dec_row = jnp.dot(jnp.broadcast_to(jnp.exp(tot), (8, LANES)), e_ref[...],
                      precision=lax.Precision.HIGHEST, preferred_element_type=F32)[0:1]

    lane = lax.broadcasted_iota(jnp.int32, (q, LANES), 1)
    lo = lane < SSD_P
    dsk = dsk_ref[d]
    for g in range(SSD_G):
        bm = u[:, SSD_INNER + g * SSD_N:SSD_INNER + (g + 1) * SSD_N]
        cm = u[:, SSD_INNER + SSD_GN + g * SSD_N:SSD_INNER + SSD_GN + (g + 1) * SSD_N]
        cmb = cm.astype(BF16)
        cb = _dot_nt(cmb, bm.astype(BF16))
        bm_t = bm.T
        s_prev = st_ref[bi, d, g]
        y_off = _dot(cmb, s_prev.astype(BF16))
        s_parts = []
        for pr in range(SSD_R // 2):
            col0 = g * SSD_R * SSD_P + pr * LANES
            xs_pair = xs[:, col0:col0 + LANES]
            xsb = xs_pair.astype(BF16)
            zero = jnp.zeros_like(xsb)
            xs_bd = jnp.concatenate([jnp.where(lo, xsb, zero), jnp.where(lo, zero, xsb)], axis=0)
            gm, bw, colbs = [], [], []
            for sub in range(2):
                h = g * SSD_R + pr * 2 + sub
                colb = jnp.broadcast_to(cs[:, h:h + 1], (q, q))
                decay = jnp.exp(jnp.where(tri, colb - cs_t[h:h + 1, :], -jnp.inf))
                gm.append((cb * decay * dt_t[h:h + 1, :]).astype(BF16))
                bw.append((bm_t * w_t[h:h + 1, :]).astype(BF16))
                colbs.append(colb)
            y_diag = _dot(jnp.concatenate(gm, axis=1), xs_bd)
            e_col = jnp.exp(jnp.where(lo, colbs[0], colbs[1]))
            y_pair = y_diag + y_off[:, pr * LANES:(pr + 1) * LANES] * e_col + dsk[:, col0:col0 + LANES] * xs_pair
            y_ref[bi, :, col0:col0 + LANES] = y_pair.astype(BF16)
            s_parts.append(_dot(jnp.concatenate(bw, axis=1), xs_bd))
        g0 = g * SSD_R * SSD_P
        st_ref[bi, d, g] = s_prev * dec_row[:, g0:g0 + SSD_R * SSD_P] + jnp.concatenate(s_parts, axis=1)


def _ssd(pm, dt, arow, brow, dsk, expand, n_lat):
    nb, t, _ = pm.shape
    q = SSD_CHUNK
    nc = t // q
    nxc = n_lat // q

    def cf(j):
        return (j + nxc) % nc

    def cbk(j):
        return nc - 1 - j

    y_shape = jax.ShapeDtypeStruct((nb, t, SSD_INNER), BF16)
    bps = SSD_BPS
    assert nb % bps == 0
    return pl.pallas_call(
        _ssd_kernel,
        grid=(nb // bps, nc),
        in_specs=[pl.BlockSpec((bps, q, SSD_CONV_DIM), lambda b, j: (b, cf(j), 0)),
                  pl.BlockSpec((bps, q, SSD_CONV_DIM), lambda b, j: (b, cbk(j), 0)),
                  pl.BlockSpec((bps, q, LANES), lambda b, j: (b, cf(j), 0)),
                  pl.BlockSpec((bps, q, LANES), lambda b, j: (b, cbk(j), 1)),
                  _resident(arow.shape), _resident(brow.shape), _resident(dsk.shape), _resident(expand.shape)],
        out_specs=[pl.BlockSpec((bps, q, SSD_INNER), lambda b, j: (b, cf(j), 0)),
                   pl.BlockSpec((bps, q, SSD_INNER), lambda b, j: (b, cbk(j), 0))],
        out_shape=[y_shape, y_shape],
        scratch_shapes=[pltpu.VMEM((bps, 2, SSD_G, SSD_N, SSD_R * SSD_P), F32)],
        name="ssd_scan",
        compiler_params=_params(("parallel", "arbitrary")),
    )(pm, pm, dt, dt, arow, brow, dsk, expand)


def _na_kernel(qt_ref, k0_ref, k1_ref, k2_ref, kc_ref, v0_ref, v1_ref, v2_ref, vc_ref,
               n0_ref, n1_ref, n2_ref, nc_ref, bias_ref, bmax_ref, o_ref, *, nblk):
    i = pl.program_id(1)
    nq = NA_BLK_ROWS * GRID_W
    zeros = jnp.zeros((NA_DH, nq), BF16)

    def run(with_local):
        k_refs = ([k0_ref, k1_ref, k2_ref] if with_local else []) + [kc_ref]
        v_refs = ([v0_ref, v1_ref, v2_ref] if with_local else []) + [vc_ref]
        n_refs = ([n0_ref, n1_ref, n2_ref] if with_local else []) + [nc_ref]
        kmax = jnp.sqrt(NORM_SLACK * functools.reduce(
            jnp.maximum, [jnp.max(nr[0], axis=0, keepdims=True) for nr in n_refs]))

        def scores(hh):
            sl = slice((hh // 2) * LANES, (hh // 2 + 1) * LANES)
            qh = qt_ref[0, hh * NA_DH:(hh + 1) * NA_DH, :]
            qm = jnp.concatenate([qh, zeros] if hh % 2 == 0 else [zeros, qh], axis=0)
            sc = [_dot(kr[0, :, sl], qm) for kr in k_refs]
            if with_local:
                sc = [s + bias_ref[0, hh, jb * nq:(jb + 1) * nq, :] if jb < NA_KEY_BLKS else s
                      for jb, s in enumerate(sc)]
            return sc

        def attend(hh, s_cur, shift):
            o_t = functools.reduce(lambda a, b: a + b, [_dot(vr[0, hh], jnp.exp(s - shift).astype(BF16))
                                                          for s, vr in zip(s_cur, v_refs)])
            return o_t[:NA_DH, :], o_t[NA_DH:NA_DH + 1, :]

        outs, oks = [], []
        s_next = scores(0)
        for hh in range(NA_HEADS):
            s_cur = s_next
            if hh + 1 < NA_HEADS:
                s_next = scores(hh + 1)
            qh = qt_ref[0, hh * NA_DH:(hh + 1) * NA_DH, :].astype(F32)
            bound = jnp.sqrt(jnp.sum(qh * qh, axis=0, keepdims=True)) * kmax[:, hh // 2:hh // 2 + 1]
            if with_local:
                bound = bound + jnp.maximum(bmax_ref[hh][:, 0:1], 0.0)
            num, den = attend(hh, s_cur, bound)
            oks.append(jnp.logical_and(jnp.min(den) > SUM_LO, jnp.max(den) < SUM_HI))
            outs.append(num * (1.0 / den))
        o_ref[0] = jnp.concatenate(outs, axis=0).T.astype(BF16)

        for hh in range(NA_HEADS):
            @pl.when(jnp.logical_not(oks[hh]))
            def _(hh=hh):
                s_cur = scores(hh)
                mx = functools.reduce(jnp.maximum, [jnp.max(s, axis=0, keepdims=True) for s in s_cur])
                num, den = attend(hh, s_cur, mx)
                o_ref[0, :, hh * NA_DH:(hh + 1) * NA_DH] = (num * (1.0 / den)).T.astype(BF16)

    @pl.when(i < nblk)
    def _():
        run(True)

    @pl.when(i >= nblk)
    def _():
        run(False)


def _na_variants(rows):
    nblk = rows // NA_BLK_ROWS
    assert rows % NA_BLK_ROWS == 0 and nblk >= NA_KEY_BLKS
    keys, vid = [], []
    for b in range(nblk):
        ws = int(np.clip(b - 1, 0, nblk - NA_KEY_BLKS)) * NA_BLK_ROWS
        key = []
        for a in range(NA_BLK_ROWS):
            r = b * NA_BLK_ROWS + a
            r0 = int(np.clip(r - NA_WIN_ROWS // 2, 0, rows - NA_WIN_ROWS))
            assert 0 <= r0 - ws and r0 - ws + NA_WIN_ROWS <= NA_KEY_BLKS * NA_BLK_ROWS
            key.append((r0 - ws, r0 - r))
        key = tuple(key)
        if key not in keys:
            keys.append(key)
        vid.append(keys.index(key))
    return keys, vid


def _na_bias_table(rpb, variants):
    c = np.arange(GRID_W)[:, None]
    kc = np.arange(GRID_W)[None, :]
    c0 = np.clip(c - NA_WIN_COLS // 2, 0, GRID_W - NA_WIN_COLS)
    valid_c = (kc >= c0) & (kc < c0 + NA_WIN_COLS)
    dc = kc - c + NA_WIN_COLS - 1
    onehot = ((dc[None] == np.arange(2 * NA_WIN_COLS - 1)[:, None, None]) & valid_c[None]).astype(np.float32)
    tz = jnp.einsum('hrd,dck->hrck', rpb.astype(F32), jnp.asarray(onehot), precision=lax.Precision.HIGHEST)
    nkr = NA_KEY_BLKS * NA_BLK_ROWS
    ndr = 2 * NA_WIN_ROWS - 1
    tabs = []
    for key in variants:
        sel = np.zeros((NA_BLK_ROWS, nkr, ndr), np.float32)
        valid_r = np.zeros((NA_BLK_ROWS, nkr), bool)
        for a, (off, e) in enumerate(key):
            for w in range(NA_WIN_ROWS):
                sel[a, off + w, e + NA_WIN_ROWS - 1 + w] = 1.0
                valid_r[a, off + w] = True
        tab = jnp.einsum('ajr,hrck->hajck', jnp.asarray(sel), tz, precision=lax.Precision.HIGHEST)
        ok = jnp.asarray(valid_r[None, :, :, None, None] & valid_c[None, None, None, :, :])
        tab = jnp.where(ok, tab, -jnp.inf)
        tabs.append(tab.transpose(0, 2, 4, 1, 3).reshape(NA_HEADS, nkr * GRID_W, NA_BLK_ROWS * GRID_W))
    return jnp.stack(tabs)


def _na(qt, k, vt, kn, bias, bmax, vid, n_lat):
    nb, t, _ = k.shape
    n_ctx = t - n_lat
    nq = NA_BLK_ROWS * GRID_W
    nblk = n_lat // nq
    assert n_ctx % nq == 0 and n_lat % n_ctx == 0
    steps = t // nq
    cblk = n_lat // n_ctx

    def wb(i):
        return jnp.clip(i - 1, 0, nblk - NA_KEY_BLKS)

    def variant(i):
        v = jnp.int32(0)
        for blk in range(1, nblk):
            v = jnp.where(i == blk, vid[blk], v)
        return v

    kspec = [pl.BlockSpec((1, nq, NA_WIDTH), functools.partial(lambda b, i, j: (b, wb(i) + j, 0), j=j))
             for j in range(NA_KEY_BLKS)]
    vspec = [pl.BlockSpec((1, NA_HEADS, NA_VROWS, nq), functools.partial(lambda b, i, j: (b, 0, 0, wb(i) + j), j=j))
             for j in range(NA_KEY_BLKS)]
    nspec = [pl.BlockSpec((1, nq, LANES), functools.partial(lambda b, i, j: (b, wb(i) + j, 0), j=j))
             for j in range(NA_KEY_BLKS)]
    return pl.pallas_call(
        functools.partial(_na_kernel, nblk=nblk),
        grid=(nb, steps),
        in_specs=([pl.BlockSpec((1, NA_WIDTH, nq), lambda b, i: (b, 0, i))] + kspec
                  + [pl.BlockSpec((1, n_ctx, NA_WIDTH), lambda b, i: (b, cblk, 0))] + vspec
                  + [pl.BlockSpec((1, NA_HEADS, NA_VROWS, n_ctx), lambda b, i: (b, 0, 0, cblk))] + nspec
                  + [pl.BlockSpec((1, n_ctx, LANES), lambda b, i: (b, cblk, 0)),
                     pl.BlockSpec((1,) + bias.shape[1:], lambda b, i: (variant(i), 0, 0, 0)),
                     _resident(bmax.shape)]),
        out_specs=pl.BlockSpec((1, nq, NA_WIDTH), lambda b, i: (b, i, 0)),
        out_shape=jax.ShapeDtypeStruct((nb, t, NA_WIDTH), BF16),
        name="na_attn",
        compiler_params=_params(("parallel", "arbitrary")),
    )(qt, k, k, k, k, vt, vt, vt, vt, kn, kn, kn, kn, bias, bmax)


def _hyb_out_kernel(yf_ref, yb_ref, z_ref, a_ref, h_ref, mod_ref, nw_ref, wy_ref, wa_ref, g_ref, b_ref, o_ref):
    y = (yf_ref[0].astype(F32) + yb_ref[0].astype(F32)) * _silu(z_ref[0].astype(F32))
    y = y * lax.rsqrt(jnp.mean(y * y, axis=-1, keepdims=True) + RMS_EPS) * nw_ref[...]
    o = _dot(y.astype(BF16), wy_ref[...]) + _dot(a_ref[0], wa_ref[...])
    gate = mod_ref[0][2:3, :]
    o_ref[0] = _layer_norm(ALPHA * h_ref[0] + gate * o, g_ref[...], b_ref[...])


def _hyb_out(yf, yb, pm, att, h, mod, norm_w, wy, wa, ln_g, ln_b, nxt):
    nb, t, _ = h.shape
    nt = t // TM
    zcol = SSD_CONV_DIM // SSD_INNER
    return pl.pallas_call(
        _hyb_out_kernel,
        grid=(nb, nt),
        in_specs=[pl.BlockSpec((1, TM, SSD_INNER), lambda b, i: (b, i, 0)),
                  pl.BlockSpec((1, TM, SSD_INNER), lambda b, i: (b, i, 0)),
                  pl.BlockSpec((1, TM, SSD_INNER), lambda b, i: (b, i, zcol)),
                  pl.BlockSpec((1, TM, NA_WIDTH), lambda b, i: (b, i, 0)),
                  pl.BlockSpec((1, TM, D_MODEL), lambda b, i: (b, i, 0)),
                  _mod_spec(nxt, nb),
                  _resident(norm_w.shape), _resident(wy.shape), _resident(wa.shape),
                  _resident(ln_g.shape), _resident(ln_b.shape)],
        out_specs=pl.BlockSpec((1, TM, D_MODEL), lambda b, i: (b, i, 0)),
        out_shape=jax.ShapeDtypeStruct((nb, t, D_MODEL), F32),
        name="hyb_out",
        compiler_params=_params(("parallel", "parallel")),
    )(yf, yb, pm, att, h, mod, norm_w, wy, wa, ln_g, ln_b)


def _ffn_kernel(hm_ref, hp_ref, hn_ref, mod_ref, wup_ref, cw_ref, cb_ref, wdn_ref, g_ref, b_ref, p_ref, pt_ref, o_ref,
                *scratch, nxt, nt):
    per_tile = len(scratch) // FFN_BPS
    hv_ref = [scratch[bi * per_tile:bi * per_tile + 2] for bi in range(FFN_BPS)]
    hg_ref = [scratch[bi * per_tile + 2:bi * per_tile + 4] for bi in range(FFN_BPS)]
    acc_ref = [scratch[bi * per_tile + 4] for bi in range(FFN_BPS)]
    i = pl.program_id(1)
    seg_start = jnp.logical_or(i == 0, i == nxt)
    seg_end = jnp.logical_or(i == nxt - 1, i == nt - 1)
    sl = HALO_F32
    sub_d = lax.broadcasted_iota(jnp.int32, (2 * sl, D_MODEL), 0)
    sub_c = lax.broadcasted_iota(jnp.int32, (sl, FFN_CH), 0)

    def load_tile(bi):
        m = mod_ref[bi]
        shift, scale = m[3:4, :], m[4:5, :]
        hm = hm_ref[bi]
        u_perm = _dot(p_ref[...], (hm * (1.0 + scale) + shift).astype(BF16)).astype(BF16)
        prev_row = jnp.where(seg_start, 0.0, hp_ref[bi, sl - 1:sl, :] * (1.0 + scale) + shift)
        next_row = jnp.where(seg_end, 0.0, hn_ref[bi, 0:1, :] * (1.0 + scale) + shift)
        halo = jnp.where(sub_d == 0, prev_row, jnp.where(sub_d == 1, next_row, 0.0))
        uext = jnp.concatenate([u_perm, halo.astype(BF16)], axis=0)
        return hm, uext

    def conv(ref, col0):
        w = cw_ref[:, col0:col0 + FFN_CH]
        before_first = jnp.where(sub_c == 0, ref[TM:TM + 1, :], pltpu.roll(ref[TM - sl:TM, :], 1, axis=0))
        after_last = jnp.where(sub_c == sl - 1, ref[TM + 1:TM + 2, :], pltpu.roll(ref[0:sl, :], sl - 1, axis=0))
        prev = jnp.concatenate([before_first, ref[0:TM - sl, :]], axis=0)
        nxt_ = jnp.concatenate([ref[sl:TM, :], after_last], axis=0)
        return w[0:1, :] * prev + w[1:2, :] * ref[0:TM, :] + w[2:3, :] * nxt_ + cb_ref[:, col0:col0 + FFN_CH]

    n_ch = D_FF // FFN_CH

    def up_proj(bi, uext, c):
        hv_ref[bi][c % 2][...] = _dot(uext, wup_ref[:, c * FFN_CH:(c + 1) * FFN_CH])
        hg_ref[bi][c % 2][...] = _dot(uext, wup_ref[:, D_FF + c * FFN_CH:D_FF + (c + 1) * FFN_CH])

    def chunk(bi, uext, c):
        v0 = c * FFN_CH
        if c + 1 < n_ch:
            up_proj(bi, uext, c + 1)
        act = (_silu(conv(hg_ref[bi][c % 2], D_FF + v0)) * conv(hv_ref[bi][c % 2], v0)).astype(BF16)
        contrib = _dot(act, wdn_ref[v0:v0 + FFN_CH, :])
        if c == 0:
            acc_ref[bi][...] = contrib
        else:
            acc_ref[bi][...] += contrib

    def finish(bi, hm):
        gate = mod_ref[bi][5:6, :]
        f = _dot(pt_ref[...], acc_ref[bi][...].astype(BF16))
        o_ref[bi] = _layer_norm(ALPHA * hm + gate * f, g_ref[...], b_ref[...])

    tiles = [load_tile(bi) for bi in range(FFN_BPS)]
    for bi, (hm, uext) in enumerate(tiles):
        up_proj(bi, uext, 0)
        for c in range(n_ch):
            chunk(bi, uext, c)
        finish(bi, hm)


def _ffn(h, mod, w_up, conv_w, conv_b, w_dn, ln_g, ln_b, nxt, n_out_tiles):
    nb, t, _ = h.shape
    nt = t // TM
    hb = TM // HALO_F32
    last_hb = t // HALO_F32 - 1
    bps = FFN_BPS
    assert nb % bps == 0
    ngrp = TM // HALO_F32
    src = (np.arange(TM) % HALO_F32) * ngrp + np.arange(TM) // HALO_F32
    perm = jnp.asarray(np.eye(TM, dtype=np.float32)[src], BF16)
    mod_spec = pl.BlockSpec((bps, 6, D_MODEL), lambda b, i: (jnp.where(i >= nxt, nb // bps, b), 0, 0))
    return pl.pallas_call(
        functools.partial(_ffn_kernel, nxt=nxt, nt=nt),
        grid=(nb // bps, n_out_tiles),
        in_specs=[pl.BlockSpec((bps, TM, D_MODEL), lambda b, i: (b, i, 0)),
                  pl.BlockSpec((bps, HALO_F32, D_MODEL), lambda b, i: (b, jnp.maximum(i * hb - 1, 0), 0)),
                  pl.BlockSpec((bps, HALO_F32, D_MODEL), lambda b, i: (b, jnp.minimum((i + 1) * hb, last_hb), 0)),
                  mod_spec,
                  _resident(w_up.shape), _resident(conv_w.shape), _resident(conv_b.shape),
                  _resident(w_dn.shape), _resident(ln_g.shape), _resident(ln_b.shape), _resident(perm.shape),
                  _resident(perm.shape)],
        out_specs=pl.BlockSpec((bps, TM, D_MODEL), lambda b, i: (b, i, 0)),
        out_shape=jax.ShapeDtypeStruct((nb, n_out_tiles * TM, D_MODEL), F32),
        scratch_shapes=bps * ([pltpu.VMEM((TM + 2 * HALO_F32, FFN_CH), F32)] * 4
                              + [pltpu.VMEM((TM, D_MODEL), F32)]),
        name="conv_ffn",
        compiler_params=_params(("parallel", "parallel")),
    )(h, h, h, mod, w_up, conv_w, conv_b, w_dn, ln_g, ln_b, perm, perm.T)


def _diff_proj_kernel(h_ref, mod_ref, w_ref, cos_ref, sin_ref, cost_ref, sint_ref, qt_ref, k_ref, vt_ref):
    m = mod_ref[0]
    u = (h_ref[0] * (1.0 + m[1:2, :]) + m[0:1, :]).astype(BF16)
    half = LANES // 2
    for n0 in range(0, D_MODEL, HYB_NCH):
        y_t = _dot(u, w_ref[:, n0:n0 + HYB_NCH]).T
        for p0 in range(0, HYB_NCH, LANES):
            blk = y_t[p0:p0 + LANES, :]
            partner = jnp.concatenate([blk[half:, :], blk[:half, :]], axis=0)
            qt_ref[0, n0 + p0:n0 + p0 + LANES, :] = (blk * cost_ref[...] + partner * sint_ref[...]).astype(BF16)
    for n0 in range(0, D_MODEL, HYB_NCH):
        y = _dot(u, w_ref[:, D_MODEL + n0:D_MODEL + n0 + HYB_NCH])
        parts = []
        for p0 in range(0, HYB_NCH, LANES):
            yp = y[:, p0:p0 + LANES]
            parts.append(yp * cos_ref[...] + pltpu.roll(yp, half, axis=1) * sin_ref[...])
        k_ref[0, :, n0:n0 + HYB_NCH] = jnp.concatenate(parts, axis=1).astype(BF16)
    hd = 2 * DIFF_SUB
    for n0 in range(0, D_MODEL, HYB_NCH):
        y_t = _dot(u, w_ref[:, 2 * D_MODEL + n0:2 * D_MODEL + n0 + HYB_NCH]).T
        for p0 in range(0, HYB_NCH, hd):
            vt_ref[0, (n0 + p0) // hd, 0:hd, :] = y_t[p0:p0 + hd, :].astype(BF16)


def _diff_proj(h, mod, w, cos_t, sin_t, nxt):
    nb, t, _ = h.shape
    nt = t // TM
    return pl.pallas_call(
        _diff_proj_kernel,
        grid=(nb, nt),
        in_specs=[pl.BlockSpec((1, TM, D_MODEL), lambda b, i: (b, i, 0)),
                  _mod_spec(nxt, nb),
                  _resident(w.shape),
                  pl.BlockSpec((TM, LANES), lambda b, i: (i, 0)),
                  pl.BlockSpec((TM, LANES), lambda b, i: (i, 0)),
                  pl.BlockSpec((LANES, TM), lambda b, i: (0, i)),
                  pl.BlockSpec((LANES, TM), lambda b, i: (0, i))],
        out_specs=[pl.BlockSpec((1, D_MODEL, TM), lambda b, i: (b, 0, i)),
                   pl.BlockSpec((1, TM, D_MODEL), lambda b, i: (b, i, 0)),
                   pl.BlockSpec((1, DIFF_HEADS, DIFF_VROWS, TM), lambda b, i: (b, 0, 0, i))],
        out_shape=[jax.ShapeDtypeStruct((nb, D_MODEL, t), BF16),
                   jax.ShapeDtypeStruct((nb, t, D_MODEL), BF16),
                   jax.ShapeDtypeStruct((nb, DIFF_HEADS, DIFF_VROWS, t), BF16)],
        name="diff_proj",
        compiler_params=_params(("parallel", "parallel")),
    )(h, mod, w, cos_t, sin_t, cos_t.T, sin_t.T)


def _diff_attn_kernel(qt_ref, k_ref, vt_ref, lam_ref, sw_ref, o_ref, kmax_ref, *, n_lat, n_ctx, nxt, lam_init):
    i = pl.program_id(2)
    lp = lam_ref[...]
    lam = (jnp.exp(jnp.sum(lp[0:1, :] * lp[1:2, :], axis=1, keepdims=True))
           - jnp.exp(jnp.sum(lp[2:3, :] * lp[3:4, :], axis=1, keepdims=True)) + lam_init)
    hd = 2 * DIFF_SUB
    tq = qt_ref.shape[-1]
    row = lax.broadcasted_iota(jnp.int32, (hd, tq), 0)
    sub0 = (row % DIFF_SUB) < (DIFF_SUB // 2)
    zero = jnp.zeros((hd, tq), BF16)
    heads = range(DIFF_HPS)

    @pl.when(i == 0)
    def _():
        for hh in heads:
            kf = k_ref[0, :, hh * hd:(hh + 1) * hd].astype(F32)
            kmax_ref[hh] = jnp.sqrt(jnp.max(jnp.sum(kf * kf, axis=1, keepdims=True), axis=0, keepdims=True))

    qms, bounds = [], []
    for hh in heads:
        qt = qt_ref[0, hh * hd:(hh + 1) * hd, :]
        qms.append((jnp.where(sub0, qt, zero), jnp.where(sub0, zero, qt)))
        bounds.append([jnp.sqrt(jnp.sum(jnp.square(qm.astype(F32)), axis=0, keepdims=True)) * kmax_ref[hh]
                       for qm in qms[hh]])

    def finish(hh, acc, den):
        outs = [a * (1.0 / d) for a, d in zip(acc, den)]
        o = outs[0] - lam * outs[1]
        o = o * lax.rsqrt(jnp.mean(o * o, axis=0, keepdims=True) + RMS_EPS) * sw_ref[...] * (1.0 - lam_init)
        o_ref[0, :, hh * hd:(hh + 1) * hd] = o.T.astype(BF16)

    def run(k0, nk):
        chunks = [(k0 + c0, min(DIFF_KCH, nk - c0)) for c0 in range(0, nk, DIFF_KCH)]

        def attend(hh, shifted):
            def scores(ci):
                c0, n = chunks[ci]
                return [_dot(k_ref[0, c0:c0 + n, hh * hd:(hh + 1) * hd], qm) for qm in qms[hh]]

            s_next = scores(0)
            mx = [None, None]
            acc = [None, None]
            den = [None, None]
            for ci, (c0, n) in enumerate(chunks):
                s_cur = s_next
                if ci + 1 < len(chunks):
                    s_next = scores(ci + 1)
                vv = vt_ref[0, hh, :, c0:c0 + n]
                for sub in range(2):
                    if shifted:
                        p = jnp.exp2(s_cur[sub] - bounds[hh][sub])
                        pv = _dot(vv, p.astype(BF16))
                        ps = jnp.sum(p, axis=0, keepdims=True)
                        acc[sub] = pv if ci == 0 else acc[sub] + pv
                        den[sub] = ps if ci == 0 else den[sub] + ps
                    else:
                        cm = jnp.max(s_cur[sub], axis=0, keepdims=True)
                        m_new = cm if ci == 0 else jnp.maximum(mx[sub], cm)
                        p = jnp.exp2(s_cur[sub] - m_new)
                        pv = _dot(vv, p.astype(BF16))
                        ps = jnp.sum(p, axis=0, keepdims=True)
                        if ci > 0:
                            alpha = jnp.exp2(mx[sub] - m_new)
                            pv, ps = acc[sub] * alpha + pv, den[sub] * alpha + ps
                        acc[sub], den[sub], mx[sub] = pv, ps, m_new
            return acc, den

        oks = []
        for hh in heads:
            acc, den = attend(hh, True)
            sums = jnp.concatenate(den, axis=0)
            oks.append(jnp.logical_and(jnp.min(sums) > SUM_LO, jnp.max(sums) < SUM_HI))
            finish(hh, acc, den)

        for hh in heads:
            @pl.when(jnp.logical_not(oks[hh]))
            def _(hh=hh):
                finish(hh, *attend(hh, False))

    @pl.when(i < nxt)
    def _():
        run(0, n_lat + n_ctx)

    @pl.when(i >= nxt)
    def _():
        run(n_lat, n_ctx)


def _diff_attn(qt, k, vt, lam_p, subln_b, n_lat, lam_init, n_q_tiles, tq):
    nb, t, _ = k.shape
    n_ctx = t - n_lat
    nxt = n_lat // tq
    hd = 2 * DIFF_SUB
    return pl.pallas_call(
        functools.partial(_diff_attn_kernel, n_lat=n_lat, n_ctx=n_ctx, nxt=nxt, lam_init=lam_init),
        grid=(nb, DIFF_HEADS // DIFF_HPS, n_q_tiles),
        in_specs=[pl.BlockSpec((1, DIFF_HPS * hd, tq), lambda b, hh, i: (b, hh, i)),
                  pl.BlockSpec((1, t, DIFF_HPS * hd), lambda b, hh, i: (b, 0, hh)),
                  pl.BlockSpec((1, DIFF_HPS, DIFF_VROWS, t), lambda b, hh, i: (b, hh, 0, 0)),
                  _resident(lam_p.shape), _resident(subln_b.shape)],
        out_specs=pl.BlockSpec((1, tq, DIFF_HPS * hd), lambda b, hh, i: (b, i, hh)),
        out_shape=jax.ShapeDtypeStruct((nb, n_q_tiles * tq, D_MODEL), BF16),
        scratch_shapes=[pltpu.VMEM((DIFF_HPS, 1, 1), F32)],
        name="diff_attn",
        compiler_params=_params(("parallel", "parallel", "arbitrary")),
    )(qt, k, vt, lam_p, subln_b)


def _diff_out_kernel(a_ref, h_ref, mod_ref, w_ref, g_ref, b_ref, o_ref):
    o = _dot(a_ref[0], w_ref[...])
    gate = mod_ref[0][2:3, :]
    o_ref[0] = _layer_norm(ALPHA * h_ref[0] + gate * o, g_ref[...], b_ref[...])


def _diff_out(att, h, mod, w, ln_g, ln_b, nxt, n_tiles):
    nb, t, _ = h.shape
    return pl.pallas_call(
        _diff_out_kernel,
        grid=(nb, n_tiles),
        in_specs=[pl.BlockSpec((1, TM, D_MODEL), lambda b, i: (b, i, 0)),
                  pl.BlockSpec((1, TM, D_MODEL), lambda b, i: (b, i, 0)),
                  _mod_spec(nxt, nb),
                  _resident(w.shape), _resident(ln_g.shape), _resident(ln_b.shape)],
        out_specs=pl.BlockSpec((1, TM, D_MODEL), lambda b, i: (b, i, 0)),
        out_shape=jax.ShapeDtypeStruct((nb, n_tiles * TM, D_MODEL), F32),
        name="diff_out",
        compiler_params=_params(("parallel", "parallel")),
    )(att, h, mod, w, ln_g, ln_b)


def _rope_tables(n_lat, n_ctx):
    tok = np.arange(n_lat)
    row = (tok // GRID_W).astype(np.float32)
    col = (tok % GRID_W).astype(np.float32)
    n_freq = DIFF_SUB // 4
    inv = jnp.asarray(ROPE_BASE, F32) ** (-jnp.arange(n_freq, dtype=F32) / n_freq)
    ang = jnp.concatenate([jnp.asarray(row)[:, None] * inv, jnp.asarray(col)[:, None] * inv], axis=-1)
    cos = jnp.cos(ang)
    sin = jnp.sin(ang)
    cos_t = jnp.concatenate([cos, cos, cos, cos], axis=-1)
    sin_t = jnp.concatenate([-sin, -sin, sin, sin], axis=-1)
    cos_t = jnp.concatenate([cos_t, jnp.ones((n_ctx, LANES), F32)], axis=0)
    sin_t = jnp.concatenate([sin_t, jnp.zeros((n_ctx, LANES), F32)], axis=0)
    return cos_t, sin_t


def _diff_head_perm():
    half = DIFF_SUB // 2
    perm = []
    for hh in range(DIFF_HEADS):
        base = hh * 2 * DIFF_SUB
        for part in range(2):
            for sub in range(2):
                start = base + sub * DIFF_SUB + part * half
                perm.extend(range(start, start + half))
    return np.asarray(perm)


def kernel(x, c, ctx, c_ctx, ada_w, ada_b, ln1_g, ln1_b, ln2_g, ln2_b, ffn_w_up, ffn_conv_w, ffn_conv_b, ffn_w_down, hyb_w_in, ssd_conv_w, ssd_conv_b, ssd_a_log, ssd_dt_bias, ssd_d, ssd_norm_w, na_rpb, hyb_w_out, diff_w_in, diff_lambda, diff_subln_w, diff_w_out):
    nb, n_lat, d = x.shape
    n_ctx = ctx.shape[1]
    assert d == D_MODEL and n_lat % TM == 0 and n_ctx % TM == 0 and n_lat % (NA_WIN_ROWS * GRID_W) == 0
    t = n_lat + n_ctx
    nt = t // TM
    nxt = n_lat // TM

    h = jnp.concatenate([x, ctx], axis=1)
    cond_rows = -(-(nb + FFN_BPS) // 8) * 8
    cond = jnp.concatenate([c, jnp.broadcast_to(c_ctx[None, :], (FFN_BPS, d)),
                            jnp.zeros((cond_rows - nb - FFN_BPS, d), F32)], axis=0)
    mod_all = _ada_mod(cond, ada_w, ada_b).reshape(DEPTH, cond_rows, 6, d)
    cos_t, sin_t = _rope_tables(n_lat, n_ctx)
    perm = _diff_head_perm()
    na_variants, na_vid = _na_variants(n_lat // GRID_W)
    expand = jnp.asarray(np.kron(np.eye(LANES, SSD_H, dtype=np.float32), np.ones((1, SSD_P), np.float32)))

    for i in range(DEPTH):
        last = i == DEPTH - 1
        j = i // 2
        mod = mod_all[i]
        n_tiles = nxt if last else nt
        row = lambda v: v.reshape(1, -1)
        if i % 2 == 0:
            w_in = hyb_w_in[j]
            o_xbc = SSD_INNER
            o_dt = o_xbc + SSD_CONV_DIM
            o_q = o_dt + 2 * SSD_H
            w_main = jnp.concatenate([w_in[:, o_xbc:o_dt], w_in[:, :SSD_INNER],
                                      w_in[:, o_q:o_q + NA_WIDTH] * NA_DH ** -0.5,
                                      w_in[:, o_q + NA_WIDTH:]], axis=1).astype(BF16)
            w_dt = jnp.zeros((d, 2 * LANES), F32)
            w_dt = w_dt.at[:, :SSD_H].set(w_in[:, o_dt:o_dt + SSD_H])
            w_dt = w_dt.at[:, LANES:LANES + SSD_H].set(w_in[:, o_dt + SSD_H:o_q]).astype(BF16)
            pm, dt, na_qt, na_k, na_vt, na_kn = _hyb_proj(h, mod, w_main, w_dt, ssd_conv_w[j], row(ssd_conv_b[j]), nxt)

            pad = lambda v: jnp.zeros((2, 1, LANES), F32).at[:, 0, :SSD_H].set(v)
            arow = pad(-jnp.exp(ssd_a_log[j].astype(F32)))
            brow = pad(ssd_dt_bias[j].astype(F32))
            dsk = jnp.repeat(ssd_d[j].astype(F32), SSD_P, axis=-1).reshape(2, 1, SSD_INNER)
            yf, yb = _ssd(pm, dt, arow, brow, dsk, expand, n_lat)
            rpb_max = jnp.max(na_rpb[j].astype(F32).reshape(NA_HEADS, -1), axis=1)
            bmax = jnp.broadcast_to(rpb_max[:, None, None], (NA_HEADS, 1, LANES))
            att = _na(na_qt, na_k, na_vt, na_kn, _na_bias_table(na_rpb[j], na_variants), bmax, na_vid, n_lat)
            w_out = hyb_w_out[j].astype(BF16)
            h = _hyb_out(yf, yb, pm, att, h, mod, row(ssd_norm_w[j]), w_out[:SSD_INNER], w_out[SSD_INNER:],
                         row(ln1_g[i]), row(ln1_b[i]), nxt)
        else:
            lam_init = 0.8 - 0.6 * math.exp(-0.3 * i)
            w_in = diff_w_in[j]
            wq = w_in[:, :d][:, perm] * (DIFF_SUB ** -0.5 * LOG2E)
            wk = w_in[:, d:2 * d][:, perm]
            w_qkv = jnp.concatenate([wq, wk, w_in[:, 2 * d:]], axis=1).astype(BF16)
            d_qt, d_k, d_vt = _diff_proj(h, mod, w_qkv, cos_t, sin_t, nxt)
            subln_b = jnp.broadcast_to(diff_subln_w[j].astype(F32)[:, None], (2 * DIFF_SUB, TM))
            att = _diff_attn(d_qt, d_k, d_vt, diff_lambda[j].astype(F32), subln_b, n_lat, lam_init, n_tiles, TM)
            h = _diff_out(att, h, mod, diff_w_out[j].astype(BF16), row(ln1_g[i]), row(ln1_b[i]), nxt, n_tiles)
        h = _ffn(h, mod, ffn_w_up[i].astype(BF16), ffn_conv_w[i], row(ffn_conv_b[i]),
                 ffn_w_down[i].astype(BF16), row(ln2_g[i]), row(ln2_b[i]), nxt, n_tiles)
    return h
```
